```python
import math
import jax
import jax.numpy as jnp
from jax import lax
import numpy as np

D_MODEL = 2048
BATCH = 4
SEQ = 4096
DEPTH = 2

GRID_W = 64
CTX_LEN = 256
N_EVEN = (DEPTH + 1) // 2
N_ODD = DEPTH // 2
EPS = 1e-6

HY_WIDTH = D_MODEL // 2
HY_ORDER = 2
HY_SHORT = 3
HY_BANDS = 16
HY_EMB = 2 * HY_BANDS + 1
HY_FFN = 64
HY_TARGET = 1e-2
HY_FAST_PCT = 0.3
HY_SLOW_PCT = 1.5

S5_WIDTH = D_MODEL // 2
S5_GROUP = 16
S5_GROUPS = S5_WIDTH // S5_GROUP
S5_STATE = 64
S5_DT_MIN = 1e-3
S5_DT_MAX = 1e-1

EVEN_IN = (HY_ORDER + 2) * HY_WIDTH + 2 * S5_WIDTH
EVEN_MIX = HY_WIDTH + S5_WIDTH

DA_HEAD = 64
DA_HEADS = D_MODEL // (2 * DA_HEAD)
DA_QK = DA_HEADS * 2 * DA_HEAD
DA_V = DA_HEADS * 2 * DA_HEAD
ODD_IN = 2 * DA_QK + 2 * DA_V
Q_BLOCK = 128
ROPE_BASE = 10000.0

kernel_name = 'hybrid_hyena_s5_diffattn_dit'


def rms_norm(x, w):
    xf = x.astype(jnp.float32)
    y = xf * lax.rsqrt(jnp.mean(xf * xf, axis=-1, keepdims=True) + EPS)
    return (y * w.astype(jnp.float32)).astype(x.dtype)


def short_conv(u, w, b):
    n = u.shape[1]
    pad = HY_SHORT // 2
    up = jnp.pad(u, ((0, 0), (pad, pad), (0, 0)))
    return sum(up[:, j:j + n] * w[j] for j in range(HY_SHORT)) + b


def hyena_filters(n, w1, b1, w2, b2, w3, b3, freq):
    f32 = jnp.float32
    t = jnp.arange(n, dtype=f32)
    tn = t / n
    bands = jnp.linspace(1e-4, HY_BANDS - 1, HY_BANDS, dtype=f32)
    ang = (2.0 * math.pi / n) * t[:, None] * bands[None, :]
    feat = jnp.concatenate([tn[:, None], jnp.cos(ang), -jnp.sin(ang)], axis=-1)
    h = jnp.sin(freq[0].astype(f32) * (feat @ w1.astype(f32) + b1.astype(f32)))
    h = jnp.sin(freq[1].astype(f32) * (h @ w2.astype(f32) + b2.astype(f32)))
    h = (h @ w3.astype(f32) + b3.astype(f32)).reshape(n, HY_ORDER, 2, HY_WIDTH)
    deltas = jnp.abs(jnp.linspace(math.log(HY_TARGET) / HY_SLOW_PCT, math.log(HY_TARGET) / HY_FAST_PCT, HY_WIDTH, dtype=f32))
    h = h * jnp.exp(-tn[:, None] * deltas[None, :])[:, None, None, :]
    h_fwd = h[:, :, 0]
    h_bwd = h[1:, :, 1]
    l1 = jnp.sum(jnp.abs(h_fwd), axis=0) + jnp.sum(jnp.abs(h_bwd), axis=0)
    k = jnp.concatenate([h_fwd, jnp.zeros((1, HY_ORDER, HY_WIDTH), f32), h_bwd[::-1]], axis=0) / l1
    return jnp.fft.rfft(k, axis=0)


def fft_long_conv(u, kf, skip):
    n = u.shape[1]
    uf32 = u.astype(jnp.float32)
    uf = jnp.fft.rfft(uf32, n=2 * n, axis=1)
    y = jnp.fft.irfft(uf * kf[None], n=2 * n, axis=1)[:, :n]
    return y + uf32 * skip.astype(jnp.float32)


def hyena(proj, conv_w, conv_b, filt, skip):
    n = proj.shape[1]
    parts = jnp.split(short_conv(proj, conv_w, conv_b), HY_ORDER + 1, axis=-1)
    kf = hyena_filters(n, *filt)
    y = parts[0]
    for i in range(HY_ORDER):
        y = parts[i + 1].astype(jnp.float32) * fft_long_conv(y, kf[:, i], skip[i])
    return y.astype(proj.dtype)


def s5_discretise(a_re, a_im, log_dt, b_re, b_im):
    f32 = jnp.float32
    lam = lax.complex(a_re.astype(f32), a_im.astype(f32))
    dt = jnp.exp(log_dt.astype(f32))[:, None]
    lam_bar = jnp.exp(lam * dt)
    b = lax.complex(b_re.astype(f32), b_im.astype(f32))
    b_bar = ((lam_bar - 1.0) / lam)[..., None] * b
    return lam_bar, b_bar


def s5_scan(u, lam_bar, b_bar, s0, reverse):
    n = u.shape[1]
    bu = jnp.einsum('bngc,gpc->bngp', u.astype(jnp.complex64), b_bar)
    first = n - 1 if reverse else 0
    last = 0 if reverse else n - 1
    bu = bu.at[:, first].add(lam_bar[None] * s0)
    a = jnp.broadcast_to(lam_bar, (1,) + bu.shape[1:])

    def combine(e1, e2):
        a1, b1 = e1
        a2, b2 = e2
        return a2 * a1, a2 * b1 + b2

    _, states = lax.associative_scan(combine, (a, bu), reverse=reverse, axis=1)
    return states, states[:, last]


def s5_branch(u_ctx, u_lat, a_re, a_im, log_dt, b_re, b_im, c_re, c_im, d):
    f32 = jnp.float32
    bsz, n_ctx, _ = u_ctx.shape
    n_lat = u_lat.shape[1]
    uc = u_ctx.astype(f32)
    ul = u_lat.astype(f32)
    dd = d.astype(f32)
    y_ctx = uc * dd
    y_lat = ul * dd
    uc_g = uc.reshape(bsz, n_ctx, S5_GROUPS, S5_GROUP)
    ul_g = ul.reshape(bsz, n_lat, S5_GROUPS, S5_GROUP)
    s_zero = jnp.zeros((bsz, S5_GROUPS, S5_STATE), jnp.complex64)
    for direction, reverse in ((0, False), (1, True)):
        lam_bar, b_bar = s5_discretise(a_re[direction], a_im[direction], log_dt[direction], b_re[direction], b_im[direction])
        c_mat = lax.complex(c_re[direction].astype(f32), c_im[direction].astype(f32))
        st_c, s_fin = s5_scan(uc_g, lam_bar, b_bar, s_zero, reverse)
        st_l = s5_scan(ul_g, lam_bar, b_bar, s_fin, reverse)[0]
        y_ctx = y_ctx + jnp.real(jnp.einsum('bngp,gcp->bngc', st_c, c_mat)).reshape(bsz, n_ctx, S5_WIDTH)
        y_lat = y_lat + jnp.real(jnp.einsum('bngp,gcp->bngc', st_l, c_mat)).reshape(bsz, n_lat, S5_WIDTH)
    return y_ctx, y_lat


def s5_glu(y, w, b):
    g = jax.nn.gelu(y)
    return g * jax.nn.sigmoid(g @ w.astype(jnp.float32) + b.astype(jnp.float32))


def even_mixer(h_lat, h_ctx, in_w, out_w, conv_w, conv_b, w1, b1, w2, b2, w3, b3, freq, skip,
               a_re, a_im, log_dt, b_re, b_im, c_re, c_im, d, glu_w, glu_b):
    cuts = [(HY_ORDER + 1) * HY_WIDTH, (HY_ORDER + 2) * HY_WIDTH, (HY_ORDER + 2) * HY_WIDTH + S5_WIDTH]
    hp_l, hg_l, su_l, sg_l = jnp.split(h_lat @ in_w, cuts, axis=-1)
    hp_c, hg_c, su_c, sg_c = jnp.split(h_ctx @ in_w, cuts, axis=-1)
    filt = (w1, b1, w2, b2, w3, b3, freq)
    hy_l = hyena(hp_l, conv_w, conv_b, filt, skip)
    hy_c = hyena(hp_c, conv_w, conv_b, filt, skip)
    s5_c, s5_l = s5_branch(su_c, su_l, a_re, a_im, log_dt, b_re, b_im, c_re, c_im, d)
    s5_l = s5_glu(s5_l, glu_w, glu_b)
    s5_c = s5_glu(s5_c, glu_w, glu_b)

    def merge(hy, hg, s5, sg):
        mixed = jnp.concatenate([hy * jax.nn.silu(hg), s5.astype(hy.dtype) * jax.nn.silu(sg)], axis=-1)
        return mixed @ out_w

    return merge(hy_l, hg_l, s5_l, sg_l), merge(hy_c, hg_c, s5_c, sg_c)


def rope_2d(x):
    f32 = jnp.float32
    n = x.shape[1]
    rows = n // GRID_W
    row = jnp.broadcast_to(jnp.arange(rows, dtype=f32)[:, None], (rows, GRID_W)).reshape(n)
    col = jnp.broadcast_to(jnp.arange(GRID_W, dtype=f32)[None, :], (rows, GRID_W)).reshape(n)
    half = DA_HEAD // 2
    quarter = DA_HEAD // 4
    freqs = ROPE_BASE ** (-jnp.arange(quarter, dtype=f32) / quarter)
    xf = x.astype(f32)

    def rot(xa, pos):
        ang = pos[:, None] * freqs[None, :]
        cos = jnp.cos(ang)[None, :, None, None, :]
        sin = jnp.sin(ang)[None, :, None, None, :]
        x1, x2 = xa[..., :quarter], xa[..., quarter:]
        return jnp.concatenate([x1 * cos - x2 * sin, x1 * sin + x2 * cos], axis=-1)

    return jnp.concatenate([rot(xf[..., :half], row), rot(xf[..., half:], col)], axis=-1).astype(x.dtype)


def diff_attend(q, k, v, lam):
    s = jnp.einsum('bqhmd,bkhmd->bhmqk', q.astype(jnp.float32), k.astype(jnp.float32)) * (DA_HEAD ** -0.5)
    p = jax.nn.softmax(s, axis=-1)
    w = p[:, :, 0] - lam * p[:, :, 1]
    return jnp.einsum('bhqk,bkhe->bqhe', w, v.astype(jnp.float32))


def odd_mixer(h_lat, h_ctx, layer, need_ctx, in_w, out_w, q_norm, k_norm, lq1, lk1, lq2, lk2, subln_w):
    f32 = jnp.float32
    bsz, n_lat, _ = h_lat.shape
    n_ctx = h_ctx.shape[1]
    lam_init = 0.8 - 0.6 * math.exp(-0.3 * layer)
    lam = (jnp.exp(jnp.sum(lq1.astype(f32) * lk1.astype(f32)))
           - jnp.exp(jnp.sum(lq2.astype(f32) * lk2.astype(f32))) + lam_init)

    def qk_shape(t, n):
        return t.reshape(bsz, n, DA_HEADS, 2, DA_HEAD)

    def v_shape(t, n):
        return t.reshape(bsz, n, DA_HEADS, 2 * DA_HEAD)

    cuts = [DA_QK, 2 * DA_QK, 2 * DA_QK + DA_V]
    q_l, k_l, v_l, g_l = jnp.split(h_lat @ in_w, cuts, axis=-1)
    q_l = rope_2d(rms_norm(qk_shape(q_l, n_lat), q_norm))
    k_l = rope_2d(rms_norm(qk_shape(k_l, n_lat), k_norm))
    v_l = v_shape(v_l, n_lat)
    if need_ctx:
        q_c, k_c, v_c, g_c = jnp.split(h_ctx @ in_w, cuts, axis=-1)
    else:
        k_c, v_c = jnp.split(h_ctx @ in_w[:, DA_QK:2 * DA_QK + DA_V], [DA_QK], axis=-1)
    k_c = rms_norm(qk_shape(k_c, n_ctx), k_norm)
    v_c = v_shape(v_c, n_ctx)
    k_all = jnp.concatenate([k_c, k_l], axis=1)
    v_all = jnp.concatenate([v_c, v_l], axis=1)
    n_blk = n_lat // Q_BLOCK
    q_blocks = q_l.reshape(bsz, n_blk, Q_BLOCK, DA_HEADS, 2, DA_HEAD).transpose(1, 0, 2, 3, 4, 5)
    o_l = lax.map(lambda qb: diff_attend(qb, k_all, v_all, lam), q_blocks)
    o_l = o_l.transpose(1, 0, 2, 3, 4).reshape(bsz, n_lat, DA_HEADS, 2 * DA_HEAD)

    def post(o, g):
        o = rms_norm(o, subln_w) * (1.0 - lam_init)
        o = o.reshape(bsz, o.shape[1], DA_V).astype(g.dtype) * jax.nn.silu(g)
        return o @ out_w

    out_l = post(o_l, g_l)
    if need_ctx:
        q_c = rms_norm(qk_shape(q_c, n_ctx), q_norm)
        out_c = post(diff_attend(q_c, k_c, v_c, lam), g_c)
    else:
        out_c = None
    return out_l, out_c


def setup_inputs(seed: int = 0) -> dict:
    key = jax.random.key(seed)
    keys = iter(jax.random.split(key, 64))
    f32 = jnp.float32

    def nrm(shape, scale):
        return jax.random.normal(next(keys), shape, f32) * scale

    D = D_MODEL
    x = nrm((BATCH, SEQ, D), 1.0)
    c = nrm((BATCH, D), 1.0)
    ctx = nrm((BATCH, CTX_LEN, D), 1.0)
    c_ctx = nrm((D,), 1.0)
    mod_w = nrm((DEPTH, D, 3 * D), 0.5 * D ** -0.5)
    mod_b = nrm((DEPTH, 3 * D), 0.02)
    norm_w = 1.0 + nrm((DEPTH, D), 0.02)
    ev_in_w = nrm((N_EVEN, D, EVEN_IN), D ** -0.5)
    ev_out_w = nrm((N_EVEN, EVEN_MIX, D), EVEN_MIX ** -0.5)
    hy_conv_w = nrm((N_EVEN, HY_SHORT, (HY_ORDER + 1) * HY_WIDTH), HY_SHORT ** -0.5)
    hy_conv_b = nrm((N_EVEN, (HY_ORDER + 1) * HY_WIDTH), 0.02)
    hy_w1 = nrm((N_EVEN, HY_EMB, HY_FFN), HY_EMB ** -0.5)
    hy_b1 = nrm((N_EVEN, HY_FFN), 0.02)
    hy_w2 = nrm((N_EVEN, HY_FFN, HY_FFN), HY_FFN ** -0.5)
    hy_b2 = nrm((N_EVEN, HY_FFN), 0.02)
    hy_w3 = nrm((N_EVEN, HY_FFN, HY_ORDER * 2 * HY_WIDTH), HY_FFN ** -0.5)
    hy_b3 = nrm((N_EVEN, HY_ORDER * 2 * HY_WIDTH), 0.02)
    hy_freq = 1.0 + nrm((N_EVEN, 2, HY_FFN), 0.02)
    hy_skip = nrm((N_EVEN, HY_ORDER, HY_WIDTH), 0.5)
    s5_a_re = -0.5 + nrm((N_EVEN, 2, S5_GROUPS, S5_STATE), 0.01)
    s5_a_im = math.pi * jnp.arange(S5_STATE, dtype=f32) + nrm((N_EVEN, 2, S5_GROUPS, S5_STATE), 0.01)
    s5_log_dt = math.log(S5_DT_MIN) + jax.random.uniform(next(keys), (N_EVEN, 2, S5_GROUPS), f32) * (math.log(S5_DT_MAX) - math.log(S5_DT_MIN))
    s5_b_re = nrm((N_EVEN, 2, S5_GROUPS, S5_STATE, S5_GROUP), (2 * S5_GROUP) ** -0.5)
    s5_b_im = nrm((N_EVEN, 2, S5_GROUPS, S5_STATE, S5_GROUP), (2 * S5_GROUP) ** -0.5)
    s5_c_re = nrm((N_EVEN, 2, S5_GROUPS, S5_GROUP, S5_STATE), (2 * S5_STATE) ** -0.5)
    s5_c_im = nrm((N_EVEN, 2, S5_GROUPS, S5_GROUP, S5_STATE), (2 * S5_STATE) ** -0.5)
    s5_d = nrm((N_EVEN, S5_WIDTH), 1.0)
    s5_glu_w = nrm((N_EVEN, S5_WIDTH, S5_WIDTH), S5_WIDTH ** -0.5)
    s5_glu_b = nrm((N_EVEN, S5_WIDTH), 0.02)
    od_in_w = nrm((N_ODD, D, ODD_IN), D ** -0.5)
    od_out_w = nrm((N_ODD, DA_V, D), DA_V ** -0.5)
    da_q_norm = 1.0 + nrm((N_ODD, DA_HEAD), 0.02)
    da_k_norm = 1.0 + nrm((N_ODD, DA_HEAD), 0.02)
    da_lq1 = nrm((N_ODD, DA_HEAD), 0.1)
    da_lk1 = nrm((N_ODD, DA_HEAD), 0.1)
    da_lq2 = nrm((N_ODD, DA_HEAD), 0.1)
    da_lk2 = nrm((N_ODD, DA_HEAD), 0.1)
    da_subln = 1.0 + nrm((N_ODD, 2 * DA_HEAD), 0.02)
    return {'x': x, 'c': c, 'ctx': ctx, 'c_ctx': c_ctx,
            'mod_w': mod_w, 'mod_b': mod_b, 'norm_w': norm_w,
            'ev_in_w': ev_in_w, 'ev_out_w': ev_out_w,
            'hy_conv_w': hy_conv_w, 'hy_conv_b': hy_conv_b,
            'hy_w1': hy_w1, 'hy_b1': hy_b1, 'hy_w2': hy_w2, 'hy_b2': hy_b2, 'hy_w3': hy_w3, 'hy_b3': hy_b3,
            'hy_freq': hy_freq, 'hy_skip': hy_skip,
            's5_a_re': s5_a_re, 's5_a_im': s5_a_im, 's5_log_dt': s5_log_dt,
            's5_b_re': s5_b_re, 's5_b_im': s5_b_im, 's5_c_re': s5_c_re, 's5_c_im': s5_c_im,
            's5_d': s5_d, 's5_glu_w': s5_glu_w, 's5_glu_b': s5_glu_b,
            'od_in_w': od_in_w, 'od_out_w': od_out_w, 'da_q_norm': da_q_norm, 'da_k_norm': da_k_norm,
            'da_lq1': da_lq1, 'da_lk1': da_lk1, 'da_lq2': da_lq2, 'da_lk2': da_lk2, 'da_subln': da_subln}


def reference(x, c, ctx, c_ctx, mod_w, mod_b, norm_w, ev_in_w, ev_out_w, hy_conv_w, hy_conv_b,
              hy_w1, hy_b1, hy_w2, hy_b2, hy_w3, hy_b3, hy_freq, hy_skip,
              s5_a_re, s5_a_im, s5_log_dt, s5_b_re, s5_b_im, s5_c_re, s5_c_im, s5_d, s5_glu_w, s5_glu_b,
              od_in_w, od_out_w, da_q_norm, da_k_norm, da_lq1, da_lk1, da_lq2, da_lk2, da_subln):
    for layer in range(DEPTH):
        last = layer == DEPTH - 1
        i = layer // 2
        mod = jax.nn.silu(c) @ mod_w[layer] + mod_b[layer]
        mod_c = jax.nn.silu(c_ctx) @ mod_w[layer] + mod_b[layer]
        shift, scale, gate = jnp.split(mod, 3, axis=-1)
        shift_c, scale_c, gate_c = jnp.split(mod_c, 3, axis=-1)
        h_lat = rms_norm(x, norm_w[layer]) * (1.0 + scale[:, None, :]) + shift[:, None, :]
        h_ctx = rms_norm(ctx, norm_w[layer]) * (1.0 + scale_c) + shift_c
        if layer % 2 == 0:
            out_l, out_c = even_mixer(h_lat, h_ctx, ev_in_w[i], ev_out_w[i], hy_conv_w[i], hy_conv_b[i],
                                      hy_w1[i], hy_b1[i], hy_w2[i], hy_b2[i], hy_w3[i], hy_b3[i], hy_freq[i], hy_skip[i],
                                      s5_a_re[i], s5_a_im[i], s5_log_dt[i], s5_b_re[i], s5_b_im[i],
                                      s5_c_re[i], s5_c_im[i], s5_d[i], s5_glu_w[i], s5_glu_b[i])
        else:
            out_l, out_c = odd_mixer(h_lat, h_ctx, layer, not last, od_in_w[i], od_out_w[i],
                                     da_q_norm[i], da_k_norm[i], da_lq1[i], da_lk1[i], da_lq2[i], da_lk2[i], da_subln[i])
        x = x + gate[:, None, :] * out_l
        if not last:
            ctx = ctx + gate_c * out_c
    return x
```

```python
import functools
import math

import jax
import jax.numpy as jnp
from jax import lax
from jax.experimental import pallas as pl
from jax.experimental.pallas import tpu as pltpu

F32 = jnp.float32
BF16 = jnp.bfloat16
HIGHEST = lax.Precision.HIGHEST

EPS = 1e-6
ROW_TILE = 256
LANES = 128
SUBLANES = 8
VMEM_LIMIT = 56 * 1024 * 1024

HY_ORDER = 2
HY_BANDS = 16
HY_TARGET = 1e-2
HY_FAST_PCT = 0.3
HY_SLOW_PCT = 1.5
S5_GROUP = 16
S5_STATE = 64
DA_HEAD = 64
GRID_W = 64
ROPE_BASE = 10000.0

S5_CHUNK = 128
S5_PITCH = S5_CHUNK + 4


def _cparams(sem):
    return pltpu.CompilerParams(dimension_semantics=sem, vmem_limit_bytes=VMEM_LIMIT)


def _silu(x):
    return x * jax.nn.sigmoid(x)


def _dot(a, b):
    return jnp.dot(a, b, preferred_element_type=F32)


def _mod_kernel(c_ref, w_ref, b_ref, o_ref):
    a = _silu(c_ref[...])
    o_ref[0] = jnp.dot(a, w_ref[0], precision=HIGHEST, preferred_element_type=F32) + b_ref[0]


def _mod_vectors(cvec, mod_w, mod_b):
    depth, d, d3 = mod_w.shape
    tn = 1024
    return pl.pallas_call(
        _mod_kernel,
        out_shape=jax.ShapeDtypeStruct((depth, cvec.shape[0], d3), F32),
        grid=(depth, d3 // tn),
        in_specs=[
            pl.BlockSpec(cvec.shape, lambda l, j: (0, 0)),
            pl.BlockSpec((1, d, tn), lambda l, j: (l, 0, j)),
            pl.BlockSpec((1, 1, tn), lambda l, j: (l, 0, j)),
        ],
        out_specs=pl.BlockSpec((1, cvec.shape[0], tn), lambda l, j: (l, 0, j)),
        compiler_params=_cparams(("arbitrary", "arbitrary")),
        name="mod_vectors",
    )(cvec, mod_w, mod_b.reshape(depth, 1, d3))


def _normed(x, modv_ref, nw_ref, row, d):
    m = modv_ref[pl.ds(row, 1), :]
    shift = m[:, 0:d]
    scale = m[:, d:2 * d]
    ms = jnp.mean(x * x, axis=-1, keepdims=True)
    y = x * lax.rsqrt(ms + EPS) * nw_ref[...]
    return (y * (1.0 + scale) + shift).astype(BF16)


def _inproj0_kernel(x_ref, ctx_ref, modv_ref, nw_ref, w_ref, o_ref, *, nl_tiles, bsz, d):
    b = pl.program_id(1)
    i = pl.program_id(2)
    is_lat = i < nl_tiles
    x = jnp.where(is_lat, x_ref[0], ctx_ref[0])
    row = jnp.where(is_lat, b, bsz)
    h = _normed(x, modv_ref, nw_ref, row, d)
    o_ref[0] = _dot(h, w_ref[...]).astype(o_ref.dtype)


def _inproj0(x, ctx, modv, nw, w):
    bsz, n, d = x.shape
    nc = ctx.shape[1]
    tm = ROW_TILE
    nl_t, nc_t = n // tm, nc // tm
    nout = w.shape[1]
    tn = 2048
    kern = functools.partial(_inproj0_kernel, nl_tiles=nl_t, bsz=bsz, d=d)
    return pl.pallas_call(
        kern,
        out_shape=jax.ShapeDtypeStruct((bsz, n + nc, nout), BF16),
        grid=(nout // tn, bsz, nl_t + nc_t),
        in_specs=[
            pl.BlockSpec((1, tm, d), lambda j, b, i: (b, jnp.minimum(i, nl_t - 1), 0)),
            pl.BlockSpec((1, tm, d), lambda j, b, i: (b, jnp.maximum(i - nl_t, 0), 0)),
            pl.BlockSpec(modv.shape, lambda j, b, i: (0, 0)),
            pl.BlockSpec((1, d), lambda j, b, i: (0, 0)),
            pl.BlockSpec((d, tn), lambda j, b, i: (0, j)),
        ],
        out_specs=pl.BlockSpec((1, tm, tn), lambda j, b, i: (b, i, j)),
        compiler_params=_cparams(("arbitrary", "arbitrary", "arbitrary")),
        name="inproj_even",
    )(x, ctx, modv, nw, w)


def _swap_quarters(x):
    nlanes = x.shape[-1]
    lane = lax.broadcasted_iota(jnp.int32, x.shape, x.ndim - 1)
    fwd = pltpu.roll(x, nlanes - 16, axis=x.ndim - 1)
    bwd = pltpu.roll(x, 16, axis=x.ndim - 1)
    return jnp.where((lane & 16) == 0, fwd, bwd)


def _inproj1_kernel(x_ref, modv_ref, nw_ref, w_ref, qkw_ref, gm_ref, cos_ref, sin_ref, o_ref,
                    *, nl_tiles, bsz, d):
    j = pl.program_id(0)
    b = pl.program_id(1)
    i = pl.program_id(2)
    is_lat = i < nl_tiles
    row = jnp.where(is_lat, b, bsz)
    needed = jnp.logical_or(is_lat, jnp.logical_or(j == 1, j == 2))

    @pl.when(jnp.logical_not(needed))
    def _():
        o_ref[...] = jnp.zeros(o_ref.shape, o_ref.dtype)

    @pl.when(needed)
    def _():
        h = _normed(x_ref[0], modv_ref, nw_ref, row, d)
        acc = _dot(h, w_ref[...])

        @pl.when(j >= 2)
        def _():
            o_ref[0] = acc.astype(o_ref.dtype)

        @pl.when(j < 2)
        def _():
            sq = (acc * acc).astype(BF16)
            gm = gm_ref[...]
            width = gm.shape[0]
            ms = jnp.concatenate(
                [_dot(sq[:, t * width:(t + 1) * width], gm) for t in range(acc.shape[1] // width)], axis=1)
            wsel = jnp.where(j == 0, qkw_ref[0:1, :], qkw_ref[1:2, :])
            yn = acc * lax.rsqrt(ms + EPS) * wsel
            reps = acc.shape[1] // cos_ref.shape[1]
            cos = pltpu.repeat(cos_ref[...], reps, axis=1)
            sin = pltpu.repeat(sin_ref[...], reps, axis=1)
            o_ref[0] = (yn * cos + _swap_quarters(yn) * sin).astype(o_ref.dtype)


def _inproj1(xc, modv, nw, w, qkw, gm, cos_t, sin_t, n_lat):
    bsz, r, d = xc.shape
    tm = ROW_TILE
    nl_t = n_lat // tm
    nout = w.shape[1]
    tn = 2048
    kern = functools.partial(_inproj1_kernel, nl_tiles=nl_t, bsz=bsz, d=d)
    return pl.pallas_call(
        kern,
        out_shape=jax.ShapeDtypeStruct((bsz, r, nout), BF16),
        grid=(nout // tn, bsz, r // tm),
        in_specs=[
            pl.BlockSpec((1, tm, d), lambda j, b, i: (b, i, 0)),
            pl.BlockSpec(modv.shape, lambda j, b, i: (0, 0)),
            pl.BlockSpec((1, d), lambda j, b, i: (0, 0)),
            pl.BlockSpec((d, tn), lambda j, b, i: (0, j)),
            pl.BlockSpec(qkw.shape, lambda j, b, i: (0, 0)),
            pl.BlockSpec(gm.shape, lambda j, b, i: (0, 0)),
            pl.BlockSpec((tm, cos_t.shape[1]), lambda j, b, i: (i, 0)),
            pl.BlockSpec((tm, sin_t.shape[1]), lambda j, b, i: (i, 0)),
        ],
        out_specs=pl.BlockSpec((1, tm, tn), lambda j, b, i: (b, i, j)),
        compiler_params=_cparams(("arbitrary", "arbitrary", "arbitrary")),
        name="inproj_odd",
    )(xc, modv, nw, w, qkw, gm, cos_t, sin_t)


def _shortconv_kernel(p_ref, w_ref, b_ref, o_ref, *, n_lat):
    p = p_ref[0].astype(F32)
    r = p.shape[0]
    row = lax.broadcasted_iota(jnp.int32, p.shape, 0)
    prev = pltpu.roll(p, 1, axis=0)
    nxt = pltpu.roll(p, r - 1, axis=0)
    prev = jnp.where((row == 0) | (row == n_lat), 0.0, prev)
    nxt = jnp.where((row == n_lat - 1) | (row == r - 1), 0.0, nxt)
    w = w_ref[...]
    o_ref[0] = (prev * w[0:1] + p * w[1:2] + nxt * w[2:3] + b_ref[...]).astype(o_ref.dtype)


def _shortconv(p0, conv_w, conv_b, n_lat):
    bsz, r, _ = p0.shape
    width = conv_w.shape[1]
    tc = 512
    return pl.pallas_call(
        functools.partial(_shortconv_kernel, n_lat=n_lat),
        out_shape=jax.ShapeDtypeStruct((bsz, r, width), BF16),
        grid=(bsz, width // tc),
        in_specs=[
            pl.BlockSpec((1, r, tc), lambda b, j: (b, 0, j)),
            pl.BlockSpec((3, tc), lambda b, j: (0, j)),
            pl.BlockSpec((1, tc), lambda b, j: (0, j)),
        ],
        out_specs=pl.BlockSpec((1, r, tc), lambda b, j: (b, 0, j)),
        compiler_params=_cparams(("arbitrary", "arbitrary")),
        name="hyena_shortconv",
    )(p0, conv_w, conv_b.reshape(1, width))


def _filter_kernel(feat_ref, w1_ref, b1_ref, w2_ref, b2_ref, w3_ref, b3_ref, freq_ref, delta_ref,
                   h_ref, l1_ref):
    q = pl.program_id(0)
    ti = pl.program_id(1)
    feat = feat_ref[...]
    tt = feat.shape[0]
    z1 = jnp.dot(feat, w1_ref[...], precision=HIGHEST, preferred_element_type=F32) + b1_ref[...]
    h1 = jnp.sin(freq_ref[0:1, :] * z1)
    z2 = jnp.dot(h1, w2_ref[...], precision=HIGHEST, preferred_element_type=F32) + b2_ref[...]
    h2 = jnp.sin(freq_ref[1:2, :] * z2)
    z3 = jnp.dot(h2, w3_ref[...], precision=HIGHEST, preferred_element_type=F32) + b3_ref[...]
    tn = feat[:, 0:1]
    hf = z3 * jnp.exp(-tn * delta_ref[...])
    row = lax.broadcasted_iota(jnp.int32, hf.shape, 0) + ti * tt
    hf = jnp.where((row == 0) & (q % 2 == 1), 0.0, hf)
    h_ref[...] = hf

    @pl.when(ti == 0)
    def _():
        l1_ref[...] = jnp.zeros(l1_ref.shape, l1_ref.dtype)

    part = jnp.sum(jnp.abs(hf).reshape(tt // SUBLANES, SUBLANES, hf.shape[1]), axis=0)
    l1_ref[...] += part


def _hyena_filter_taps(n, w1, b1, w2, b2, w3, b3, freq, width):
    t = jnp.arange(n, dtype=F32)
    tn = t / n
    bands = jnp.linspace(1e-4, HY_BANDS - 1, HY_BANDS, dtype=F32)
    ang = (2.0 * math.pi / n) * t[:, None] * bands[None, :]
    feat = jnp.concatenate([tn[:, None], jnp.cos(ang), -jnp.sin(ang)], axis=-1)
    emb = feat.shape[1]
    feat = jnp.pad(feat, ((0, 0), (0, LANES - emb)))
    w1p = jnp.pad(w1.astype(F32), ((0, LANES - emb), (0, 0)))
    ffn = w1.shape[1]
    deltas = jnp.abs(jnp.linspace(math.log(HY_TARGET) / HY_SLOW_PCT, math.log(HY_TARGET) / HY_FAST_PCT,
                                  width, dtype=F32)).reshape(1, width)
    ncol = w3.shape[1]
    nq = ncol // width
    tt = min(512, n)
    taps, l1 = pl.pallas_call(
        _filter_kernel,
        out_shape=(jax.ShapeDtypeStruct((n, ncol), F32), jax.ShapeDtypeStruct((SUBLANES, ncol), F32)),
        grid=(nq, n // tt),
        in_specs=[
            pl.BlockSpec((tt, LANES), lambda q, i: (i, 0)),
            pl.BlockSpec((LANES, ffn), lambda q, i: (0, 0)),
            pl.BlockSpec((1, ffn), lambda q, i: (0, 0)),
            pl.BlockSpec((ffn, ffn), lambda q, i: (0, 0)),
            pl.BlockSpec((1, ffn), lambda q, i: (0, 0)),
            pl.BlockSpec((ffn, width), lambda q, i: (0, q)),
            pl.BlockSpec((1, width), lambda q, i: (0, q)),
            pl.BlockSpec((2, ffn), lambda q, i: (0, 0)),
            pl.BlockSpec((1, width), lambda q, i: (0, 0)),
        ],
        out_specs=(pl.BlockSpec((tt, width), lambda q, i: (i, q)),
                   pl.BlockSpec((SUBLANES, width), lambda q, i: (0, q))),
        compiler_params=_cparams(("arbitrary", "arbitrary")),
        name="hyena_filter_taps",
    )(feat, w1p, b1.reshape(1, ffn).astype(F32), w2.astype(F32), b2.reshape(1, ffn).astype(F32),
      w3.astype(F32), b3.reshape(1, ncol).astype(F32), freq.astype(F32), deltas)
    return taps, jnp.sum(l1, axis=0)


def _dft_tables(n):
    blk = 64
    hi = jnp.arange(n // blk, dtype=jnp.int32)[:, None]
    lo = jnp.arange(blk, dtype=jnp.int32)[:, None]
    other = jnp.arange(n, dtype=jnp.int32)[None, :]
    period = 4 * n
    unit = math.pi / (2 * n)

    def cs(m):
        a = (m % period).astype(F32) * unit
        return jnp.cos(a), jnp.sin(a)

    def combine(ca, sa, cb, sb):
        c = ca[:, None, :] * cb[None, :, :] - sa[:, None, :] * sb[None, :, :]
        s = sa[:, None, :] * cb[None, :, :] + ca[:, None, :] * sb[None, :, :]
        return c.reshape(n, n).astype(BF16), (-s).reshape(n, n).astype(BF16)

    fc, fs = combine(*cs(2 * blk * hi * other), *cs((2 * lo + 1) * other))
    tc, ts = combine(*cs((2 * other + 1) * blk * hi), *cs((2 * other + 1) * lo))
    return fc, fs, tc, ts


def _dft_fwd_kernel(fc_ref, fs_ref, z_ref, o_re_ref, o_im_ref):
    z = z_ref[...].astype(BF16)
    o_re_ref[...] = _dot(fc_ref[...], z)
    o_im_ref[...] = _dot(fs_ref[...], z)


def _filter_spectrum(fc, fs, taps):
    n, ncol = taps.shape
    tf = min(512, n)
    tcn = 512
    return pl.pallas_call(
        _dft_fwd_kernel,
        out_shape=(jax.ShapeDtypeStruct((n, ncol), F32), jax.ShapeDtypeStruct((n, ncol), F32)),
        grid=(ncol // tcn, n // tf),
        in_specs=[
            pl.BlockSpec((tf, n), lambda j, f: (f, 0)),
            pl.BlockSpec((tf, n), lambda j, f: (f, 0)),
            pl.BlockSpec((n, tcn), lambda j, f: (0, j)),
        ],
        out_specs=(pl.BlockSpec((tf, tcn), lambda j, f: (f, j)),
                   pl.BlockSpec((tf, tcn), lambda j, f: (f, j))),
        compiler_params=_cparams(("arbitrary", "arbitrary")),
        name="hyena_filter_spectrum",
    )(fc, fs, taps)


def _dft_fwd_mul_kernel(fc_ref, fs_ref, z_ref, kr_ref, ki_ref, pr_ref, pi_ref):
    z = z_ref[0]
    ur = _dot(fc_ref[...], z)
    ui = _dot(fs_ref[...], z)
    kr = kr_ref[...]
    ki = ki_ref[...]
    pr_ref[0] = (ur * kr - ui * ki).astype(pr_ref.dtype)
    pi_ref[0] = (ur * ki + ui * kr).astype(pi_ref.dtype)


def _dft_fwd_mul(fc, fs, z, kr, ki, *, n, row_blk, col_blk0):
    bsz = z.shape[0]
    width = kr.shape[1]
    tf = min(512, n)
    tcn = 512
    return pl.pallas_call(
        _dft_fwd_mul_kernel,
        out_shape=(jax.ShapeDtypeStruct((bsz, n, width), BF16), jax.ShapeDtypeStruct((bsz, n, width), BF16)),
        grid=(width // tcn, n // tf, bsz),
        in_specs=[
            pl.BlockSpec((tf, n), lambda j, f, b: (f, 0)),
            pl.BlockSpec((tf, n), lambda j, f, b: (f, 0)),
            pl.BlockSpec((1, n, tcn), lambda j, f, b: (b, row_blk, col_blk0 + j)),
            pl.BlockSpec((tf, tcn), lambda j, f, b: (f, j)),
            pl.BlockSpec((tf, tcn), lambda j, f, b: (f, j)),
        ],
        out_specs=(pl.BlockSpec((1, tf, tcn), lambda j, f, b: (b, f, j)),
                   pl.BlockSpec((1, tf, tcn), lambda j, f, b: (b, f, j))),
        compiler_params=_cparams(("arbitrary", "arbitrary", "arbitrary")),
        name="hyena_dft_fwd",
    )(fc, fs, z, kr, ki)


def _dft_inv_gate_kernel(tc_ref, ts_ref, pr_ref, pi_ref, z_ref, xg_ref, skip_ref, o_ref):
    y = _dot(tc_ref[...], pr_ref[0]) + _dot(ts_ref[...], pi_ref[0])
    z = z_ref[0].astype(F32)
    xg = xg_ref[0].astype(F32)
    o_ref[0] = (xg * (y + z * skip_ref[...])).astype(o_ref.dtype)


def _dft_inv_gate(tc, ts, pr, pi, z, xg, skip, *, n, z_row_blk, z_col_blk0, xg_row_blk, xg_col_blk0):
    bsz, _, width = pr.shape
    tt = min(512, n)
    tcn = 512
    zrb = z_row_blk * (n // tt)
    xrb = xg_row_blk * (n // tt)
    return pl.pallas_call(
        _dft_inv_gate_kernel,
        out_shape=jax.ShapeDtypeStruct((bsz, n, width), BF16),
        grid=(width // tcn, bsz, n // tt),
        in_specs=[
            pl.BlockSpec((tt, n), lambda j, b, i: (i, 0)),
            pl.BlockSpec((tt, n), lambda j, b, i: (i, 0)),
            pl.BlockSpec((1, n, tcn), lambda j, b, i: (b, 0, j)),
            pl.BlockSpec((1, n, tcn), lambda j, b, i: (b, 0, j)),
            pl.BlockSpec((1, tt, tcn), lambda j, b, i: (b, zrb + i, z_col_blk0 + j)),
            pl.BlockSpec((1, tt, tcn), lambda j, b, i: (b, xrb + i, xg_col_blk0 + j)),
            pl.BlockSpec((1, tcn), lambda j, b, i: (0, j)),
        ],
        out_specs=pl.BlockSpec((1, tt, tcn), lambda j, b, i: (b, i, j)),
        compiler_params=_cparams(("arbitrary", "arbitrary", "arbitrary")),
        name="hyena_dft_inv",
    )(tc, ts, pr, pi, z, xg, skip)


def _hyena_seq(sc, n, seq_row_blk, filt, skip, width):
    w1, b1, w2, b2, w3, b3, freq = filt
    taps, l1 = _hyena_filter_taps(n, w1, b1, w2, b2, w3, b3, freq, width)
    fc, fs, tc, ts = _dft_tables(n)
    sr, si = _filter_spectrum(fc, fs, taps)
    sr = sr.reshape(n, HY_ORDER, 2, width)
    si = si.reshape(n, HY_ORDER, 2, width)
    l1 = l1.reshape(HY_ORDER, 2, width).sum(axis=1)
    norm = (1.0 / n) / l1
    cb = width // 512
    y = None
    for i in range(HY_ORDER):
        kr = (sr[:, i, 0] + sr[:, i, 1]) * norm[i]
        ki = (si[:, i, 0] - si[:, i, 1]) * norm[i]
        if i == 0:
            z, zrb, zcb = sc, seq_row_blk, 0
        else:
            z, zrb, zcb = y, 0, 0
        pr, pi = _dft_fwd_mul(fc, fs, z, kr, ki, n=n, row_blk=zrb, col_blk0=zcb)
        y = _dft_inv_gate(tc, ts, pr, pi, z, sc, skip[i].reshape(1, width).astype(F32), n=n,
                          z_row_blk=zrb, z_col_blk0=zcb, xg_row_blk=seq_row_blk, xg_col_blk0=(i + 1) * cb)
    return y


def _s5_kernel(u_ref, bblk_ref, cblk_ref, lr_ref, li_ref, y_ref, st_ref, cr_ref, ci_ref, *, bsz, chunk):
    d = pl.program_id(0)
    s = pl.program_id(1)
    t_len = chunk
    ngh = bblk_ref.shape[1]
    kb = bblk_ref.shape[2]
    half = bblk_ref.shape[3] // 2
    nslab = half // LANES
    per_seq = ngh // 2

    @pl.when(s == 0)
    def _():
        cr_ref[...] = jnp.zeros(cr_ref.shape, F32)
        ci_ref[...] = jnp.zeros(ci_ref.shape, F32)

    for b in range(bsz):
        for gh in range(ngh):
            hsel, k = divmod(gh, per_seq)
            q = hsel * bsz + b
            bu = _dot(u_ref[b, :, gh * kb:(gh + 1) * kb], bblk_ref[0, gh])
            for lb in range(2 * nslab):
                st_ref[k * 2 * nslab + lb, q * S5_PITCH:q * S5_PITCH + t_len, :] = bu[:, lb * LANES:(lb + 1) * LANES]

    for k in range(per_seq):
        lr = lr_ref[0, :, k * half:(k + 1) * half]
        li = li_ref[0, :, k * half:(k + 1) * half]
        base = k * 2 * nslab

        def step(i, carry, base=base, lr=lr, li=li):
            sr, si = carry
            t = jnp.where(d == 0, i, t_len - 1 - i)
            rows = pl.ds(t, SUBLANES, stride=S5_PITCH)
            xr = jnp.concatenate([st_ref[base + lb, rows, :] for lb in range(nslab)], axis=1)
            xi = jnp.concatenate([st_ref[base + nslab + lb, rows, :] for lb in range(nslab)], axis=1)
            nr = lr * sr - li * si + xr
            ni = lr * si + li * sr + xi
            for lb in range(nslab):
                st_ref[base + lb, rows, :] = nr[:, lb * LANES:(lb + 1) * LANES]
                st_ref[base + nslab + lb, rows, :] = ni[:, lb * LANES:(lb + 1) * LANES]
            return nr, ni

        sr0 = cr_ref[:, k * half:(k + 1) * half]
        si0 = ci_ref[:, k * half:(k + 1) * half]
        sr1, si1 = lax.fori_loop(0, t_len, step, (sr0, si0), unroll=2)
        cr_ref[:, k * half:(k + 1) * half] = sr1
        ci_ref[:, k * half:(k + 1) * half] = si1

    for b in range(bsz):
        for gh in range(ngh):
            hsel, k = divmod(gh, per_seq)
            q = hsel * bsz + b
            st = jnp.concatenate(
                [st_ref[k * 2 * nslab + lb, q * S5_PITCH:q * S5_PITCH + t_len, :] for lb in range(2 * nslab)],
                axis=1).astype(BF16)
            y_ref[0, b, :, gh * kb:(gh + 1) * kb] = _dot(st, cblk_ref[0, gh])


def _s5_scan(p0, bblk, cblk, lr, li, *, n_lat, n_ctx, col_blk):
    bsz, r, _ = p0.shape
    t_len = S5_CHUNK
    nl, nc = n_lat // t_len, n_ctx // t_len
    ngh, kb, two_half = bblk.shape[1], bblk.shape[2], bblk.shape[3]
    width = ngh * kb

    def chunk_idx(d, s):
        fwd = jnp.where(s < nc, nl + s, s - nc)
        rev = jnp.where(s < nc, nl + nc - 1 - s, nl - 1 - (s - nc))
        return jnp.where(d == 0, fwd, rev)

    nrows = SUBLANES
    kern = functools.partial(_s5_kernel, bsz=bsz, chunk=t_len)
    return pl.pallas_call(
        kern,
        out_shape=jax.ShapeDtypeStruct((2, bsz, r, width), F32),
        grid=(2, nl + nc),
        in_specs=[
            pl.BlockSpec((bsz, t_len, width), lambda d, s: (0, chunk_idx(d, s), col_blk)),
            pl.BlockSpec((1,) + bblk.shape[1:], lambda d, s: (d, 0, 0, 0)),
            pl.BlockSpec((1,) + cblk.shape[1:], lambda d, s: (d, 0, 0, 0)),
            pl.BlockSpec((1,) + lr.shape[1:], lambda d, s: (d, 0, 0)),
            pl.BlockSpec((1,) + li.shape[1:], lambda d, s: (d, 0, 0)),
        ],
        out_specs=pl.BlockSpec((1, bsz, t_len, width), lambda d, s: (d, 0, chunk_idx(d, s), 0)),
        scratch_shapes=[
            pltpu.VMEM((two_half // LANES * (ngh // 2), nrows * S5_PITCH, LANES), F32),
            pltpu.VMEM((nrows, (ngh // 2) * (two_half // 2)), F32),
            pltpu.VMEM((nrows, (ngh // 2) * (two_half // 2)), F32),
        ],
        compiler_params=_cparams(("arbitrary", "arbitrary")),
        name="s5_scan",
    )(p0, bblk, cblk, lr, li)


def _s5_operators(a_re, a_im, log_dt, b_re, b_im, c_re, c_im, bsz):
    f32 = F32
    ndir, g, p = a_re.shape
    cin = b_re.shape[-1]
    gl = 16
    ngh = g // gl
    ar, ai = a_re.astype(f32), a_im.astype(f32)
    dt = jnp.exp(log_dt.astype(f32))[..., None]
    mag = jnp.exp(ar * dt)
    lr = mag * jnp.cos(ai * dt)
    li = mag * jnp.sin(ai * dt)
    den = ar * ar + ai * ai
    zr = ((lr - 1.0) * ar + li * ai) / den
    zi = (li * ar - (lr - 1.0) * ai) / den
    br, bi = b_re.astype(f32), b_im.astype(f32)
    bbr = zr[..., None] * br - zi[..., None] * bi
    bbi = zr[..., None] * bi + zi[..., None] * br
    eye = jnp.eye(gl, dtype=f32)

    def in_block(m):
        m = m.reshape(ndir, ngh, gl, p, cin)
        return jnp.einsum('dhgpc,gk->dhgckp', m, eye).reshape(ndir, ngh, gl * cin, gl * p)

    bblk = jnp.concatenate([in_block(bbr), in_block(bbi)], axis=-1).astype(BF16)

    def out_block(m):
        m = m.reshape(ndir, ngh, gl, cin, p)
        return jnp.einsum('dhgcp,gk->dhgpkc', m, eye).reshape(ndir, ngh, gl * p, gl * cin)

    cblk = jnp.concatenate([out_block(c_re.astype(f32)), out_block(-c_im.astype(f32))], axis=-2).astype(BF16)

    def rows(v):
        per_seq = ngh // 2
        v = v.reshape(ndir, 2, 1, per_seq * gl * p)
        return jnp.broadcast_to(v, (ndir, 2, bsz, per_seq * gl * p)).reshape(ndir, 2 * bsz, per_seq * gl * p)

    return bblk, cblk, rows(lr), rows(li)


def _s5_glu_kernel(yf_ref, yr_ref, u_ref, sg_ref, d_ref, w_ref, b_ref, o_ref):
    y = u_ref[0].astype(F32) * d_ref[...] + yf_ref[0, 0] + yr_ref[0, 0]
    g = 0.5 * y * (1.0 + jnp.tanh(math.sqrt(2.0 / math.pi) * (y + 0.044715 * (y * y * y))))
    z = _dot(g.astype(BF16), w_ref[...]) + b_ref[...]
    sg = sg_ref[0].astype(F32)
    o_ref[0] = (g * jax.nn.sigmoid(z) * _silu(sg)).astype(o_ref.dtype)


def _s5_glu(ydir, p0, d, glu_w, glu_b, *, u_col_blk, sg_col_blk):
    _, bsz, r, width = ydir.shape
    tm = ROW_TILE
    return pl.pallas_call(
        _s5_glu_kernel,
        out_shape=jax.ShapeDtypeStruct((bsz, r, width), BF16),
        grid=(bsz, r // tm),
        in_specs=[
            pl.BlockSpec((1, 1, tm, width), lambda b, i: (0, b, i, 0)),
            pl.BlockSpec((1, 1, tm, width), lambda b, i: (1, b, i, 0)),
            pl.BlockSpec((1, tm, width), lambda b, i: (b, i, u_col_blk)),
            pl.BlockSpec((1, tm, width), lambda b, i: (b, i, sg_col_blk)),
            pl.BlockSpec((1, width), lambda b, i: (0, 0)),
            pl.BlockSpec((width, width), lambda b, i: (0, 0)),
            pl.BlockSpec((1, width), lambda b, i: (0, 0)),
        ],
        out_specs=pl.BlockSpec((1, tm, width), lambda b, i: (b, i, 0)),
        compiler_params=_cparams(("arbitrary", "arbitrary")),
        name="s5_glu",
    )(ydir, ydir, p0, p0, d.reshape(1, width).astype(F32), glu_w.astype(BF16),
      glu_b.reshape(1, width).astype(F32))


def _outproj0_kernel(hyl_ref, hyc_ref, hg_ref, s5_ref, x_ref, ctx_ref, modv_ref, w_ref, o_ref,
                     *, nl_tiles, bsz, d, hw):
    b = pl.program_id(0)
    i = pl.program_id(1)
    is_lat = i < nl_tiles
    hy = jnp.where(is_lat, hyl_ref[0], hyc_ref[0]).astype(F32)
    a = (hy * _silu(hg_ref[0].astype(F32))).astype(BF16)
    acc = _dot(a, w_ref[0:hw, :]) + _dot(s5_ref[0], w_ref[hw:, :])
    row = jnp.where(is_lat, b, bsz)
    gate = modv_ref[pl.ds(row, 1), :][:, 2 * d:3 * d]
    xin = jnp.where(is_lat, x_ref[0], ctx_ref[0])
    o_ref[0] = xin + gate * acc


def _outproj0(hy_l, hy_c, p0, s5g, x, ctx, modv, w, *, hg_col_blk):
    bsz, n, d = x.shape
    nc = ctx.shape[1]
    tm = ROW_TILE
    nl_t, nc_t = n // tm, nc // tm
    hw = hy_l.shape[2]
    kern = functools.partial(_outproj0_kernel, nl_tiles=nl_t, bsz=bsz, d=d, hw=hw)
    lat = lambda b, i: (b, jnp.minimum(i, nl_t - 1), 0)
    cx = lambda b, i: (b, jnp.maximum(i - nl_t, 0), 0)
    return pl.pallas_call(
        kern,
        out_shape=jax.ShapeDtypeStruct((bsz, n + nc, d), F32),
        grid=(bsz, nl_t + nc_t),
        in_specs=[
            pl.BlockSpec((1, tm, hw), lat),
            pl.BlockSpec((1, tm, hw), cx),
            pl.BlockSpec((1, tm, hw), lambda b, i: (b, i, hg_col_blk)),
            pl.BlockSpec((1, tm, hw), lambda b, i: (b, i, 0)),
            pl.BlockSpec((1, tm, d), lat),
            pl.BlockSpec((1, tm, d), cx),
            pl.BlockSpec(modv.shape, lambda b, i: (0, 0)),
            pl.BlockSpec(w.shape, lambda b, i: (0, 0)),
        ],
        out_specs=pl.BlockSpec((1, tm, d), lambda b, i: (b, i, 0)),
        compiler_params=_cparams(("arbitrary", "arbitrary")),
        name="outproj_even",
    )(hy_l, hy_c, p0, s5g, x, ctx, modv, w)


def _attn_kernel(q_ref, k_ref, v_ref, g_ref, lvec_ref, subln_ref, o_ref, *, lam_init):
    q = q_ref[0]
    k = k_ref[0]
    v = v_ref[0]
    lane = lax.broadcasted_iota(jnp.int32, q.shape, 1)
    zero = jnp.zeros_like(q)
    dn = (((1,), (1,)), ((), ()))

    def probs(qm):
        s = lax.dot_general(qm, k, dn, preferred_element_type=F32)
        m = jnp.max(s, axis=-1, keepdims=True)
        p = jnp.exp(s - m)
        return p, jnp.sum(p, axis=-1, keepdims=True)

    p1, l1 = probs(jnp.where(lane < DA_HEAD, q, zero))
    p2, l2 = probs(jnp.where(lane >= DA_HEAD, q, zero))
    lv = lvec_ref[...]
    lam = (jnp.exp(jnp.sum(lv[0:1] * lv[1:2], axis=-1, keepdims=True))
           - jnp.exp(jnp.sum(lv[2:3] * lv[3:4], axis=-1, keepdims=True)) + lam_init)
    w = p1 * (1.0 / l1) - p2 * (lam / l2)
    o = _dot(w.astype(BF16), v)
    ms = jnp.mean(o * o, axis=-1, keepdims=True)
    o = o * lax.rsqrt(ms + EPS) * subln_ref[...] * (1.0 - lam_init)
    g = g_ref[0].astype(F32)
    o_ref[0] = (o * _silu(g)).astype(o_ref.dtype)


def _attention(p1, lvec, subln, *, n_lat, heads, lam_init):
    bsz, r, _ = p1.shape
    hd = 2 * DA_HEAD
    tq = 256
    kern = functools.partial(_attn_kernel, lam_init=lam_init)
    return pl.pallas_call(
        kern,
        out_shape=jax.ShapeDtypeStruct((bsz, n_lat, heads * hd), BF16),
        grid=(bsz, heads, n_lat // tq),
        in_specs=[
            pl.BlockSpec((1, tq, hd), lambda b, h, i: (b, i, h)),
            pl.BlockSpec((1, r, hd), lambda b, h, i: (b, 0, heads + h)),
            pl.BlockSpec((1, r, hd), lambda b, h, i: (b, 0, 2 * heads + h)),
            pl.BlockSpec((1, tq, hd), lambda b, h, i: (b, i, 3 * heads + h)),
            pl.BlockSpec(lvec.shape, lambda b, h, i: (0, 0)),
            pl.BlockSpec((1, hd), lambda b, h, i: (0, 0)),
        ],
        out_specs=pl.BlockSpec((1, tq, hd), lambda b, h, i: (b, i, h)),
        compiler_params=_cparams(("arbitrary", "arbitrary", "arbitrary")),
        name="diff_attention",
    )(p1, p1, p1, p1, lvec, subln)


def _outproj1_kernel(o_ref, x_ref, modv_ref, w_ref, out_ref, *, d):
    b = pl.program_id(0)
    acc = _dot(o_ref[0], w_ref[...])
    gate = modv_ref[pl.ds(b, 1), :][:, 2 * d:3 * d]
    out_ref[0] = x_ref[0] + gate * acc


def _outproj1(o, xc, modv, w):
    bsz, n, dv = o.shape
    d = xc.shape[2]
    tm = ROW_TILE
    return pl.pallas_call(
        functools.partial(_outproj1_kernel, d=d),
        out_shape=jax.ShapeDtypeStruct((bsz, n, d), F32),
        grid=(bsz, n // tm),
        in_specs=[
            pl.BlockSpec((1, tm, dv), lambda b, i: (b, i, 0)),
            pl.BlockSpec((1, tm, d), lambda b, i: (b, i, 0)),
            pl.BlockSpec(modv.shape, lambda b, i: (0, 0)),
            pl.BlockSpec(w.shape, lambda b, i: (0, 0)),
        ],
        out_specs=pl.BlockSpec((1, tm, d), lambda b, i: (b, i, 0)),
        compiler_params=_cparams(("arbitrary", "arbitrary")),
        name="outproj_odd",
    )(o, xc, modv, w)


def _rope_tables(n_lat, n_ctx):
    quarter = DA_HEAD // 4
    rows = n_lat // GRID_W
    row = jnp.broadcast_to(jnp.arange(rows, dtype=F32)[:, None], (rows, GRID_W)).reshape(n_lat)
    col = jnp.broadcast_to(jnp.arange(GRID_W, dtype=F32)[None, :], (rows, GRID_W)).reshape(n_lat)
    freqs = ROPE_BASE ** (-jnp.arange(quarter, dtype=F32) / quarter)
    ar = row[:, None] * freqs[None, :]
    ac = col[:, None] * freqs[None, :]
    cos = jnp.concatenate([jnp.cos(ar), jnp.cos(ar), jnp.cos(ac), jnp.cos(ac)], axis=-1)
    sin = jnp.concatenate([-jnp.sin(ar), jnp.sin(ar), -jnp.sin(ac), jnp.sin(ac)], axis=-1)
    cos = jnp.concatenate([cos, cos], axis=-1)
    sin = jnp.concatenate([sin, sin], axis=-1)
    cos = jnp.concatenate([cos, jnp.ones((n_ctx, 2 * DA_HEAD), F32)], axis=0)
    sin = jnp.concatenate([sin, jnp.zeros((n_ctx, 2 * DA_HEAD), F32)], axis=0)
    return cos, sin


def kernel(x, c, ctx, c_ctx, mod_w, mod_b, norm_w, ev_in_w, ev_out_w, hy_conv_w, hy_conv_b, hy_w1, hy_b1, hy_w2, hy_b2, hy_w3, hy_b3, hy_freq, hy_skip, s5_a_re, s5_a_im, s5_log_dt, s5_b_re, s5_b_im, s5_c_re, s5_c_im, s5_d, s5_glu_w, s5_glu_b, od_in_w, od_out_w, da_q_norm, da_k_norm, da_lq1, da_lk1, da_lq2, da_lk2, da_subln):
    bsz, n, d = x.shape
    nc = ctx.shape[1]
    assert n % ROW_TILE == 0 and nc % ROW_TILE == 0 and 2 * bsz == SUBLANES
    depth = mod_w.shape[0]
    assert depth == 2

    npad = SUBLANES * ((bsz + 1 + SUBLANES - 1) // SUBLANES)
    cvec = jnp.concatenate([c, c_ctx[None, :], jnp.zeros((npad - bsz - 1, d), F32)], axis=0)
    modv = _mod_vectors(cvec, mod_w, mod_b)

    hw = hy_skip.shape[-1]
    sw = s5_d.shape[-1]
    p0 = _inproj0(x, ctx, modv[0], norm_w[0:1], ev_in_w[0].astype(BF16))
    sc = _shortconv(p0, hy_conv_w[0].astype(F32), hy_conv_b[0].astype(F32), n)
    filt = (hy_w1[0], hy_b1[0], hy_w2[0], hy_b2[0], hy_w3[0], hy_b3[0], hy_freq[0])
    hy_l = _hyena_seq(sc, n, 0, filt, hy_skip[0], hw)
    hy_c = _hyena_seq(sc, nc, n // nc, filt, hy_skip[0], hw)

    bblk, cblk, lr, li = _s5_operators(s5_a_re[0], s5_a_im[0], s5_log_dt[0], s5_b_re[0], s5_b_im[0],
                                       s5_c_re[0], s5_c_im[0], bsz)
    su_blk = ((HY_ORDER + 2) * hw) // sw
    ydir = _s5_scan(p0, bblk, cblk, lr, li, n_lat=n, n_ctx=nc, col_blk=su_blk)
    s5g = _s5_glu(ydir, p0, s5_d[0], s5_glu_w[0], s5_glu_b[0], u_col_blk=su_blk, sg_col_blk=su_blk + 1)
    x1 = _outproj0(hy_l, hy_c, p0, s5g, x, ctx, modv[0], ev_out_w[0].astype(BF16),
                   hg_col_blk=(HY_ORDER + 1))

    heads = d // (2 * DA_HEAD)
    reps = d // DA_HEAD
    qscale = DA_HEAD ** -0.5
    qkw = jnp.stack([jnp.tile(da_q_norm[0].astype(F32), reps) * qscale, jnp.tile(da_k_norm[0].astype(F32), reps)])
    qkw = jnp.concatenate([qkw, jnp.zeros((SUBLANES - 2, d), F32)], axis=0)
    gidx = jnp.arange(2 * LANES) // DA_HEAD
    gm = (gidx[:, None] == gidx[None, :]).astype(BF16) * (1.0 / DA_HEAD)
    cos_t, sin_t = _rope_tables(n, nc)
    p1 = _inproj1(x1, modv[1], norm_w[1:2], od_in_w[0].astype(BF16), qkw, gm.astype(BF16), cos_t, sin_t, n)
    lam_init = 0.8 - 0.6 * math.exp(-0.3 * 1)
    lvec = jnp.stack([da_lq1[0], da_lk1[0], da_lq2[0], da_lk2[0]]).astype(F32)
    lvec = jnp.pad(lvec, ((0, SUBLANES - 4), (0, LANES - lvec.shape[1])))
    o = _attention(p1, lvec, da_subln[0].reshape(1, 2 * DA_HEAD).astype(F32), n_lat=n, heads=heads,
                   lam_init=lam_init)
    return _outproj1(o, x1, modv[1], od_out_w[0].astype(BF16))
```

```python
import functools
import math

import jax
import jax.numpy as jnp
from jax import lax
from jax.experimental import pallas as pl
from jax.experimental.pallas import tpu as pltpu

F32 = jnp.float32
BF16 = jnp.bfloat16
HIGHEST = lax.Precision.HIGHEST

EPS = 1e-6
ROW_TILE = 256
LANES = 128
SUBLANES = 8
VMEM_LIMIT = 56 * 1024 * 1024

HY_ORDER = 2
HY_BANDS = 16
HY_TARGET = 1e-2
HY_FAST_PCT = 0.3
HY_SLOW_PCT = 1.5
S5_GROUP = 16
S5_STATE = 64
DA_HEAD = 64
GRID_W = 64
ROPE_BASE = 10000.0

S5_CHUNK = 128
S5_PITCH = S5_CHUNK + 4


def _cparams(sem):
    return pltpu.CompilerParams(dimension_semantics=sem, vmem_limit_bytes=VMEM_LIMIT)


def _silu(x):
    return x * jax.nn.sigmoid(x)


def _dot(a, b):
    return jnp.dot(a, b, preferred_element_type=F32)


def _mod_kernel(c_ref, w_ref, b_ref, o_ref):
    a = _silu(c_ref[...])
    o_ref[0] = jnp.dot(a, w_ref[0], precision=HIGHEST, preferred_element_type=F32) + b_ref[0]


def _mod_vectors(cvec, mod_w, mod_b):
    depth, d, d3 = mod_w.shape
    tn = 1024
    return pl.pallas_call(
        _mod_kernel,
        out_shape=jax.ShapeDtypeStruct((depth, cvec.shape[0], d3), F32),
        grid=(depth, d3 // tn),
        in_specs=[
            pl.BlockSpec(cvec.shape, lambda l, j: (0, 0)),
            pl.BlockSpec((1, d, tn), lambda l, j: (l, 0, j)),
            pl.BlockSpec((1, 1, tn), lambda l, j: (l, 0, j)),
        ],
        out_specs=pl.BlockSpec((1, cvec.shape[0], tn), lambda l, j: (l, 0, j)),
        compiler_params=_cparams(("arbitrary", "arbitrary")),
        name="mod_vectors",
    )(cvec, mod_w, mod_b.reshape(depth, 1, d3))


def _normed(x, modv_ref, nw_ref, row, d):
    m = modv_ref[pl.ds(row, 1), :]
    shift = m[:, 0:d]
    scale = m[:, d:2 * d]
    ms = jnp.mean(x * x, axis=-1, keepdims=True)
    y = x * lax.rsqrt(ms + EPS) * nw_ref[...]
    return (y * (1.0 + scale) + shift).astype(BF16)


def _inproj0_kernel(x_ref, ctx_ref, modv_ref, nw_ref, w_ref, o_ref, *, nl_tiles, bsz, d):
    b = pl.program_id(1)
    i = pl.program_id(2)
    is_lat = i < nl_tiles
    x = jnp.where(is_lat, x_ref[0], ctx_ref[0])
    row = jnp.where(is_lat, b, bsz)
    h = _normed(x, modv_ref, nw_ref, row, d)
    o_ref[0] = _dot(h, w_ref[...]).astype(o_ref.dtype)


def _inproj0(x, ctx, modv, nw, w):
    bsz, n, d = x.shape
    nc = ctx.shape[1]
    tm = ROW_TILE
    nl_t, nc_t = n // tm, nc // tm
    nout = w.shape[1]
    tn = 2048
    kern = functools.partial(_inproj0_kernel, nl_tiles=nl_t, bsz=bsz, d=d)
    return pl.pallas_call(
        kern,
        out_shape=jax.ShapeDtypeStruct((bsz, n + nc, nout), BF16),
        grid=(nout // tn, bsz, nl_t + nc_t),
        in_specs=[
            pl.BlockSpec((1, tm, d), lambda j, b, i: (b, jnp.minimum(i, nl_t - 1), 0)),
            pl.BlockSpec((1, tm, d), lambda j, b, i: (b, jnp.maximum(i - nl_t, 0), 0)),
            pl.BlockSpec(modv.shape, lambda j, b, i: (0, 0)),
            pl.BlockSpec((1, d), lambda j, b, i: (0, 0)),
            pl.BlockSpec((d, tn), lambda j, b, i: (0, j)),
        ],
        out_specs=pl.BlockSpec((1, tm, tn), lambda j, b, i: (b, i, j)),
        compiler_params=_cparams(("arbitrary", "arbitrary", "arbitrary")),
        name="inproj_even",
    )(x, ctx, modv, nw, w)


def _swap_quarters(x):
    nlanes = x.shape[-1]
    lane = lax.broadcasted_iota(jnp.int32, x.shape, x.ndim - 1)
    fwd = pltpu.roll(x, nlanes - 16, axis=x.ndim - 1)
    bwd = pltpu.roll(x, 16, axis=x.ndim - 1)
    return jnp.where((lane & 16) == 0, fwd, bwd)


def _inproj1_kernel(x_ref, modv_ref, nw_ref, w_ref, qkw_ref, gm_ref, cos_ref, sin_ref, o_ref,
                    *, nl_tiles, bsz, d):
    j = pl.program_id(0)
    b = pl.program_id(1)
    i = pl.program_id(2)
    is_lat = i < nl_tiles
    row = jnp.where(is_lat, b, bsz)
    needed = jnp.logical_or(is_lat, jnp.logical_or(j == 1, j == 2))

    @pl.when(jnp.logical_not(needed))
    def _():
        o_ref[...] = jnp.zeros(o_ref.shape, o_ref.dtype)

    @pl.when(needed)
    def _():
        h = _normed(x_ref[0], modv_ref, nw_ref, row, d)
        acc = _dot(h, w_ref[...])

        @pl.when(j >= 2)
        def _():
            o_ref[0] = acc.astype(o_ref.dtype)

        @pl.when(j < 2)
        def _():
            sq = (acc * acc).astype(BF16)
            gm = gm_ref[...]
            width = gm.shape[0]
            ms = jnp.concatenate(
                [_dot(sq[:, t * width:(t + 1) * width], gm) for t in range(acc.shape[1] // width)], axis=1)
            wsel = jnp.where(j == 0, qkw_ref[0:1, :], qkw_ref[1:2, :])
            yn = acc * lax.rsqrt(ms + EPS) * wsel
            reps = acc.shape[1] // cos_ref.shape[1]
            cos = pltpu.repeat(cos_ref[...], reps, axis=1)
            sin = pltpu.repeat(sin_ref[...], reps, axis=1)
            o_ref[0] = (yn * cos + _swap_quarters(yn) * sin).astype(o_ref.dtype)


def _inproj1(xc, modv, nw, w, qkw, gm, cos_t, sin_t, n_lat):
    bsz, r, d = xc.shape
    tm = ROW_TILE
    nl_t = n_lat // tm
    nout = w.shape[1]
    tn = 2048
    kern = functools.partial(_inproj1_kernel, nl_tiles=nl_t, bsz=bsz, d=d)
    return pl.pallas_call(
        kern,
        out_shape=jax.ShapeDtypeStruct((bsz, r, nout), BF16),
        grid=(nout // tn, bsz, r // tm),
        in_specs=[
            pl.BlockSpec((1, tm, d), lambda j, b, i: (b, i, 0)),
            pl.BlockSpec(modv.shape, lambda j, b, i: (0, 0)),
            pl.BlockSpec((1, d), lambda j, b, i: (0, 0)),
            pl.BlockSpec((d, tn), lambda j, b, i: (0, j)),
            pl.BlockSpec(qkw.shape, lambda j, b, i: (0, 0)),
            pl.BlockSpec(gm.shape, lambda j, b, i: (0, 0)),
            pl.BlockSpec((tm, cos_t.shape[1]), lambda j, b, i: (i, 0)),
            pl.BlockSpec((tm, sin_t.shape[1]), lambda j, b, i: (i, 0)),
        ],
        out_specs=pl.BlockSpec((1, tm, tn), lambda j, b, i: (b, i, j)),
        compiler_params=_cparams(("arbitrary", "arbitrary", "arbitrary")),
        name="inproj_odd",
    )(xc, modv, nw, w, qkw, gm, cos_t, sin_t)


def _shortconv_kernel(p_ref, w_ref, b_ref, o_ref, *, n_lat):
    p = p_ref[0].astype(F32)
    r = p.shape[0]
    row = lax.broadcasted_iota(jnp.int32, p.shape, 0)
    prev = pltpu.roll(p, 1, axis=0)
    nxt = pltpu.roll(p, r - 1, axis=0)
    prev = jnp.where((row == 0) | (row == n_lat), 0.0, prev)
    nxt = jnp.where((row == n_lat - 1) | (row == r - 1), 0.0, nxt)
    w = w_ref[...]
    o_ref[0] = (prev * w[0:1] + p * w[1:2] + nxt * w[2:3] + b_ref[...]).astype(o_ref.dtype)


def _shortconv(p0, conv_w, conv_b, n_lat):
    bsz, r, _ = p0.shape
    width = conv_w.shape[1]
    tc = 512
    return pl.pallas_call(
        functools.partial(_shortconv_kernel, n_lat=n_lat),
        out_shape=jax.ShapeDtypeStruct((bsz, r, width), BF16),
        grid=(bsz, width // tc),
        in_specs=[
            pl.BlockSpec((1, r, tc), lambda b, j: (b, 0, j)),
            pl.BlockSpec((3, tc), lambda b, j: (0, j)),
            pl.BlockSpec((1, tc), lambda b, j: (0, j)),
        ],
        out_specs=pl.BlockSpec((1, r, tc), lambda b, j: (b, 0, j)),
        compiler_params=_cparams(("arbitrary", "arbitrary")),
        name="hyena_shortconv",
    )(p0, conv_w, conv_b.reshape(1, width))


def _filter_kernel(feat_ref, w1_ref, b1_ref, w2_ref, b2_ref, w3_ref, b3_ref, freq_ref, delta_ref,
                   h_ref, l1_ref):
    q = pl.program_id(0)
    ti = pl.program_id(1)
    feat = feat_ref[...]
    tt = feat.shape[0]
    z1 = jnp.dot(feat, w1_ref[...], precision=HIGHEST, preferred_element_type=F32) + b1_ref[...]
    h1 = jnp.sin(freq_ref[0:1, :] * z1)
    z2 = jnp.dot(h1, w2_ref[...], precision=HIGHEST, preferred_element_type=F32) + b2_ref[...]
    h2 = jnp.sin(freq_ref[1:2, :] * z2)
    z3 = jnp.dot(h2, w3_ref[...], precision=HIGHEST, preferred_element_type=F32) + b3_ref[...]
    tn = feat[:, 0:1]
    hf = z3 * jnp.exp(-tn * delta_ref[...])
    row = lax.broadcasted_iota(jnp.int32, hf.shape, 0) + ti * tt
    hf = jnp.where((row == 0) & (q % 2 == 1), 0.0, hf)
    h_ref[...] = hf

    @pl.when(ti == 0)
    def _():
        l1_ref[...] = jnp.zeros(l1_ref.shape, l1_ref.dtype)

    part = jnp.sum(jnp.abs(hf).reshape(tt // SUBLANES, SUBLANES, hf.shape[1]), axis=0)
    l1_ref[...] += part


def _hyena_filter_taps(n, w1, b1, w2, b2, w3, b3, freq, width):
    t = jnp.arange(n, dtype=F32)
    tn = t / n
    bands = jnp.linspace(1e-4, HY_BANDS - 1, HY_BANDS, dtype=F32)
    ang = (2.0 * math.pi / n) * t[:, None] * bands[None, :]
    feat = jnp.concatenate([tn[:, None], jnp.cos(ang), -jnp.sin(ang)], axis=-1)
    emb = feat.shape[1]
    feat = jnp.pad(feat, ((0, 0), (0, LANES - emb)))
    w1p = jnp.pad(w1.astype(F32), ((0, LANES - emb), (0, 0)))
    ffn = w1.shape[1]
    deltas = jnp.abs(jnp.linspace(math.log(HY_TARGET) / HY_SLOW_PCT, math.log(HY_TARGET) / HY_FAST_PCT,
                                  width, dtype=F32)).reshape(1, width)
    ncol = w3.shape[1]
    nq = ncol // width
    tt = min(512, n)
    taps, l1 = pl.pallas_call(
        _filter_kernel,
        out_shape=(jax.ShapeDtypeStruct((n, ncol), F32), jax.ShapeDtypeStruct((SUBLANES, ncol), F32)),
        grid=(nq, n // tt),
        in_specs=[
            pl.BlockSpec((tt, LANES), lambda q, i: (i, 0)),
            pl.BlockSpec((LANES, ffn), lambda q, i: (0, 0)),
            pl.BlockSpec((1, ffn), lambda q, i: (0, 0)),
            pl.BlockSpec((ffn, ffn), lambda q, i: (0, 0)),
            pl.BlockSpec((1, ffn), lambda q, i: (0, 0)),
            pl.BlockSpec((ffn, width), lambda q, i: (0, q)),
            pl.BlockSpec((1, width), lambda q, i: (0, q)),
            pl.BlockSpec((2, ffn), lambda q, i: (0, 0)),
            pl.BlockSpec((1, width), lambda q, i: (0, 0)),
        ],
        out_specs=(pl.BlockSpec((tt, width), lambda q, i: (i, q)),
                   pl.BlockSpec((SUBLANES, width), lambda q, i: (0, q))),
        compiler_params=_cparams(("arbitrary", "arbitrary")),
        name="hyena_filter_taps",
    )(feat, w1p, b1.reshape(1, ffn).astype(F32), w2.astype(F32), b2.reshape(1, ffn).astype(F32),
      w3.astype(F32), b3.reshape(1, ncol).astype(F32), freq.astype(F32), deltas)
    return taps, jnp.sum(l1, axis=0)


def _dft_tables(n):
    blk = 64
    hi = jnp.arange(n // blk, dtype=jnp.int32)[:, None]
    lo = jnp.arange(blk, dtype=jnp.int32)[:, None]
    other = jnp.arange(n, dtype=jnp.int32)[None, :]
    period = 4 * n
    unit = math.pi / (2 * n)

    def cs(m):
        a = (m % period).astype(F32) * unit
        return jnp.cos(a), jnp.sin(a)

    def combine(ca, sa, cb, sb):
        c = ca[:, None, :] * cb[None, :, :] - sa[:, None, :] * sb[None, :, :]
        s = sa[:, None, :] * cb[None, :, :] + ca[:, None, :] * sb[None, :, :]
        return c.reshape(n, n).astype(BF16), (-s).reshape(n, n).astype(BF16)

    fc, fs = combine(*cs(2 * blk * hi * other), *cs((2 * lo + 1) * other))
    tc, ts = combine(*cs((2 * other + 1) * blk * hi), *cs((2 * other + 1) * lo))
    return fc, fs, tc, ts


def _dft_fwd_kernel(fc_ref, fs_ref, z_ref, o_re_ref, o_im_ref):
    z = z_ref[...].astype(BF16)
    o_re_ref[...] = _dot(fc_ref[...], z)
    o_im_ref[...] = _dot(fs_ref[...], z)


def _filter_spectrum(fc, fs, taps):
    n, ncol = taps.shape
    tf = min(512, n)
    tcn = 512
    return pl.pallas_call(
        _dft_fwd_kernel,
        out_shape=(jax.ShapeDtypeStruct((n, ncol), F32), jax.ShapeDtypeStruct((n, ncol), F32)),
        grid=(ncol // tcn, n // tf),
        in_specs=[
            pl.BlockSpec((tf, n), lambda j, f: (f, 0)),
            pl.BlockSpec((tf, n), lambda j, f: (f, 0)),
            pl.BlockSpec((n, tcn), lambda j, f: (0, j)),
        ],
        out_specs=(pl.BlockSpec((tf, tcn), lambda j, f: (f, j)),
                   pl.BlockSpec((tf, tcn), lambda j, f: (f, j))),
        compiler_params=_cparams(("arbitrary", "arbitrary")),
        name="hyena_filter_spectrum",
    )(fc, fs, taps)


def _dft_fwd_mul_kernel(fc_ref, fs_ref, z_ref, kr_ref, ki_ref, pr_ref, pi_ref):
    z = z_ref[0]
    ur = _dot(fc_ref[...], z)
    ui = _dot(fs_ref[...], z)
    kr = kr_ref[...]
    ki = ki_ref[...]
    pr_ref[0] = (ur * kr - ui * ki).astype(pr_ref.dtype)
    pi_ref[0] = (ur * ki + ui * kr).astype(pi_ref.dtype)


def _dft_fwd_mul(fc, fs, z, kr, ki, *, n, row_blk, col_blk0):
    bsz = z.shape[0]
    width = kr.shape[1]
    tf = min(512, n)
    tcn = 512
    return pl.pallas_call(
        _dft_fwd_mul_kernel,
        out_shape=(jax.ShapeDtypeStruct((bsz, n, width), BF16), jax.ShapeDtypeStruct((bsz, n, width), BF16)),
        grid=(width // tcn, n // tf, bsz),
        in_specs=[
            pl.BlockSpec((tf, n), lambda j, f, b: (f, 0)),
            pl.BlockSpec((tf, n), lambda j, f, b: (f, 0)),
            pl.BlockSpec((1, n, tcn), lambda j, f, b: (b, row_blk, col_blk0 + j)),
            pl.BlockSpec((tf, tcn), lambda j, f, b: (f, j)),
            pl.BlockSpec((tf, tcn), lambda j, f, b: (f, j)),
        ],
        out_specs=(pl.BlockSpec((1, tf, tcn), lambda j, f, b: (b, f, j)),
                   pl.BlockSpec((1, tf, tcn), lambda j, f, b: (b, f, j))),
        compiler_params=_cparams(("arbitrary", "arbitrary", "arbitrary")),
        name="hyena_dft_fwd",
    )(fc, fs, z, kr, ki)


def _dft_inv_gate_kernel(tc_ref, ts_ref, pr_ref, pi_ref, z_ref, xg_ref, skip_ref, o_ref):
    y = _dot(tc_ref[...], pr_ref[0]) + _dot(ts_ref[...], pi_ref[0])
    z = z_ref[0].astype(F32)
    xg = xg_ref[0].astype(F32)
    o_ref[0] = (xg * (y + z * skip_ref[...])).astype(o_ref.dtype)


def _dft_inv_gate(tc, ts, pr, pi, z, xg, skip, *, n, z_row_blk, z_col_blk0, xg_row_blk, xg_col_blk0):
    bsz, _, width = pr.shape
    tt = min(512, n)
    tcn = 512
    zrb = z_row_blk * (n // tt)
    xrb = xg_row_blk * (n // tt)
    return pl.pallas_call(
        _dft_inv_gate_kernel,
        out_shape=jax.ShapeDtypeStruct((bsz, n, width), BF16),
        grid=(width // tcn, bsz, n // tt),
        in_specs=[
            pl.BlockSpec((tt, n), lambda j, b, i: (i, 0)),
            pl.BlockSpec((tt, n), lambda j, b, i: (i, 0)),
            pl.BlockSpec((1, n, tcn), lambda j, b, i: (b, 0, j)),
            pl.BlockSpec((1, n, tcn), lambda j, b, i: (b, 0, j)),
            pl.BlockSpec((1, tt, tcn), lambda j, b, i: (b, zrb + i, z_col_blk0 + j)),
            pl.BlockSpec((1, tt, tcn), lambda j, b, i: (b, xrb + i, xg_col_blk0 + j)),
            pl.BlockSpec((1, tcn), lambda j, b, i: (0, j)),
        ],
        out_specs=pl.BlockSpec((1, tt, tcn), lambda j, b, i: (b, i, j)),
        compiler_params=_cparams(("arbitrary", "arbitrary", "arbitrary")),
        name="hyena_dft_inv",
    )(tc, ts, pr, pi, z, xg, skip)


def _hyena_seq(sc, n, seq_row_blk, filt, skip, width):
    w1, b1, w2, b2, w3, b3, freq = filt
    taps, l1 = _hyena_filter_taps(n, w1, b1, w2, b2, w3, b3, freq, width)
    fc, fs, tc, ts = _dft_tables(n)
    sr, si = _filter_spectrum(fc, fs, taps)
    sr = sr.reshape(n, HY_ORDER, 2, width)
    si = si.reshape(n, HY_ORDER, 2, width)
    l1 = l1.reshape(HY_ORDER, 2, width).sum(axis=1)
    norm = (1.0 / n) / l1
    cb = width // 512
    y = None
    for i in range(HY_ORDER):
        kr = (sr[:, i, 0] + sr[:, i, 1]) * norm[i]
        ki = (si[:, i, 0] - si[:, i, 1]) * norm[i]
        if i == 0:
            z, zrb, zcb = sc, seq_row_blk, 0
        else:
            z, zrb, zcb = y, 0, 0
        pr, pi = _dft_fwd_mul(fc, fs, z, kr, ki, n=n, row_blk=zrb, col_blk0=zcb)
        y = _dft_inv_gate(tc, ts, pr, pi, z, sc, skip[i].reshape(1, width).astype(F32), n=n,
                          z_row_blk=zrb, z_col_blk0=zcb, xg_row_blk=seq_row_blk, xg_col_blk0=(i + 1) * cb)
    return y


def _s5_kernel(u_ref, bblk_ref, cblk_ref, lr_ref, li_ref, y_ref, st_ref, cr_ref, ci_ref, *, bsz, chunk):
    d = pl.program_id(0)
    s = pl.program_id(1)
    t_len = chunk
    ngh = bblk_ref.shape[1]
    kb = bblk_ref.shape[2]
    half = bblk_ref.shape[3] // 2
    nslab = half // LANES
    per_seq = ngh // 2

    @pl.when(s == 0)
    def _():
        cr_ref[...] = jnp.zeros(cr_ref.shape, F32)
        ci_ref[...] = jnp.zeros(ci_ref.shape, F32)

    for b in range(bsz):
        for gh in range(ngh):
            hsel, k = divmod(gh, per_seq)
            q = hsel * bsz + b
            bu = _dot(u_ref[b, :, gh * kb:(gh + 1) * kb], bblk_ref[0, gh])
            for lb in range(2 * nslab):
                st_ref[k * 2 * nslab + lb, q * S5_PITCH:q * S5_PITCH + t_len, :] = bu[:, lb * LANES:(lb + 1) * LANES]

    for k in range(per_seq):
        lr = lr_ref[0, :, k * half:(k + 1) * half]
        li = li_ref[0, :, k * half:(k + 1) * half]
        base = k * 2 * nslab

        def step(i, carry, base=base, lr=lr, li=li):
            sr, si = carry
            t = jnp.where(d == 0, i, t_len - 1 - i)
            rows = pl.ds(t, SUBLANES, stride=S5_PITCH)
            xr = jnp.concatenate([st_ref[base + lb, rows, :] for lb in range(nslab)], axis=1)
            xi = jnp.concatenate([st_ref[base + nslab + lb, rows, :] for lb in range(nslab)], axis=1)
            nr = lr * sr - li * si + xr
            ni = lr * si + li * sr + xi
            for lb in range(nslab):
                st_ref[base + lb, rows, :] = nr[:, lb * LANES:(lb + 1) * LANES]
                st_ref[base + nslab + lb, rows, :] = ni[:, lb * LANES:(lb + 1) * LANES]
            return nr, ni

        sr0 = cr_ref[:, k * half:(k + 1) * half]
        si0 = ci_ref[:, k * half:(k + 1) * half]
        sr1, si1 = lax.fori_loop(0, t_len, step, (sr0, si0), unroll=2)
        cr_ref[:, k * half:(k + 1) * half] = sr1
        ci_ref[:, k * half:(k + 1) * half] = si1

    for b in range(bsz):
        for gh in range(ngh):
            hsel, k = divmod(gh, per_seq)
            q = hsel * bsz + b
            st = jnp.concatenate(
                [st_ref[k * 2 * nslab + lb, q * S5_PITCH:q * S5_PITCH + t_len, :] for lb in range(2 * nslab)],
                axis=1).astype(BF16)
            y_ref[0, b, :, gh * kb:(gh + 1) * kb] = _dot(st, cblk_ref[0, gh])


def _s5_scan(p0, bblk, cblk, lr, li, *, n_lat, n_ctx, col_blk):
    bsz, r, _ = p0.shape
    t_len = S5_CHUNK
    nl, nc = n_lat // t_len, n_ctx // t_len
    ngh, kb, two_half = bblk.shape[1], bblk.shape[2], bblk.shape[3]
    width = ngh * kb

    def chunk_idx(d, s):
        fwd = jnp.where(s < nc, nl + s, s - nc)
        rev = jnp.where(s < nc, nl + nc - 1 - s, nl - 1 - (s - nc))
        return jnp.where(d == 0, fwd, rev)

    nrows = SUBLANES
    kern = functools.partial(_s5_kernel, bsz=bsz, chunk=t_len)
    return pl.pallas_call(
        kern,
        out_shape=jax.ShapeDtypeStruct((2, bsz, r, width), F32),
        grid=(2, nl + nc),
        in_specs=[
            pl.BlockSpec((bsz, t_len, width), lambda d, s: (0, chunk_idx(d, s), col_blk)),
            pl.BlockSpec((1,) + bblk.shape[1:], lambda d, s: (d, 0, 0, 0)),
            pl.BlockSpec((1,) + cblk.shape[1:], lambda d, s: (d, 0, 0, 0)),
            pl.BlockSpec((1,) + lr.shape[1:], lambda d, s: (d, 0, 0)),
            pl.BlockSpec((1,) + li.shape[1:], lambda d, s: (d, 0, 0)),
        ],
        out_specs=pl.BlockSpec((1, bsz, t_len, width), lambda d, s: (d, 0, chunk_idx(d, s), 0)),
        scratch_shapes=[
            pltpu.VMEM((two_half // LANES * (ngh // 2), nrows * S5_PITCH, LANES), F32),
            pltpu.VMEM((nrows, (ngh // 2) * (two_half // 2)), F32),
            pltpu.VMEM((nrows, (ngh // 2) * (two_half // 2)), F32),
        ],
        compiler_params=_cparams(("arbitrary", "arbitrary")),
        name="s5_scan",
    )(p0, bblk, cblk, lr, li)


def _s5_operators(a_re, a_im, log_dt, b_re, b_im, c_re, c_im, bsz):
    f32 = F32
    ndir, g, p = a_re.shape
    cin = b_re.shape[-1]
    gl = 16
    ngh = g // gl
    ar, ai = a_re.astype(f32), a_im.astype(f32)
    dt = jnp.exp(log_dt.astype(f32))[..., None]
    mag = jnp.exp(ar * dt)
    lr = mag * jnp.cos(ai * dt)
    li = mag * jnp.sin(ai * dt)
    den = ar * ar + ai * ai
    zr = ((lr - 1.0) * ar + li * ai) / den
    zi = (li * ar - (lr - 1.0) * ai) / den
    br, bi = b_re.astype(f32), b_im.astype(f32)
    bbr = zr[..., None] * br - zi[..., None] * bi
    bbi = zr[..., None] * bi + zi[..., None] * br
    eye = jnp.eye(gl, dtype=f32)

    def in_block(m):
        m = m.reshape(ndir, ngh, gl, p, cin)
        return jnp.einsum('dhgpc,gk->dhgckp', m, eye).reshape(ndir, ngh, gl * cin, gl * p)

    bblk = jnp.concatenate([in_block(bbr), in_block(bbi)], axis=-1).astype(BF16)

    def out_block(m):
        m = m.reshape(ndir, ngh, gl, cin, p)
        return jnp.einsum('dhgcp,gk->dhgpkc', m, eye).reshape(ndir, ngh, gl * p, gl * cin)

    cblk = jnp.concatenate([out_block(c_re.astype(f32)), out_block(-c_im.astype(f32))], axis=-2).astype(BF16)

    def rows(v):
        per_seq = ngh // 2
        v = v.reshape(ndir, 2, 1, per_seq * gl * p)
        return jnp.broadcast_to(v, (ndir, 2, bsz, per_seq * gl * p)).reshape(ndir, 2 * bsz, per_seq * gl * p)

    return bblk, cblk, rows(lr), rows(li)


def _s5_glu_kernel(yf_ref, yr_ref, u_ref, sg_ref, d_ref, w_ref, b_ref, o_ref):
    y = u_ref[0].astype(F32) * d_ref[...] + yf_ref[0, 0] + yr_ref[0, 0]
    g = 0.5 * y * (1.0 + jnp.tanh(math.sqrt(2.0 / math.pi) * (y + 0.044715 * (y * y * y))))
    z = _dot(g.astype(BF16), w_ref[...]) + b_ref[...]
    sg = sg_ref[0].astype(F32)
    o_ref[0] = (g * jax.nn.sigmoid(z) * _silu(sg)).astype(o_ref.dtype)


def _s5_glu(ydir, p0, d, glu_w, glu_b, *, u_col_blk, sg_col_blk):
    _, bsz, r, width = ydir.shape
    tm = ROW_TILE
    return pl.pallas_call(
        _s5_glu_kernel,
        out_shape=jax.ShapeDtypeStruct((bsz, r, width), BF16),
        grid=(bsz, r // tm),
        in_specs=[
            pl.BlockSpec((1, 1, tm, width), lambda b, i: (0, b, i, 0)),
            pl.BlockSpec((1, 1, tm, width), lambda b, i: (1, b, i, 0)),
            pl.BlockSpec((1, tm, width), lambda b, i: (b, i, u_col_blk)),
            pl.BlockSpec((1, tm, width), lambda b, i: (b, i, sg_col_blk)),
            pl.BlockSpec((1, width), lambda b, i: (0, 0)),
            pl.BlockSpec((width, width), lambda b, i: (0, 0)),
            pl.BlockSpec((1, width), lambda b, i: (0, 0)),
        ],
        out_specs=pl.BlockSpec((1, tm, width), lambda b, i: (b, i, 0)),
        compiler_params=_cparams(("arbitrary", "arbitrary")),
        name="s5_glu",
    )(ydir, ydir, p0, p0, d.reshape(1, width).astype(F32), glu_w.astype(BF16),
      glu_b.reshape(1, width).astype(F32))


def _outproj0_kernel(hyl_ref, hyc_ref, hg_ref, s5_ref, x_ref, ctx_ref, modv_ref, w_ref, o_ref,
                     *, nl_tiles, bsz, d, hw):
    b = pl.program_id(0)
    i = pl.program_id(1)
    is_lat = i < nl_tiles
    hy = jnp.where(is_lat, hyl_ref[0], hyc_ref[0]).astype(F32)
    a = (hy * _silu(hg_ref[0].astype(F32))).astype(BF16)
    acc = _dot(a, w_ref[0:hw, :]) + _dot(s5_ref[0], w_ref[hw:, :])
    row = jnp.where(is_lat, b, bsz)
    gate = modv_ref[pl.ds(row, 1), :][:, 2 * d:3 * d]
    xin = jnp.where(is_lat, x_ref[0], ctx_ref[0])
    o_ref[0] = xin + gate * acc


def _outproj0(hy_l, hy_c, p0, s5g, x, ctx, modv, w, *, hg_col_blk):
    bsz, n, d = x.shape
    nc = ctx.shape[1]
    tm = ROW_TILE
    nl_t, nc_t = n // tm, nc // tm
    hw = hy_l.shape[2]
    kern = functools.partial(_outproj0_kernel, nl_tiles=nl_t, bsz=bsz, d=d, hw=hw)
    lat = lambda b, i: (b, jnp.minimum(i, nl_t - 1), 0)
    cx = lambda b, i: (b, jnp.maximum(i - nl_t, 0), 0)
    return pl.pallas_call(
        kern,
        out_shape=jax.ShapeDtypeStruct((bsz, n + nc, d), F32),
        grid=(bsz, nl_t + nc_t),
        in_specs=[
            pl.BlockSpec((1, tm, hw), lat),
            pl.BlockSpec((1, tm, hw), cx),
            pl.BlockSpec((1, tm, hw), lambda b, i: (b, i, hg_col_blk)),
            pl.BlockSpec((1, tm, hw), lambda b, i: (b, i, 0)),
            pl.BlockSpec((1, tm, d), lat),
            pl.BlockSpec((1, tm, d), cx),
            pl.BlockSpec(modv.shape, lambda b, i: (0, 0)),
            pl.BlockSpec(w.shape, lambda b, i: (0, 0)),
        ],
        out_specs=pl.BlockSpec((1, tm, d), lambda b, i: (b, i, 0)),
        compiler_params=_cparams(("arbitrary", "arbitrary")),
        name="outproj_even",
    )(hy_l, hy_c, p0, s5g, x, ctx, modv, w)


ATTN_SAFE_LOG2 = 57.0
ATTN_KEY_CHUNK = 256


def _attn_prepare(k_ref, v_ref, vt_ref, kn_ref):
    hd = v_ref.shape[2]
    vt_ref[0:hd, :] = v_ref[0].astype(F32).T.astype(BF16)
    vt_ref[hd:, :] = jnp.ones((vt_ref.shape[0] - hd, vt_ref.shape[1]), BF16)
    k = k_ref[0].astype(F32)
    ksq = k * k
    lane = lax.broadcasted_iota(jnp.int32, ksq.shape, 1)
    n1 = jnp.max(jnp.sum(jnp.where(lane < DA_HEAD, ksq, 0.0), axis=1, keepdims=True), axis=0, keepdims=True)
    n2 = jnp.max(jnp.sum(jnp.where(lane >= DA_HEAD, ksq, 0.0), axis=1, keepdims=True), axis=0, keepdims=True)
    kn_ref[0:1, :] = jnp.broadcast_to(n1, (1, kn_ref.shape[1]))
    kn_ref[1:2, :] = jnp.broadcast_to(n2, (1, kn_ref.shape[1]))


def _attn_kernel(q_ref, k_ref, v_ref, g_ref, lvec_ref, subln_ref, o_ref, vt_ref, kn_ref, st_ref, *, lam_init):
    @pl.when(pl.program_id(2) == 0)
    def _():
        _attn_prepare(k_ref, v_ref, vt_ref, kn_ref)

    q = q_ref[0]
    hd = q.shape[1]
    lane = lax.broadcasted_iota(jnp.int32, q.shape, 1)
    zero = jnp.zeros_like(q)
    qs = (jnp.where(lane < DA_HEAD, q, zero), jnp.where(lane >= DA_HEAD, q, zero))
    lv = lvec_ref[...]
    lam = (jnp.exp(jnp.sum(lv[0:1] * lv[1:2], axis=-1, keepdims=True))
           - jnp.exp(jnp.sum(lv[2:3] * lv[3:4], axis=-1, keepdims=True)) + lam_init)

    bound = None
    for m in range(2):
        qf = qs[m].astype(F32)
        qn = jnp.max(jnp.sum(qf * qf, axis=1, keepdims=True), axis=0, keepdims=True)
        bm = jnp.sqrt(qn * kn_ref[m:m + 1, 0:1])
        bound = bm if bound is None else jnp.maximum(bound, bm)
    safe = bound[0, 0] <= ATTN_SAFE_LOG2

    nkc = k_ref.shape[1] // ATTN_KEY_CHUNK

    def scores_t(c, m):
        kc = k_ref[0, c * ATTN_KEY_CHUNK:(c + 1) * ATTN_KEY_CHUNK, :]
        return lax.dot_general(kc, qs[m], (((1,), (1,)), ((), ())), preferred_element_type=F32)

    def attend(subtract_max):
        shifts = [None, None]
        if subtract_max:
            for m in range(2):
                for c in range(nkc):
                    cm = jnp.max(scores_t(c, m), axis=0, keepdims=True)
                    shifts[m] = cm if shifts[m] is None else jnp.maximum(shifts[m], cm)
        accs = [None, None]

        def stage(c):
            for m in range(2):
                st_ref[c % 2, m] = scores_t(c, m)

        stage(0)
        for c in range(nkc):
            if c + 1 < nkc:
                stage(c + 1)
            for m in range(2):
                st = st_ref[c % 2, m]
                if subtract_max:
                    st = st - shifts[m]
                p = jnp.exp2(st).astype(BF16)
                part = _dot(vt_ref[:, c * ATTN_KEY_CHUNK:(c + 1) * ATTN_KEY_CHUNK], p)
                accs[m] = part if accs[m] is None else accs[m] + part
        outs = [acc[0:hd] * (1.0 / acc[hd:hd + 1]) for acc in accs]
        o = (outs[0] - lam * outs[1]).T
        ms = jnp.mean(o * o, axis=-1, keepdims=True)
        o = o * lax.rsqrt(ms + EPS) * subln_ref[...] * (1.0 - lam_init)
        g = g_ref[0].astype(F32)
        o_ref[0] = (o * _silu(g)).astype(o_ref.dtype)

    @pl.when(safe)
    def _():
        attend(False)

    @pl.when(jnp.logical_not(safe))
    def _():
        attend(True)


def _attention(p1, lvec, subln, *, n_lat, heads, lam_init):
    bsz, r, _ = p1.shape
    hd = 2 * DA_HEAD
    tq = 512
    kern = functools.partial(_attn_kernel, lam_init=lam_init)
    return pl.pallas_call(
        kern,
        out_shape=jax.ShapeDtypeStruct((bsz, n_lat, heads * hd), BF16),
        grid=(bsz, heads, n_lat // tq),
        in_specs=[
            pl.BlockSpec((1, tq, hd), lambda b, h, i: (b, i, h)),
            pl.BlockSpec((1, r, hd), lambda b, h, i: (b, 0, heads + h)),
            pl.BlockSpec((1, r, hd), lambda b, h, i: (b, 0, 2 * heads + h)),
            pl.BlockSpec((1, tq, hd), lambda b, h, i: (b, i, 3 * heads + h)),
            pl.BlockSpec(lvec.shape, lambda b, h, i: (0, 0)),
            pl.BlockSpec((1, hd), lambda b, h, i: (0, 0)),
        ],
        out_specs=pl.BlockSpec((1, tq, hd), lambda b, h, i: (b, i, h)),
        scratch_shapes=[
            pltpu.VMEM((hd + 2 * SUBLANES, r), BF16),
            pltpu.VMEM((SUBLANES, LANES), F32),
            pltpu.VMEM((2, 2, ATTN_KEY_CHUNK, tq), F32),
        ],
        compiler_params=_cparams(("arbitrary", "arbitrary", "arbitrary")),
        name="diff_attention",
    )(p1, p1, p1, p1, lvec, subln)


def _outproj1_kernel(o_ref, x_ref, modv_ref, w_ref, out_ref, *, d):
    b = pl.program_id(0)
    acc = _dot(o_ref[0], w_ref[...])
    gate = modv_ref[pl.ds(b, 1), :][:, 2 * d:3 * d]
    out_ref[0] = x_ref[0] + gate * acc


def _outproj1(o, xc, modv, w):
    bsz, n, dv = o.shape
    d = xc.shape[2]
    tm = ROW_TILE
    return pl.pallas_call(
        functools.partial(_outproj1_kernel, d=d),
        out_shape=jax.ShapeDtypeStruct((bsz, n, d), F32),
        grid=(bsz, n // tm),
        in_specs=[
            pl.BlockSpec((1, tm, dv), lambda b, i: (b, i, 0)),
            pl.BlockSpec((1, tm, d), lambda b, i: (b, i, 0)),
            pl.BlockSpec(modv.shape, lambda b, i: (0, 0)),
            pl.BlockSpec(w.shape, lambda b, i: (0, 0)),
        ],
        out_specs=pl.BlockSpec((1, tm, d), lambda b, i: (b, i, 0)),
        compiler_params=_cparams(("arbitrary", "arbitrary")),
        name="outproj_odd",
    )(o, xc, modv, w)


def _rope_tables(n_lat, n_ctx):
    quarter = DA_HEAD // 4
    rows = n_lat // GRID_W
    row = jnp.broadcast_to(jnp.arange(rows, dtype=F32)[:, None], (rows, GRID_W)).reshape(n_lat)
    col = jnp.broadcast_to(jnp.arange(GRID_W, dtype=F32)[None, :], (rows, GRID_W)).reshape(n_lat)
    freqs = ROPE_BASE ** (-jnp.arange(quarter, dtype=F32) / quarter)
    ar = row[:, None] * freqs[None, :]
    ac = col[:, None] * freqs[None, :]
    cos = jnp.concatenate([jnp.cos(ar), jnp.cos(ar), jnp.cos(ac), jnp.cos(ac)], axis=-1)
    sin = jnp.concatenate([-jnp.sin(ar), jnp.sin(ar), -jnp.sin(ac), jnp.sin(ac)], axis=-1)
    cos = jnp.concatenate([cos, cos], axis=-1)
    sin = jnp.concatenate([sin, sin], axis=-1)
    cos = jnp.concatenate([cos, jnp.ones((n_ctx, 2 * DA_HEAD), F32)], axis=0)
    sin = jnp.concatenate([sin, jnp.zeros((n_ctx, 2 * DA_HEAD), F32)], axis=0)
    return cos, sin


def kernel(x, c, ctx, c_ctx, mod_w, mod_b, norm_w, ev_in_w, ev_out_w, hy_conv_w, hy_conv_b, hy_w1, hy_b1, hy_w2, hy_b2, hy_w3, hy_b3, hy_freq, hy_skip, s5_a_re, s5_a_im, s5_log_dt, s5_b_re, s5_b_im, s5_c_re, s5_c_im, s5_d, s5_glu_w, s5_glu_b, od_in_w, od_out_w, da_q_norm, da_k_norm, da_lq1, da_lk1, da_lq2, da_lk2, da_subln):
    bsz, n, d = x.shape
    nc = ctx.shape[1]
    assert n % ROW_TILE == 0 and nc % ROW_TILE == 0 and 2 * bsz == SUBLANES
    depth = mod_w.shape[0]
    assert depth == 2

    npad = SUBLANES * ((bsz + 1 + SUBLANES - 1) // SUBLANES)
    cvec = jnp.concatenate([c, c_ctx[None, :], jnp.zeros((npad - bsz - 1, d), F32)], axis=0)
    modv = _mod_vectors(cvec, mod_w, mod_b)

    hw = hy_skip.shape[-1]
    sw = s5_d.shape[-1]
    p0 = _inproj0(x, ctx, modv[0], norm_w[0:1], ev_in_w[0].astype(BF16))
    sc = _shortconv(p0, hy_conv_w[0].astype(F32), hy_conv_b[0].astype(F32), n)
    filt = (hy_w1[0], hy_b1[0], hy_w2[0], hy_b2[0], hy_w3[0], hy_b3[0], hy_freq[0])
    hy_l = _hyena_seq(sc, n, 0, filt, hy_skip[0], hw)
    hy_c = _hyena_seq(sc, nc, n // nc, filt, hy_skip[0], hw)

    bblk, cblk, lr, li = _s5_operators(s5_a_re[0], s5_a_im[0], s5_log_dt[0], s5_b_re[0], s5_b_im[0],
                                       s5_c_re[0], s5_c_im[0], bsz)
    su_blk = ((HY_ORDER + 2) * hw) // sw
    ydir = _s5_scan(p0, bblk, cblk, lr, li, n_lat=n, n_ctx=nc, col_blk=su_blk)
    s5g = _s5_glu(ydir, p0, s5_d[0], s5_glu_w[0], s5_glu_b[0], u_col_blk=su_blk, sg_col_blk=su_blk + 1)
    x1 = _outproj0(hy_l, hy_c, p0, s5g, x, ctx, modv[0], ev_out_w[0].astype(BF16),
                   hg_col_blk=(HY_ORDER + 1))

    heads = d // (2 * DA_HEAD)
    reps = d // DA_HEAD
    qscale = DA_HEAD ** -0.5 * math.log2(math.e)
    qkw = jnp.stack([jnp.tile(da_q_norm[0].astype(F32), reps) * qscale, jnp.tile(da_k_norm[0].astype(F32), reps)])
    qkw = jnp.concatenate([qkw, jnp.zeros((SUBLANES - 2, d), F32)], axis=0)
    gidx = jnp.arange(2 * LANES) // DA_HEAD
    gm = (gidx[:, None] == gidx[None, :]).astype(BF16) * (1.0 / DA_HEAD)
    cos_t, sin_t = _rope_tables(n, nc)
    p1 = _inproj1(x1, modv[1], norm_w[1:2], od_in_w[0].astype(BF16), qkw, gm.astype(BF16), cos_t, sin_t, n)
    lam_init = 0.8 - 0.6 * math.exp(-0.3 * 1)
    lvec = jnp.stack([da_lq1[0], da_lk1[0], da_lq2[0], da_lk2[0]]).astype(F32)
    lvec = jnp.pad(lvec, ((0, SUBLANES - 4), (0, LANES - lvec.shape[1])))
    o = _attention(p1, lvec, da_subln[0].reshape(1, 2 * DA_HEAD).astype(F32), n_lat=n, heads=heads,
                   lam_init=lam_init)
    return _outproj1(o, x1, modv[1], od_out_w[0].astype(BF16))
```

```python
import functools
import math

import jax
import jax.numpy as jnp
from jax import lax
from jax.experimental import pallas as pl
from jax.experimental.pallas import tpu as pltpu

F32 = jnp.float32
BF16 = jnp.bfloat16
HIGHEST = lax.Precision.HIGHEST

EPS = 1e-6
ROW_TILE = 256
LANES = 128
SUBLANES = 8
VMEM_LIMIT = 56 * 1024 * 1024

HY_ORDER = 2
HY_BANDS = 16
HY_TARGET = 1e-2
HY_FAST_PCT = 0.3
HY_SLOW_PCT = 1.5
S5_GROUP = 16
S5_STATE = 64
DA_HEAD = 64
GRID_W = 64
ROPE_BASE = 10000.0

S5_CHUNK = 128
S5_PITCH = S5_CHUNK + 4


def _cparams(sem):
    return pltpu.CompilerParams(dimension_semantics=sem, vmem_limit_bytes=VMEM_LIMIT)


def _silu(x):
    return x * jax.nn.sigmoid(x)


def _dot(a, b):
    return jnp.dot(a, b, preferred_element_type=F32)


def _mod_kernel(c_ref, w_ref, b_ref, o_ref):
    a = _silu(c_ref[...])
    o_ref[0] = jnp.dot(a, w_ref[0], precision=HIGHEST, preferred_element_type=F32) + b_ref[0]


def _mod_vectors(cvec, mod_w, mod_b):
    depth, d, d3 = mod_w.shape
    tn = 1024
    return pl.pallas_call(
        _mod_kernel,
        out_shape=jax.ShapeDtypeStruct((depth, cvec.shape[0], d3), F32),
        grid=(depth, d3 // tn),
        in_specs=[
            pl.BlockSpec(cvec.shape, lambda l, j: (0, 0)),
            pl.BlockSpec((1, d, tn), lambda l, j: (l, 0, j)),
            pl.BlockSpec((1, 1, tn), lambda l, j: (l, 0, j)),
        ],
        out_specs=pl.BlockSpec((1, cvec.shape[0], tn), lambda l, j: (l, 0, j)),
        compiler_params=_cparams(("arbitrary", "arbitrary")),
        name="mod_vectors",
    )(cvec, mod_w, mod_b.reshape(depth, 1, d3))


def _normed(x, modv_ref, nw_ref, row, d):
    m = modv_ref[pl.ds(row, 1), :]
    shift = m[:, 0:d]
    scale = m[:, d:2 * d]
    ms = jnp.mean(x * x, axis=-1, keepdims=True)
    y = x * lax.rsqrt(ms + EPS) * nw_ref[...]
    return (y * (1.0 + scale) + shift).astype(BF16)


def _inproj0_kernel(x_ref, ctx_ref, modv_ref, nw_ref, w_ref, o_ref, *, nl_tiles, bsz, d):
    b = pl.program_id(1)
    i = pl.program_id(2)
    is_lat = i < nl_tiles
    x = jnp.where(is_lat, x_ref[0], ctx_ref[0])
    row = jnp.where(is_lat, b, bsz)
    h = _normed(x, modv_ref, nw_ref, row, d)
    o_ref[0] = _dot(h, w_ref[...]).astype(o_ref.dtype)


def _inproj0(x, ctx, modv, nw, w):
    bsz, n, d = x.shape
    nc = ctx.shape[1]
    tm = ROW_TILE
    nl_t, nc_t = n // tm, nc // tm
    nout = w.shape[1]
    tn = 2048
    kern = functools.partial(_inproj0_kernel, nl_tiles=nl_t, bsz=bsz, d=d)
    return pl.pallas_call(
        kern,
        out_shape=jax.ShapeDtypeStruct((bsz, n + nc, nout), BF16),
        grid=(nout // tn, bsz, nl_t + nc_t),
        in_specs=[
            pl.BlockSpec((1, tm, d), lambda j, b, i: (b, jnp.minimum(i, nl_t - 1), 0)),
            pl.BlockSpec((1, tm, d), lambda j, b, i: (b, jnp.maximum(i - nl_t, 0), 0)),
            pl.BlockSpec(modv.shape, lambda j, b, i: (0, 0)),
            pl.BlockSpec((1, d), lambda j, b, i: (0, 0)),
            pl.BlockSpec((d, tn), lambda j, b, i: (0, j)),
        ],
        out_specs=pl.BlockSpec((1, tm, tn), lambda j, b, i: (b, i, j)),
        compiler_params=_cparams(("arbitrary", "arbitrary", "arbitrary")),
        name="inproj_even",
    )(x, ctx, modv, nw, w)


def _swap_quarters(x):
    nlanes = x.shape[-1]
    lane = lax.broadcasted_iota(jnp.int32, x.shape, x.ndim - 1)
    fwd = pltpu.roll(x, nlanes - 16, axis=x.ndim - 1)
    bwd = pltpu.roll(x, 16, axis=x.ndim - 1)
    return jnp.where((lane & 16) == 0, fwd, bwd)


def _inproj1_kernel(x_ref, modv_ref, nw_ref, w_ref, qkw_ref, gm_ref, cos_ref, sin_ref, o_ref,
                    *, nl_tiles, bsz, d):
    j = pl.program_id(0)
    b = pl.program_id(1)
    i = pl.program_id(2)
    is_lat = i < nl_tiles
    row = jnp.where(is_lat, b, bsz)
    needed = jnp.logical_or(is_lat, jnp.logical_or(j == 1, j == 2))

    @pl.when(jnp.logical_not(needed))
    def _():
        o_ref[...] = jnp.zeros(o_ref.shape, o_ref.dtype)

    @pl.when(needed)
    def _():
        h = _normed(x_ref[0], modv_ref, nw_ref, row, d)
        acc = _dot(h, w_ref[...])

        @pl.when(j >= 2)
        def _():
            o_ref[0] = acc.astype(o_ref.dtype)

        @pl.when(j < 2)
        def _():
            sq = (acc * acc).astype(BF16)
            gm = gm_ref[...]
            width = gm.shape[0]
            ms = jnp.concatenate(
                [_dot(sq[:, t * width:(t + 1) * width], gm) for t in range(acc.shape[1] // width)], axis=1)
            wsel = jnp.where(j == 0, qkw_ref[0:1, :], qkw_ref[1:2, :])
            yn = acc * lax.rsqrt(ms + EPS) * wsel
            reps = acc.shape[1] // cos_ref.shape[1]
            cos = pltpu.repeat(cos_ref[...], reps, axis=1)
            sin = pltpu.repeat(sin_ref[...], reps, axis=1)
            o_ref[0] = (yn * cos + _swap_quarters(yn) * sin).astype(o_ref.dtype)


def _inproj1(xc, modv, nw, w, qkw, gm, cos_t, sin_t, n_lat):
    bsz, r, d = xc.shape
    tm = ROW_TILE
    nl_t = n_lat // tm
    nout = w.shape[1]
    tn = 2048
    kern = functools.partial(_inproj1_kernel, nl_tiles=nl_t, bsz=bsz, d=d)
    return pl.pallas_call(
        kern,
        out_shape=jax.ShapeDtypeStruct((bsz, r, nout), BF16),
        grid=(nout // tn, bsz, r // tm),
        in_specs=[
            pl.BlockSpec((1, tm, d), lambda j, b, i: (b, i, 0)),
            pl.BlockSpec(modv.shape, lambda j, b, i: (0, 0)),
            pl.BlockSpec((1, d), lambda j, b, i: (0, 0)),
            pl.BlockSpec((d, tn), lambda j, b, i: (0, j)),
            pl.BlockSpec(qkw.shape, lambda j, b, i: (0, 0)),
            pl.BlockSpec(gm.shape, lambda j, b, i: (0, 0)),
            pl.BlockSpec((tm, cos_t.shape[1]), lambda j, b, i: (i, 0)),
            pl.BlockSpec((tm, sin_t.shape[1]), lambda j, b, i: (i, 0)),
        ],
        out_specs=pl.BlockSpec((1, tm, tn), lambda j, b, i: (b, i, j)),
        compiler_params=_cparams(("arbitrary", "arbitrary", "arbitrary")),
        name="inproj_odd",
    )(xc, modv, nw, w, qkw, gm, cos_t, sin_t)


def _shortconv_kernel(p_ref, w_ref, b_ref, o_ref, *, n_lat):
    p = p_ref[0].astype(F32)
    r = p.shape[0]
    row = lax.broadcasted_iota(jnp.int32, p.shape, 0)
    prev = pltpu.roll(p, 1, axis=0)
    nxt = pltpu.roll(p, r - 1, axis=0)
    prev = jnp.where((row == 0) | (row == n_lat), 0.0, prev)
    nxt = jnp.where((row == n_lat - 1) | (row == r - 1), 0.0, nxt)
    w = w_ref[...]
    o_ref[0] = (prev * w[0:1] + p * w[1:2] + nxt * w[2:3] + b_ref[...]).astype(o_ref.dtype)


def _shortconv(p0, conv_w, conv_b, n_lat):
    bsz, r, _ = p0.shape
    width = conv_w.shape[1]
    tc = 512
    return pl.pallas_call(
        functools.partial(_shortconv_kernel, n_lat=n_lat),
        out_shape=jax.ShapeDtypeStruct((bsz, r, width), BF16),
        grid=(bsz, width // tc),
        in_specs=[
            pl.BlockSpec((1, r, tc), lambda b, j: (b, 0, j)),
            pl.BlockSpec((3, tc), lambda b, j: (0, j)),
            pl.BlockSpec((1, tc), lambda b, j: (0, j)),
        ],
        out_specs=pl.BlockSpec((1, r, tc), lambda b, j: (b, 0, j)),
        compiler_params=_cparams(("arbitrary", "arbitrary")),
        name="hyena_shortconv",
    )(p0, conv_w, conv_b.reshape(1, width))


def _filter_kernel(feat_ref, w1_ref, b1_ref, w2_ref, b2_ref, w3_ref, b3_ref, freq_ref, delta_ref,
                   h_ref, l1_ref):
    q = pl.program_id(0)
    ti = pl.program_id(1)
    feat = feat_ref[...]
    tt = feat.shape[0]
    z1 = jnp.dot(feat, w1_ref[...], precision=HIGHEST, preferred_element_type=F32) + b1_ref[...]
    h1 = jnp.sin(freq_ref[0:1, :] * z1)
    z2 = jnp.dot(h1, w2_ref[...], precision=HIGHEST, preferred_element_type=F32) + b2_ref[...]
    h2 = jnp.sin(freq_ref[1:2, :] * z2)
    z3 = jnp.dot(h2, w3_ref[...], precision=HIGHEST, preferred_element_type=F32) + b3_ref[...]
    tn = feat[:, 0:1]
    hf = z3 * jnp.exp(-tn * delta_ref[...])
    row = lax.broadcasted_iota(jnp.int32, hf.shape, 0) + ti * tt
    hf = jnp.where((row == 0) & (q % 2 == 1), 0.0, hf)
    h_ref[...] = hf

    @pl.when(ti == 0)
    def _():
        l1_ref[...] = jnp.zeros(l1_ref.shape, l1_ref.dtype)

    part = jnp.sum(jnp.abs(hf).reshape(tt // SUBLANES, SUBLANES, hf.shape[1]), axis=0)
    l1_ref[...] += part


def _hyena_filter_taps(n, w1, b1, w2, b2, w3, b3, freq, width):
    t = jnp.arange(n, dtype=F32)
    tn = t / n
    bands = jnp.linspace(1e-4, HY_BANDS - 1, HY_BANDS, dtype=F32)
    ang = (2.0 * math.pi / n) * t[:, None] * bands[None, :]
    feat = jnp.concatenate([tn[:, None], jnp.cos(ang), -jnp.sin(ang)], axis=-1)
    emb = feat.shape[1]
    feat = jnp.pad(feat, ((0, 0), (0, LANES - emb)))
    w1p = jnp.pad(w1.astype(F32), ((0, LANES - emb), (0, 0)))
    ffn = w1.shape[1]
    deltas = jnp.abs(jnp.linspace(math.log(HY_TARGET) / HY_SLOW_PCT, math.log(HY_TARGET) / HY_FAST_PCT,
                                  width, dtype=F32)).reshape(1, width)
    ncol = w3.shape[1]
    nq = ncol // width
    tt = min(512, n)
    taps, l1 = pl.pallas_call(
        _filter_kernel,
        out_shape=(jax.ShapeDtypeStruct((n, ncol), F32), jax.ShapeDtypeStruct((SUBLANES, ncol), F32)),
        grid=(nq, n // tt),
        in_specs=[
            pl.BlockSpec((tt, LANES), lambda q, i: (i, 0)),
            pl.BlockSpec((LANES, ffn), lambda q, i: (0, 0)),
            pl.BlockSpec((1, ffn), lambda q, i: (0, 0)),
            pl.BlockSpec((ffn, ffn), lambda q, i: (0, 0)),
            pl.BlockSpec((1, ffn), lambda q, i: (0, 0)),
            pl.BlockSpec((ffn, width), lambda q, i: (0, q)),
            pl.BlockSpec((1, width), lambda q, i: (0, q)),
            pl.BlockSpec((2, ffn), lambda q, i: (0, 0)),
            pl.BlockSpec((1, width), lambda q, i: (0, 0)),
        ],
        out_specs=(pl.BlockSpec((tt, width), lambda q, i: (i, q)),
                   pl.BlockSpec((SUBLANES, width), lambda q, i: (0, q))),
        compiler_params=_cparams(("arbitrary", "arbitrary")),
        name="hyena_filter_taps",
    )(feat, w1p, b1.reshape(1, ffn).astype(F32), w2.astype(F32), b2.reshape(1, ffn).astype(F32),
      w3.astype(F32), b3.reshape(1, ncol).astype(F32), freq.astype(F32), deltas)
    return taps, jnp.sum(l1, axis=0)


def _dft_tables(n):
    blk = 64
    hi = jnp.arange(n // blk, dtype=jnp.int32)[:, None]
    lo = jnp.arange(blk, dtype=jnp.int32)[:, None]
    other = jnp.arange(n, dtype=jnp.int32)[None, :]
    period = 4 * n
    unit = math.pi / (2 * n)

    def cs(m):
        a = (m % period).astype(F32) * unit
        return jnp.cos(a), jnp.sin(a)

    def combine(ca, sa, cb, sb):
        c = ca[:, None, :] * cb[None, :, :] - sa[:, None, :] * sb[None, :, :]
        s = sa[:, None, :] * cb[None, :, :] + ca[:, None, :] * sb[None, :, :]
        return c.reshape(n, n).astype(BF16), (-s).reshape(n, n).astype(BF16)

    fc, fs = combine(*cs(2 * blk * hi * other), *cs((2 * lo + 1) * other))
    tc, ts = combine(*cs((2 * other + 1) * blk * hi), *cs((2 * other + 1) * lo))
    return fc, fs, tc, ts


def _dft_fwd_kernel(fc_ref, fs_ref, z_ref, o_re_ref, o_im_ref):
    z = z_ref[...].astype(BF16)
    o_re_ref[...] = _dot(fc_ref[...], z)
    o_im_ref[...] = _dot(fs_ref[...], z)


def _filter_spectrum(fc, fs, taps):
    n, ncol = taps.shape
    tf = min(512, n)
    tcn = 512
    return pl.pallas_call(
        _dft_fwd_kernel,
        out_shape=(jax.ShapeDtypeStruct((n, ncol), F32), jax.ShapeDtypeStruct((n, ncol), F32)),
        grid=(ncol // tcn, n // tf),
        in_specs=[
            pl.BlockSpec((tf, n), lambda j, f: (f, 0)),
            pl.BlockSpec((tf, n), lambda j, f: (f, 0)),
            pl.BlockSpec((n, tcn), lambda j, f: (0, j)),
        ],
        out_specs=(pl.BlockSpec((tf, tcn), lambda j, f: (f, j)),
                   pl.BlockSpec((tf, tcn), lambda j, f: (f, j))),
        compiler_params=_cparams(("arbitrary", "arbitrary")),
        name="hyena_filter_spectrum",
    )(fc, fs, taps)


def _dft_fwd_mul_kernel(fc_ref, fs_ref, z_ref, kr_ref, ki_ref, pr_ref, pi_ref):
    z = z_ref[0]
    ur = _dot(fc_ref[...], z)
    ui = _dot(fs_ref[...], z)
    kr = kr_ref[...]
    ki = ki_ref[...]
    pr_ref[0] = (ur * kr - ui * ki).astype(pr_ref.dtype)
    pi_ref[0] = (ur * ki + ui * kr).astype(pi_ref.dtype)


def _dft_fwd_mul(fc, fs, z, kr, ki, *, n, row_blk, col_blk0):
    bsz = z.shape[0]
    width = kr.shape[1]
    tf = min(512, n)
    tcn = 512
    return pl.pallas_call(
        _dft_fwd_mul_kernel,
        out_shape=(jax.ShapeDtypeStruct((bsz, n, width), BF16), jax.ShapeDtypeStruct((bsz, n, width), BF16)),
        grid=(width // tcn, n // tf, bsz),
        in_specs=[
            pl.BlockSpec((tf, n), lambda j, f, b: (f, 0)),
            pl.BlockSpec((tf, n), lambda j, f, b: (f, 0)),
            pl.BlockSpec((1, n, tcn), lambda j, f, b: (b, row_blk, col_blk0 + j)),
            pl.BlockSpec((tf, tcn), lambda j, f, b: (f, j)),
            pl.BlockSpec((tf, tcn), lambda j, f, b: (f, j)),
        ],
        out_specs=(pl.BlockSpec((1, tf, tcn), lambda j, f, b: (b, f, j)),
                   pl.BlockSpec((1, tf, tcn), lambda j, f, b: (b, f, j))),
        compiler_params=_cparams(("arbitrary", "arbitrary", "arbitrary")),
        name="hyena_dft_fwd",
    )(fc, fs, z, kr, ki)


def _dft_inv_gate_kernel(tc_ref, ts_ref, pr_ref, pi_ref, z_ref, xg_ref, skip_ref, o_ref):
    y = _dot(tc_ref[...], pr_ref[0]) + _dot(ts_ref[...], pi_ref[0])
    z = z_ref[0].astype(F32)
    xg = xg_ref[0].astype(F32)
    o_ref[0] = (xg * (y + z * skip_ref[...])).astype(o_ref.dtype)


def _dft_inv_gate(tc, ts, pr, pi, z, xg, skip, *, n, z_row_blk, z_col_blk0, xg_row_blk, xg_col_blk0):
    bsz, _, width = pr.shape
    tt = min(512, n)
    tcn = 512
    zrb = z_row_blk * (n // tt)
    xrb = xg_row_blk * (n // tt)
    return pl.pallas_call(
        _dft_inv_gate_kernel,
        out_shape=jax.ShapeDtypeStruct((bsz, n, width), BF16),
        grid=(width // tcn, bsz, n // tt),
        in_specs=[
            pl.BlockSpec((tt, n), lambda j, b, i: (i, 0)),
            pl.BlockSpec((tt, n), lambda j, b, i: (i, 0)),
            pl.BlockSpec((1, n, tcn), lambda j, b, i: (b, 0, j)),
            pl.BlockSpec((1, n, tcn), lambda j, b, i: (b, 0, j)),
            pl.BlockSpec((1, tt, tcn), lambda j, b, i: (b, zrb + i, z_col_blk0 + j)),
            pl.BlockSpec((1, tt, tcn), lambda j, b, i: (b, xrb + i, xg_col_blk0 + j)),
            pl.BlockSpec((1, tcn), lambda j, b, i: (0, j)),
        ],
        out_specs=pl.BlockSpec((1, tt, tcn), lambda j, b, i: (b, i, j)),
        compiler_params=_cparams(("arbitrary", "arbitrary", "arbitrary")),
        name="hyena_dft_inv",
    )(tc, ts, pr, pi, z, xg, skip)


FFT_N2 = 256
FFT_GROUP = 16


def _fft_tables(n):
    n1 = (2 * n) // FFT_N2
    h1 = n1 // 2
    nf2 = n // n1
    f1 = jnp.arange(n1, dtype=jnp.int32)
    t1 = jnp.arange(h1, dtype=jnp.int32)
    th = ((2 * f1[:, None] + 1) * t1[None, :] % (2 * n1)).astype(F32) * (math.pi / n1)
    base = jnp.stack([jnp.cos(th), -jnp.sin(th)])
    eye = jnp.eye(FFT_GROUP, dtype=F32)
    m1 = jnp.einsum('pft,jk->pfjtk', base, eye).reshape(2 * n1 * FFT_GROUP, h1 * FFT_GROUP)
    f = f1[:, None, None] + n1 * jnp.arange(nf2, dtype=jnp.int32)[None, :, None]
    t2 = jnp.arange(FFT_N2, dtype=jnp.int32)[None, None, :]
    psi = ((2 * f + 1) * t2 % (4 * n)).astype(F32) * (math.pi / (2 * n))
    cs = jnp.concatenate([jnp.cos(psi), jnp.sin(psi)], axis=1)
    return (m1.astype(BF16), m1.T.astype(BF16), cs.astype(BF16), jnp.swapaxes(cs, 1, 2).astype(BF16))


def _fft_stage1(z_ref, m1_ref, a_ref, n1):
    h1 = n1 // 2

    def body(g, carry):
        r0 = pl.multiple_of(g * FFT_GROUP, FFT_GROUP)
        xg = jnp.concatenate(
            [z_ref[0, pl.ds(pl.multiple_of(FFT_N2 * t1 + r0, FFT_GROUP), FFT_GROUP), :] for t1 in range(h1)], axis=0)
        out = _dot(m1_ref[...], xg.astype(BF16)).astype(BF16)
        for part in range(2):
            for f1 in range(n1):
                row = (part * n1 + f1) * FFT_GROUP
                a_ref[part, f1, pl.ds(r0, FFT_GROUP), :] = out[row:row + FFT_GROUP]
        return carry

    lax.fori_loop(0, FFT_N2 // FFT_GROUP, body, 0, unroll=4)


def _fft_stage2(a_ref, cs_ref, f1):
    rhs = jnp.concatenate([a_ref[0, f1], a_ref[1, f1]], axis=1)
    r = _dot(cs_ref[f1], rhs)
    nf2 = r.shape[0] // 2
    w = r.shape[1] // 2
    xr = r[0:nf2, 0:w] + r[nf2:, w:]
    xi = r[0:nf2, w:] - r[nf2:, 0:w]
    return xr, xi


def _fft_spectrum_kernel(z_ref, m1_ref, cs_ref, o_ref, a_ref, *, n1):
    _fft_stage1(z_ref, m1_ref, a_ref, n1)

    def body(f1, carry):
        xr, xi = _fft_stage2(a_ref, cs_ref, f1)
        nf2 = xr.shape[0]
        rows = pl.ds(pl.multiple_of(f1 * nf2, nf2), nf2)
        o_ref[0, rows, :] = xr
        o_ref[1, rows, :] = xi
        return carry

    lax.fori_loop(0, n1, body, 0, unroll=min(4, n1))


def _const_spec(arr):
    nd = arr.ndim
    return pl.BlockSpec(arr.shape, lambda *_: (0,) * nd)


def _fft_spectrum(taps, m1, cs):
    _, n, ncol = taps.shape
    n1 = cs.shape[0]
    tcn = 256
    return pl.pallas_call(
        functools.partial(_fft_spectrum_kernel, n1=n1),
        out_shape=jax.ShapeDtypeStruct((2, n, ncol), F32),
        grid=(ncol // tcn,),
        in_specs=[pl.BlockSpec((1, n, tcn), lambda j: (0, 0, j)), _const_spec(m1), _const_spec(cs)],
        out_specs=pl.BlockSpec((2, n, tcn), lambda j: (0, 0, j)),
        scratch_shapes=[pltpu.VMEM((2, n1, FFT_N2, tcn), BF16)],
        compiler_params=_cparams(("arbitrary",)),
        name="hyena_fft_spectrum",
    )(taps, m1, cs)


def _fft_conv_kernel(z_ref, xg_ref, k_ref, skip_ref, m1_ref, m1t_ref, cs_ref, cst_ref, o_ref, a_ref, *, n1):
    h1 = n1 // 2
    _fft_stage1(z_ref, m1_ref, a_ref, n1)

    def mid(f1, carry):
        xr, xi = _fft_stage2(a_ref, cs_ref, f1)
        nf2 = xr.shape[0]
        rows = pl.ds(pl.multiple_of(f1 * nf2, nf2), nf2)
        kr = k_ref[0, rows, :].astype(F32)
        ki = k_ref[1, rows, :].astype(F32)
        pr = xr * kr - xi * ki
        pi = xr * ki + xi * kr
        rhs = jnp.concatenate([jnp.concatenate([pr, pi], axis=1), jnp.concatenate([-pi, pr], axis=1)],
                              axis=0).astype(BF16)
        d = _dot(cst_ref[f1], rhs).astype(BF16)
        w = d.shape[1] // 2
        a_ref[0, f1] = d[:, 0:w]
        a_ref[1, f1] = d[:, w:]
        return carry

    lax.fori_loop(0, n1, mid, 0, unroll=min(8, n1))

    skip = skip_ref[...]

    def last(g, carry):
        r0 = pl.multiple_of(g * FFT_GROUP, FFT_GROUP)
        dg = jnp.concatenate(
            [a_ref[part, f1, pl.ds(r0, FFT_GROUP), :] for part in range(2) for f1 in range(n1)], axis=0)
        yg = _dot(m1t_ref[...], dg)
        for t1 in range(h1):
            rows = pl.ds(pl.multiple_of(FFT_N2 * t1 + r0, FFT_GROUP), FFT_GROUP)
            z = z_ref[0, rows, :].astype(F32)
            xg = xg_ref[0, rows, :].astype(F32)
            o_ref[0, rows, :] = (xg * (yg[t1 * FFT_GROUP:(t1 + 1) * FFT_GROUP] + z * skip)).astype(o_ref.dtype)
        return carry

    lax.fori_loop(0, FFT_N2 // FFT_GROUP, last, 0, unroll=4)


def _fft_conv_gate(z, xg, kspec, skip, tables, *, n, z_row_blk, z_col_blk0, xg_row_blk, xg_col_blk0):
    m1, m1t, cs, cst = tables
    bsz = z.shape[0]
    width = kspec.shape[2]
    n1 = cs.shape[0]
    tcn = 256
    return pl.pallas_call(
        functools.partial(_fft_conv_kernel, n1=n1),
        out_shape=jax.ShapeDtypeStruct((bsz, n, width), BF16),
        grid=(width // tcn, bsz),
        in_specs=[
            pl.BlockSpec((1, n, tcn), lambda j, b: (b, z_row_blk, z_col_blk0 + j)),
            pl.BlockSpec((1, n, tcn), lambda j, b: (b, xg_row_blk, xg_col_blk0 + j)),
            pl.BlockSpec((2, n, tcn), lambda j, b: (0, 0, j)),
            pl.BlockSpec((1, tcn), lambda j, b: (0, j)),
            _const_spec(m1), _const_spec(m1t), _const_spec(cs), _const_spec(cst),
        ],
        out_specs=pl.BlockSpec((1, n, tcn), lambda j, b: (b, 0, j)),
        scratch_shapes=[pltpu.VMEM((2, n1, FFT_N2, tcn), BF16)],
        compiler_params=_cparams(("arbitrary", "arbitrary")),
        name="hyena_fft_conv",
    )(z, xg, kspec, skip, m1, m1t, cs, cst)


def _hyena_seq_fft(sc, n, seq_row_blk, filt, skip, width):
    w1, b1, w2, b2, w3, b3, freq = filt
    taps, l1 = _hyena_filter_taps(n, w1, b1, w2, b2, w3, b3, freq, width)
    tables = _fft_tables(n)
    spec = _fft_spectrum(taps.astype(BF16)[None], tables[0], tables[2])
    spec = spec.reshape(2, n, HY_ORDER, 2, width)
    l1 = l1.reshape(HY_ORDER, 2, width).sum(axis=1)
    norm = (1.0 / n) / l1
    cb = width // 256
    y = None
    for i in range(HY_ORDER):
        kr = (spec[0, :, i, 0] + spec[0, :, i, 1]) * norm[i]
        ki = (spec[1, :, i, 0] - spec[1, :, i, 1]) * norm[i]
        kspec = jnp.stack([kr, ki]).astype(BF16)
        if i == 0:
            z, zrb, zcb = sc, seq_row_blk, 0
        else:
            z, zrb, zcb = y, 0, 0
        y = _fft_conv_gate(z, sc, kspec, skip[i].reshape(1, width).astype(F32), tables, n=n,
                           z_row_blk=zrb, z_col_blk0=zcb, xg_row_blk=seq_row_blk, xg_col_blk0=(i + 1) * cb)
    return y


def _hyena_seq(sc, n, seq_row_blk, filt, skip, width):
    if n % (2 * FFT_N2) == 0:
        return _hyena_seq_fft(sc, n, seq_row_blk, filt, skip, width)
    w1, b1, w2, b2, w3, b3, freq = filt
    taps, l1 = _hyena_filter_taps(n, w1, b1, w2, b2, w3, b3, freq, width)
    fc, fs, tc, ts = _dft_tables(n)
    sr, si = _filter_spectrum(fc, fs, taps)
    sr = sr.reshape(n, HY_ORDER, 2, width)
    si = si.reshape(n, HY_ORDER, 2, width)
    l1 = l1.reshape(HY_ORDER, 2, width).sum(axis=1)
    norm = (1.0 / n) / l1
    cb = width // 512
    y = None
    for i in range(HY_ORDER):
        kr = (sr[:, i, 0] + sr[:, i, 1]) * norm[i]
        ki = (si[:, i, 0] - si[:, i, 1]) * norm[i]
        if i == 0:
            z, zrb, zcb = sc, seq_row_blk, 0
        else:
            z, zrb, zcb = y, 0, 0
        pr, pi = _dft_fwd_mul(fc, fs, z, kr, ki, n=n, row_blk=zrb, col_blk0=zcb)
        y = _dft_inv_gate(tc, ts, pr, pi, z, sc, skip[i].reshape(1, width).astype(F32), n=n,
                          z_row_blk=zrb, z_col_blk0=zcb, xg_row_blk=seq_row_blk, xg_col_blk0=(i + 1) * cb)
    return y


def _s5_kernel(u_ref, bblk_ref, cblk_ref, lr_ref, li_ref, y_ref, st_ref, cr_ref, ci_ref, *, bsz, chunk):
    d = pl.program_id(0)
    s = pl.program_id(1)
    t_len = chunk
    ngh = bblk_ref.shape[1]
    kb = bblk_ref.shape[2]
    half = bblk_ref.shape[3] // 2
    nslab = half // LANES
    per_seq = ngh // 2

    @pl.when(s == 0)
    def _():
        cr_ref[...] = jnp.zeros(cr_ref.shape, F32)
        ci_ref[...] = jnp.zeros(ci_ref.shape, F32)

    for b in range(bsz):
        for gh in range(ngh):
            hsel, k = divmod(gh, per_seq)
            q = hsel * bsz + b
            bu = _dot(u_ref[b, :, gh * kb:(gh + 1) * kb], bblk_ref[0, gh])
            for lb in range(2 * nslab):
                st_ref[k * 2 * nslab + lb, q * S5_PITCH:q * S5_PITCH + t_len, :] = bu[:, lb * LANES:(lb + 1) * LANES]

    for k in range(per_seq):
        lr = lr_ref[0, :, k * half:(k + 1) * half]
        li = li_ref[0, :, k * half:(k + 1) * half]
        base = k * 2 * nslab

        def step(i, carry, base=base, lr=lr, li=li):
            sr, si = carry
            t = jnp.where(d == 0, i, t_len - 1 - i)
            rows = pl.ds(t, SUBLANES, stride=S5_PITCH)
            xr = jnp.concatenate([st_ref[base + lb, rows, :] for lb in range(nslab)], axis=1)
            xi = jnp.concatenate([st_ref[base + nslab + lb, rows, :] for lb in range(nslab)], axis=1)
            nr = lr * sr - li * si + xr
            ni = lr * si + li * sr + xi
            for lb in range(nslab):
                st_ref[base + lb, rows, :] = nr[:, lb * LANES:(lb + 1) * LANES]
                st_ref[base + nslab + lb, rows, :] = ni[:, lb * LANES:(lb + 1) * LANES]
            return nr, ni

        sr0 = cr_ref[:, k * half:(k + 1) * half]
        si0 = ci_ref[:, k * half:(k + 1) * half]
        sr1, si1 = lax.fori_loop(0, t_len, step, (sr0, si0), unroll=2)
        cr_ref[:, k * half:(k + 1) * half] = sr1
        ci_ref[:, k * half:(k + 1) * half] = si1

    for b in range(bsz):
        for gh in range(ngh):
            hsel, k = divmod(gh, per_seq)
            q = hsel * bsz + b
            st = jnp.concatenate(
                [st_ref[k * 2 * nslab + lb, q * S5_PITCH:q * S5_PITCH + t_len, :] for lb in range(2 * nslab)],
                axis=1).astype(BF16)
            y_ref[0, b, :, gh * kb:(gh + 1) * kb] = _dot(st, cblk_ref[0, gh])


def _s5_scan(p0, bblk, cblk, lr, li, *, n_lat, n_ctx, col_blk):
    bsz, r, _ = p0.shape
    t_len = S5_CHUNK
    nl, nc = n_lat // t_len, n_ctx // t_len
    ngh, kb, two_half = bblk.shape[1], bblk.shape[2], bblk.shape[3]
    width = ngh * kb

    def chunk_idx(d, s):
        fwd = jnp.where(s < nc, nl + s, s - nc)
        rev = jnp.where(s < nc, nl + nc - 1 - s, nl - 1 - (s - nc))
        return jnp.where(d == 0, fwd, rev)

    nrows = SUBLANES
    kern = functools.partial(_s5_kernel, bsz=bsz, chunk=t_len)
    return pl.pallas_call(
        kern,
        out_shape=jax.ShapeDtypeStruct((2, bsz, r, width), F32),
        grid=(2, nl + nc),
        in_specs=[
            pl.BlockSpec((bsz, t_len, width), lambda d, s: (0, chunk_idx(d, s), col_blk)),
            pl.BlockSpec((1,) + bblk.shape[1:], lambda d, s: (d, 0, 0, 0)),
            pl.BlockSpec((1,) + cblk.shape[1:], lambda d, s: (d, 0, 0, 0)),
            pl.BlockSpec((1,) + lr.shape[1:], lambda d, s: (d, 0, 0)),
            pl.BlockSpec((1,) + li.shape[1:], lambda d, s: (d, 0, 0)),
        ],
        out_specs=pl.BlockSpec((1, bsz, t_len, width), lambda d, s: (d, 0, chunk_idx(d, s), 0)),
        scratch_shapes=[
            pltpu.VMEM((two_half // LANES * (ngh // 2), nrows * S5_PITCH, LANES), F32),
            pltpu.VMEM((nrows, (ngh // 2) * (two_half // 2)), F32),
            pltpu.VMEM((nrows, (ngh // 2) * (two_half // 2)), F32),
        ],
        compiler_params=_cparams(("arbitrary", "arbitrary")),
        name="s5_scan",
    )(p0, bblk, cblk, lr, li)


def _s5_operators(a_re, a_im, log_dt, b_re, b_im, c_re, c_im, bsz):
    f32 = F32
    ndir, g, p = a_re.shape
    cin = b_re.shape[-1]
    gl = 16
    ngh = g // gl
    ar, ai = a_re.astype(f32), a_im.astype(f32)
    dt = jnp.exp(log_dt.astype(f32))[..., None]
    mag = jnp.exp(ar * dt)
    lr = mag * jnp.cos(ai * dt)
    li = mag * jnp.sin(ai * dt)
    den = ar * ar + ai * ai
    zr = ((lr - 1.0) * ar + li * ai) / den
    zi = (li * ar - (lr - 1.0) * ai) / den
    br, bi = b_re.astype(f32), b_im.astype(f32)
    bbr = zr[..., None] * br - zi[..., None] * bi
    bbi = zr[..., None] * bi + zi[..., None] * br
    eye = jnp.eye(gl, dtype=f32)

    def in_block(m):
        m = m.reshape(ndir, ngh, gl, p, cin)
        return jnp.einsum('dhgpc,gk->dhgckp', m, eye).reshape(ndir, ngh, gl * cin, gl * p)

    bblk = jnp.concatenate([in_block(bbr), in_block(bbi)], axis=-1).astype(BF16)

    def out_block(m):
        m = m.reshape(ndir, ngh, gl, cin, p)
        return jnp.einsum('dhgcp,gk->dhgpkc', m, eye).reshape(ndir, ngh, gl * p, gl * cin)

    cblk = jnp.concatenate([out_block(c_re.astype(f32)), out_block(-c_im.astype(f32))], axis=-2).astype(BF16)

    def rows(v):
        per_seq = ngh // 2
        v = v.reshape(ndir, 2, 1, per_seq * gl * p)
        return jnp.broadcast_to(v, (ndir, 2, bsz, per_seq * gl * p)).reshape(ndir, 2 * bsz, per_seq * gl * p)

    return bblk, cblk, rows(lr), rows(li)


def _s5_glu_kernel(yf_ref, yr_ref, u_ref, sg_ref, d_ref, w_ref, b_ref, o_ref):
    y = u_ref[0].astype(F32) * d_ref[...] + yf_ref[0, 0] + yr_ref[0, 0]
    g = 0.5 * y * (1.0 + jnp.tanh(math.sqrt(2.0 / math.pi) * (y + 0.044715 * (y * y * y))))
    z = _dot(g.astype(BF16), w_ref[...]) + b_ref[...]
    sg = sg_ref[0].astype(F32)
    o_ref[0] = (g * jax.nn.sigmoid(z) * _silu(sg)).astype(o_ref.dtype)


def _s5_glu(ydir, p0, d, glu_w, glu_b, *, u_col_blk, sg_col_blk):
    _, bsz, r, width = ydir.shape
    tm = ROW_TILE
    return pl.pallas_call(
        _s5_glu_kernel,
        out_shape=jax.ShapeDtypeStruct((bsz, r, width), BF16),
        grid=(bsz, r // tm),
        in_specs=[
            pl.BlockSpec((1, 1, tm, width), lambda b, i: (0, b, i, 0)),
            pl.BlockSpec((1, 1, tm, width), lambda b, i: (1, b, i, 0)),
            pl.BlockSpec((1, tm, width), lambda b, i: (b, i, u_col_blk)),
            pl.BlockSpec((1, tm, width), lambda b, i: (b, i, sg_col_blk)),
            pl.BlockSpec((1, width), lambda b, i: (0, 0)),
            pl.BlockSpec((width, width), lambda b, i: (0, 0)),
            pl.BlockSpec((1, width), lambda b, i: (0, 0)),
        ],
        out_specs=pl.BlockSpec((1, tm, width), lambda b, i: (b, i, 0)),
        compiler_params=_cparams(("arbitrary", "arbitrary")),
        name="s5_glu",
    )(ydir, ydir, p0, p0, d.reshape(1, width).astype(F32), glu_w.astype(BF16),
      glu_b.reshape(1, width).astype(F32))


def _outproj0_kernel(hyl_ref, hyc_ref, hg_ref, s5_ref, x_ref, ctx_ref, modv_ref, w_ref, o_ref,
                     *, nl_tiles, bsz, d, hw):
    b = pl.program_id(0)
    i = pl.program_id(1)
    is_lat = i < nl_tiles
    hy = jnp.where(is_lat, hyl_ref[0], hyc_ref[0]).astype(F32)
    a = (hy * _silu(hg_ref[0].astype(F32))).astype(BF16)
    acc = _dot(a, w_ref[0:hw, :]) + _dot(s5_ref[0], w_ref[hw:, :])
    row = jnp.where(is_lat, b, bsz)
    gate = modv_ref[pl.ds(row, 1), :][:, 2 * d:3 * d]
    xin = jnp.where(is_lat, x_ref[0], ctx_ref[0])
    o_ref[0] = xin + gate * acc


def _outproj0(hy_l, hy_c, p0, s5g, x, ctx, modv, w, *, hg_col_blk):
    bsz, n, d = x.shape
    nc = ctx.shape[1]
    tm = ROW_TILE
    nl_t, nc_t = n // tm, nc // tm
    hw = hy_l.shape[2]
    kern = functools.partial(_outproj0_kernel, nl_tiles=nl_t, bsz=bsz, d=d, hw=hw)
    lat = lambda b, i: (b, jnp.minimum(i, nl_t - 1), 0)
    cx = lambda b, i: (b, jnp.maximum(i - nl_t, 0), 0)
    return pl.pallas_call(
        kern,
        out_shape=jax.ShapeDtypeStruct((bsz, n + nc, d), F32),
        grid=(bsz, nl_t + nc_t),
        in_specs=[
            pl.BlockSpec((1, tm, hw), lat),
            pl.BlockSpec((1, tm, hw), cx),
            pl.BlockSpec((1, tm, hw), lambda b, i: (b, i, hg_col_blk)),
            pl.BlockSpec((1, tm, hw), lambda b, i: (b, i, 0)),
            pl.BlockSpec((1, tm, d), lat),
            pl.BlockSpec((1, tm, d), cx),
            pl.BlockSpec(modv.shape, lambda b, i: (0, 0)),
            pl.BlockSpec(w.shape, lambda b, i: (0, 0)),
        ],
        out_specs=pl.BlockSpec((1, tm, d), lambda b, i: (b, i, 0)),
        compiler_params=_cparams(("arbitrary", "arbitrary")),
        name="outproj_even",
    )(hy_l, hy_c, p0, s5g, x, ctx, modv, w)


ATTN_SAFE_LOG2 = 57.0
ATTN_KEY_CHUNK = 256


def _attn_prepare(k_ref, v_ref, vt_ref, kn_ref):
    hd = v_ref.shape[2]
    vt_ref[0:hd, :] = v_ref[0].astype(F32).T.astype(BF16)
    vt_ref[hd:, :] = jnp.ones((vt_ref.shape[0] - hd, vt_ref.shape[1]), BF16)
    k = k_ref[0].astype(F32)
    ksq = k * k
    lane = lax.broadcasted_iota(jnp.int32, ksq.shape, 1)
    n1 = jnp.max(jnp.sum(jnp.where(lane < DA_HEAD, ksq, 0.0), axis=1, keepdims=True), axis=0, keepdims=True)
    n2 = jnp.max(jnp.sum(jnp.where(lane >= DA_HEAD, ksq, 0.0), axis=1, keepdims=True), axis=0, keepdims=True)
    kn_ref[0:1, :] = jnp.broadcast_to(n1, (1, kn_ref.shape[1]))
    kn_ref[1:2, :] = jnp.broadcast_to(n2, (1, kn_ref.shape[1]))


def _attn_kernel(q_ref, k_ref, v_ref, g_ref, lvec_ref, subln_ref, o_ref, vt_ref, kn_ref, st_ref, *, lam_init):
    @pl.when(pl.program_id(2) == 0)
    def _():
        _attn_prepare(k_ref, v_ref, vt_ref, kn_ref)

    q = q_ref[0]
    hd = q.shape[1]
    lane = lax.broadcasted_iota(jnp.int32, q.shape, 1)
    zero = jnp.zeros_like(q)
    qs = (jnp.where(lane < DA_HEAD, q, zero), jnp.where(lane >= DA_HEAD, q, zero))
    lv = lvec_ref[...]
    lam = (jnp.exp(jnp.sum(lv[0:1] * lv[1:2], axis=-1, keepdims=True))
           - jnp.exp(jnp.sum(lv[2:3] * lv[3:4], axis=-1, keepdims=True)) + lam_init)

    bound = None
    for m in range(2):
        qf = qs[m].astype(F32)
        qn = jnp.max(jnp.sum(qf * qf, axis=1, keepdims=True), axis=0, keepdims=True)
        bm = jnp.sqrt(qn * kn_ref[m:m + 1, 0:1])
        bound = bm if bound is None else jnp.maximum(bound, bm)
    safe = bound[0, 0] <= ATTN_SAFE_LOG2

    nkc = k_ref.shape[1] // ATTN_KEY_CHUNK

    def scores_t(c, m):
        kc = k_ref[0, c * ATTN_KEY_CHUNK:(c + 1) * ATTN_KEY_CHUNK, :]
        return lax.dot_general(kc, qs[m], (((1,), (1,)), ((), ())), preferred_element_type=F32)

    def attend(subtract_max):
        shifts = [None, None]
        if subtract_max:
            for m in range(2):
                for c in range(nkc):
                    cm = jnp.max(scores_t(c, m), axis=0, keepdims=True)
                    shifts[m] = cm if shifts[m] is None else jnp.maximum(shifts[m], cm)
        accs = [None, None]

        def stage(c):
            for m in range(2):
                st_ref[c % 2, m] = scores_t(c, m)

        stage(0)
        for c in range(nkc):
            if c + 1 < nkc:
                stage(c + 1)
            for m in range(2):
                st = st_ref[c % 2, m]
                if subtract_max:
                    st = st - shifts[m]
                p = jnp.exp2(st).astype(BF16)
                part = _dot(vt_ref[:, c * ATTN_KEY_CHUNK:(c + 1) * ATTN_KEY_CHUNK], p)
                accs[m] = part if accs[m] is None else accs[m] + part
        outs = [acc[0:hd] * (1.0 / acc[hd:hd + 1]) for acc in accs]
        o = (outs[0] - lam * outs[1]).T
        ms = jnp.mean(o * o, axis=-1, keepdims=True)
        o = o * lax.rsqrt(ms + EPS) * subln_ref[...] * (1.0 - lam_init)
        g = g_ref[0].astype(F32)
        o_ref[0] = (o * _silu(g)).astype(o_ref.dtype)

    @pl.when(safe)
    def _():
        attend(False)

    @pl.when(jnp.logical_not(safe))
    def _():
        attend(True)


def _attention(p1, lvec, subln, *, n_lat, heads, lam_init):
    bsz, r, _ = p1.shape
    hd = 2 * DA_HEAD
    tq = 512
    kern = functools.partial(_attn_kernel, lam_init=lam_init)
    return pl.pallas_call(
        kern,
        out_shape=jax.ShapeDtypeStruct((bsz, n_lat, heads * hd), BF16),
        grid=(bsz, heads, n_lat // tq),
        in_specs=[
            pl.BlockSpec((1, tq, hd), lambda b, h, i: (b, i, h)),
            pl.BlockSpec((1, r, hd), lambda b, h, i: (b, 0, heads + h)),
            pl.BlockSpec((1, r, hd), lambda b, h, i: (b, 0, 2 * heads + h)),
            pl.BlockSpec((1, tq, hd), lambda b, h, i: (b, i, 3 * heads + h)),
            pl.BlockSpec(lvec.shape, lambda b, h, i: (0, 0)),
            pl.BlockSpec((1, hd), lambda b, h, i: (0, 0)),
        ],
        out_specs=pl.BlockSpec((1, tq, hd), lambda b, h, i: (b, i, h)),
        scratch_shapes=[
            pltpu.VMEM((hd + 2 * SUBLANES, r), BF16),
            pltpu.VMEM((SUBLANES, LANES), F32),
            pltpu.VMEM((2, 2, ATTN_KEY_CHUNK, tq), F32),
        ],
        compiler_params=_cparams(("arbitrary", "arbitrary", "arbitrary")),
        name="diff_attention",
    )(p1, p1, p1, p1, lvec, subln)


def _outproj1_kernel(o_ref, x_ref, modv_ref, w_ref, out_ref, *, d):
    b = pl.program_id(0)
    acc = _dot(o_ref[0], w_ref[...])
    gate = modv_ref[pl.ds(b, 1), :][:, 2 * d:3 * d]
    out_ref[0] = x_ref[0] + gate * acc


def _outproj1(o, xc, modv, w):
    bsz, n, dv = o.shape
    d = xc.shape[2]
    tm = ROW_TILE
    return pl.pallas_call(
        functools.partial(_outproj1_kernel, d=d),
        out_shape=jax.ShapeDtypeStruct((bsz, n, d), F32),
        grid=(bsz, n // tm),
        in_specs=[
            pl.BlockSpec((1, tm, dv), lambda b, i: (b, i, 0)),
            pl.BlockSpec((1, tm, d), lambda b, i: (b, i, 0)),
            pl.BlockSpec(modv.shape, lambda b, i: (0, 0)),
            pl.BlockSpec(w.shape, lambda b, i: (0, 0)),
        ],
        out_specs=pl.BlockSpec((1, tm, d), lambda b, i: (b, i, 0)),
        compiler_params=_cparams(("arbitrary", "arbitrary")),
        name="outproj_odd",
    )(o, xc, modv, w)


def _rope_tables(n_lat, n_ctx):
    quarter = DA_HEAD // 4
    rows = n_lat // GRID_W
    row = jnp.broadcast_to(jnp.arange(rows, dtype=F32)[:, None], (rows, GRID_W)).reshape(n_lat)
    col = jnp.broadcast_to(jnp.arange(GRID_W, dtype=F32)[None, :], (rows, GRID_W)).reshape(n_lat)
    freqs = ROPE_BASE ** (-jnp.arange(quarter, dtype=F32) / quarter)
    ar = row[:, None] * freqs[None, :]
    ac = col[:, None] * freqs[None, :]
    cos = jnp.concatenate([jnp.cos(ar), jnp.cos(ar), jnp.cos(ac), jnp.cos(ac)], axis=-1)
    sin = jnp.concatenate([-jnp.sin(ar), jnp.sin(ar), -jnp.sin(ac), jnp.sin(ac)], axis=-1)
    cos = jnp.concatenate([cos, cos], axis=-1)
    sin = jnp.concatenate([sin, sin], axis=-1)
    cos = jnp.concatenate([cos, jnp.ones((n_ctx, 2 * DA_HEAD), F32)], axis=0)
    sin = jnp.concatenate([sin, jnp.zeros((n_ctx, 2 * DA_HEAD), F32)], axis=0)
    return cos, sin


def kernel(x, c, ctx, c_ctx, mod_w, mod_b, norm_w, ev_in_w, ev_out_w, hy_conv_w, hy_conv_b, hy_w1, hy_b1, hy_w2, hy_b2, hy_w3, hy_b3, hy_freq, hy_skip, s5_a_re, s5_a_im, s5_log_dt, s5_b_re, s5_b_im, s5_c_re, s5_c_im, s5_d, s5_glu_w, s5_glu_b, od_in_w, od_out_w, da_q_norm, da_k_norm, da_lq1, da_lk1, da_lq2, da_lk2, da_subln):
    bsz, n, d = x.shape
    nc = ctx.shape[1]
    assert n % ROW_TILE == 0 and nc % ROW_TILE == 0 and 2 * bsz == SUBLANES
    depth = mod_w.shape[0]
    assert depth == 2

    npad = SUBLANES * ((bsz + 1 + SUBLANES - 1) // SUBLANES)
    cvec = jnp.concatenate([c, c_ctx[None, :], jnp.zeros((npad - bsz - 1, d), F32)], axis=0)
    modv = _mod_vectors(cvec, mod_w, mod_b)

    hw = hy_skip.shape[-1]
    sw = s5_d.shape[-1]
    p0 = _inproj0(x, ctx, modv[0], norm_w[0:1], ev_in_w[0].astype(BF16))
    sc = _shortconv(p0, hy_conv_w[0].astype(F32), hy_conv_b[0].astype(F32), n)
    filt = (hy_w1[0], hy_b1[0], hy_w2[0], hy_b2[0], hy_w3[0], hy_b3[0], hy_freq[0])
    hy_l = _hyena_seq(sc, n, 0, filt, hy_skip[0], hw)
    hy_c = _hyena_seq(sc, nc, n // nc, filt, hy_skip[0], hw)

    bblk, cblk, lr, li = _s5_operators(s5_a_re[0], s5_a_im[0], s5_log_dt[0], s5_b_re[0], s5_b_im[0],
                                       s5_c_re[0], s5_c_im[0], bsz)
    su_blk = ((HY_ORDER + 2) * hw) // sw
    ydir = _s5_scan(p0, bblk, cblk, lr, li, n_lat=n, n_ctx=nc, col_blk=su_blk)
    s5g = _s5_glu(ydir, p0, s5_d[0], s5_glu_w[0], s5_glu_b[0], u_col_blk=su_blk, sg_col_blk=su_blk + 1)
    x1 = _outproj0(hy_l, hy_c, p0, s5g, x, ctx, modv[0], ev_out_w[0].astype(BF16),
                   hg_col_blk=(HY_ORDER + 1))

    heads = d // (2 * DA_HEAD)
    reps = d // DA_HEAD
    qscale = DA_HEAD ** -0.5 * math.log2(math.e)
    qkw = jnp.stack([jnp.tile(da_q_norm[0].astype(F32), reps) * qscale, jnp.tile(da_k_norm[0].astype(F32), reps)])
    qkw = jnp.concatenate([qkw, jnp.zeros((SUBLANES - 2, d), F32)], axis=0)
    gidx = jnp.arange(2 * LANES) // DA_HEAD
    gm = (gidx[:, None] == gidx[None, :]).astype(BF16) * (1.0 / DA_HEAD)
    cos_t, sin_t = _rope_tables(n, nc)
    p1 = _inproj1(x1, modv[1], norm_w[1:2], od_in_w[0].astype(BF16), qkw, gm.astype(BF16), cos_t, sin_t, n)
    lam_init = 0.8 - 0.6 * math.exp(-0.3 * 1)
    lvec = jnp.stack([da_lq1[0], da_lk1[0], da_lq2[0], da_lk2[0]]).astype(F32)
    lvec = jnp.pad(lvec, ((0, SUBLANES - 4), (0, LANES - lvec.shape[1])))
    o = _attention(p1, lvec, da_subln[0].reshape(1, 2 * DA_HEAD).astype(F32), n_lat=n, heads=heads,
                   lam_init=lam_init)
    return _outproj1(o, x1, modv[1], od_out_w[0].astype(BF16))
```

```python
import functools
import math

import jax
import jax.numpy as jnp
from jax import lax
from jax.experimental import pallas as pl
from jax.experimental.pallas import tpu as pltpu

F32 = jnp.float32
BF16 = jnp.bfloat16
HIGHEST = lax.Precision.HIGHEST

EPS = 1e-6
ROW_TILE = 256
LANES = 128
SUBLANES = 8
VMEM_LIMIT = 56 * 1024 * 1024

HY_ORDER = 2
HY_BANDS = 16
HY_TARGET = 1e-2
HY_FAST_PCT = 0.3
HY_SLOW_PCT = 1.5
S5_GROUP = 16
S5_STATE = 64
DA_HEAD = 64
GRID_W = 64
ROPE_BASE = 10000.0

S5_CHUNK = 128
S5_PITCH = S5_CHUNK + 4


def _cparams(sem):
    return pltpu.CompilerParams(dimension_semantics=sem, vmem_limit_bytes=VMEM_LIMIT)


def _silu(x):
    return x * jax.nn.sigmoid(x)


def _dot(a, b):
    return jnp.dot(a, b, preferred_element_type=F32)


def _mod_kernel(c_ref, w_ref, b_ref, o_ref):
    a = _silu(c_ref[...])
    o_ref[0] = jnp.dot(a, w_ref[0], precision=HIGHEST, preferred_element_type=F32) + b_ref[0]


def _mod_vectors(cvec, mod_w, mod_b):
    depth, d, d3 = mod_w.shape
    tn = 1024
    return pl.pallas_call(
        _mod_kernel,
        out_shape=jax.ShapeDtypeStruct((depth, cvec.shape[0], d3), F32),
        grid=(depth, d3 // tn),
        in_specs=[
            pl.BlockSpec(cvec.shape, lambda l, j: (0, 0)),
            pl.BlockSpec((1, d, tn), lambda l, j: (l, 0, j)),
            pl.BlockSpec((1, 1, tn), lambda l, j: (l, 0, j)),
        ],
        out_specs=pl.BlockSpec((1, cvec.shape[0], tn), lambda l, j: (l, 0, j)),
        compiler_params=_cparams(("arbitrary", "arbitrary")),
        name="mod_vectors",
    )(cvec, mod_w, mod_b.reshape(depth, 1, d3))


def _normed(x, modv_ref, nw_ref, row, d):
    m = modv_ref[pl.ds(row, 1), :]
    shift = m[:, 0:d]
    scale = m[:, d:2 * d]
    ms = jnp.mean(x * x, axis=-1, keepdims=True)
    y = x * lax.rsqrt(ms + EPS) * nw_ref[...]
    return (y * (1.0 + scale) + shift).astype(BF16)


def _inproj0_kernel(x_ref, ctx_ref, modv_ref, nw_ref, w_ref, o_ref, *, nl_tiles, bsz, d):
    b = pl.program_id(1)
    i = pl.program_id(2)
    is_lat = i < nl_tiles
    x = jnp.where(is_lat, x_ref[0], ctx_ref[0])
    row = jnp.where(is_lat, b, bsz)
    h = _normed(x, modv_ref, nw_ref, row, d)
    o_ref[0] = _dot(h, w_ref[...]).astype(o_ref.dtype)


def _inproj0(x, ctx, modv, nw, w):
    bsz, n, d = x.shape
    nc = ctx.shape[1]
    tm = ROW_TILE
    nl_t, nc_t = n // tm, nc // tm
    nout = w.shape[1]
    tn = 2048
    kern = functools.partial(_inproj0_kernel, nl_tiles=nl_t, bsz=bsz, d=d)
    return pl.pallas_call(
        kern,
        out_shape=jax.ShapeDtypeStruct((bsz, n + nc, nout), BF16),
        grid=(nout // tn, bsz, nl_t + nc_t),
        in_specs=[
            pl.BlockSpec((1, tm, d), lambda j, b, i: (b, jnp.minimum(i, nl_t - 1), 0)),
            pl.BlockSpec((1, tm, d), lambda j, b, i: (b, jnp.maximum(i - nl_t, 0), 0)),
            pl.BlockSpec(modv.shape, lambda j, b, i: (0, 0)),
            pl.BlockSpec((1, d), lambda j, b, i: (0, 0)),
            pl.BlockSpec((d, tn), lambda j, b, i: (0, j)),
        ],
        out_specs=pl.BlockSpec((1, tm, tn), lambda j, b, i: (b, i, j)),
        compiler_params=_cparams(("arbitrary", "arbitrary", "arbitrary")),
        name="inproj_even",
    )(x, ctx, modv, nw, w)


def _inproj1_kernel(x_ref, modv_ref, nw_ref, w_ref, qkw_ref, gm_ref, pm_ref, cos_ref, sin_ref, o_ref,
                    *, nl_tiles, bsz, d):
    j = pl.program_id(0)
    b = pl.program_id(1)
    i = pl.program_id(2)
    is_lat = i < nl_tiles
    row = jnp.where(is_lat, b, bsz)
    needed = jnp.logical_or(is_lat, jnp.logical_or(j == 1, j == 2))

    @pl.when(jnp.logical_not(needed))
    def _():
        o_ref[...] = jnp.zeros(o_ref.shape, o_ref.dtype)

    @pl.when(needed)
    def _():
        h = _normed(x_ref[0], modv_ref, nw_ref, row, d)
        acc = _dot(h, w_ref[...])

        @pl.when(j >= 2)
        def _():
            o_ref[0] = acc.astype(o_ref.dtype)

        @pl.when(j < 2)
        def _():
            sq = (acc * acc).astype(BF16)
            gm = gm_ref[...]
            width = gm.shape[0]
            ms = jnp.concatenate(
                [_dot(sq[:, t * width:(t + 1) * width], gm) for t in range(acc.shape[1] // width)], axis=1)
            wsel = jnp.where(j == 0, qkw_ref[0:1, :], qkw_ref[1:2, :])
            yn = acc * lax.rsqrt(ms + EPS) * wsel
            reps = acc.shape[1] // cos_ref.shape[1]
            cos = jnp.concatenate([cos_ref[...]] * reps, axis=1)
            sin = jnp.concatenate([sin_ref[...]] * reps, axis=1)
            ynb = yn.astype(BF16)
            pm = pm_ref[...]
            swapped = jnp.concatenate(
                [_dot(ynb[:, t * width:(t + 1) * width], pm) for t in range(acc.shape[1] // width)], axis=1)
            o_ref[0] = (yn * cos + swapped * sin).astype(o_ref.dtype)


def _inproj1(xc, modv, nw, w, qkw, gm, pm, cos_t, sin_t, n_lat):
    bsz, r, d = xc.shape
    tm = ROW_TILE
    nl_t = n_lat // tm
    nout = w.shape[1]
    tn = 2048
    kern = functools.partial(_inproj1_kernel, nl_tiles=nl_t, bsz=bsz, d=d)
    return pl.pallas_call(
        kern,
        out_shape=jax.ShapeDtypeStruct((bsz, r, nout), BF16),
        grid=(nout // tn, bsz, r // tm),
        in_specs=[
            pl.BlockSpec((1, tm, d), lambda j, b, i: (b, i, 0)),
            pl.BlockSpec(modv.shape, lambda j, b, i: (0, 0)),
            pl.BlockSpec((1, d), lambda j, b, i: (0, 0)),
            pl.BlockSpec((d, tn), lambda j, b, i: (0, j)),
            pl.BlockSpec(qkw.shape, lambda j, b, i: (0, 0)),
            pl.BlockSpec(gm.shape, lambda j, b, i: (0, 0)),
            pl.BlockSpec(pm.shape, lambda j, b, i: (0, 0)),
            pl.BlockSpec((tm, cos_t.shape[1]), lambda j, b, i: (i, 0)),
            pl.BlockSpec((tm, sin_t.shape[1]), lambda j, b, i: (i, 0)),
        ],
        out_specs=pl.BlockSpec((1, tm, tn), lambda j, b, i: (b, i, j)),
        compiler_params=_cparams(("arbitrary", "arbitrary", "arbitrary")),
        name="inproj_odd",
    )(xc, modv, nw, w, qkw, gm, pm, cos_t, sin_t)


def _shortconv_kernel(p_ref, w_ref, b_ref, o_ref, *, n_lat):
    p = p_ref[0].astype(F32)
    r = p.shape[0]
    row = lax.broadcasted_iota(jnp.int32, p.shape, 0)
    prev = pltpu.roll(p, 1, axis=0)
    nxt = pltpu.roll(p, r - 1, axis=0)
    prev = jnp.where((row == 0) | (row == n_lat), 0.0, prev)
    nxt = jnp.where((row == n_lat - 1) | (row == r - 1), 0.0, nxt)
    w = w_ref[...]
    o_ref[0] = (prev * w[0:1] + p * w[1:2] + nxt * w[2:3] + b_ref[...]).astype(o_ref.dtype)


def _shortconv(p0, conv_w, conv_b, n_lat):
    bsz, r, _ = p0.shape
    width = conv_w.shape[1]
    tc = 512
    return pl.pallas_call(
        functools.partial(_shortconv_kernel, n_lat=n_lat),
        out_shape=jax.ShapeDtypeStruct((bsz, r, width), BF16),
        grid=(bsz, width // tc),
        in_specs=[
            pl.BlockSpec((1, r, tc), lambda b, j: (b, 0, j)),
            pl.BlockSpec((3, tc), lambda b, j: (0, j)),
            pl.BlockSpec((1, tc), lambda b, j: (0, j)),
        ],
        out_specs=pl.BlockSpec((1, r, tc), lambda b, j: (b, 0, j)),
        compiler_params=_cparams(("arbitrary", "arbitrary")),
        name="hyena_shortconv",
    )(p0, conv_w, conv_b.reshape(1, width))


def _filter_kernel(feat_ref, w1_ref, b1_ref, w2_ref, b2_ref, w3_ref, b3_ref, freq_ref, delta_ref,
                   h_ref, l1_ref):
    ti = pl.program_id(0)
    feat = feat_ref[...]
    tt = feat.shape[0]
    width = delta_ref.shape[1]
    z1 = jnp.dot(feat, w1_ref[...], precision=HIGHEST, preferred_element_type=F32) + b1_ref[...]
    h1 = jnp.sin(freq_ref[0:1, :] * z1)
    z2 = jnp.dot(h1, w2_ref[...], precision=HIGHEST, preferred_element_type=F32) + b2_ref[...]
    h2 = jnp.sin(freq_ref[1:2, :] * z2)
    decay = jnp.exp(-feat[:, 0:1] * delta_ref[...])
    row = lax.broadcasted_iota(jnp.int32, decay.shape, 0) + ti * tt

    @pl.when(ti == 0)
    def _():
        l1_ref[...] = jnp.zeros(l1_ref.shape, l1_ref.dtype)

    for q in range(w3_ref.shape[1] // width):
        cols = slice(q * width, (q + 1) * width)
        z3 = jnp.dot(h2, w3_ref[:, cols], precision=HIGHEST, preferred_element_type=F32) + b3_ref[:, cols]
        hf = z3 * decay
        if q % 2 == 1:
            hf = jnp.where(row == 0, 0.0, hf)
        h_ref[:, cols] = hf.astype(h_ref.dtype)
        l1_ref[:, cols] += jnp.sum(jnp.abs(hf).reshape(tt // SUBLANES, SUBLANES, width), axis=0)


def _hyena_filter_taps(n, w1, b1, w2, b2, w3, b3, freq, width):
    t = jnp.arange(n, dtype=F32)
    tn = t / n
    bands = jnp.linspace(1e-4, HY_BANDS - 1, HY_BANDS, dtype=F32)
    ang = (2.0 * math.pi / n) * t[:, None] * bands[None, :]
    feat = jnp.concatenate([tn[:, None], jnp.cos(ang), -jnp.sin(ang)], axis=-1)
    emb = feat.shape[1]
    feat = jnp.pad(feat, ((0, 0), (0, LANES - emb)))
    w1p = jnp.pad(w1.astype(F32), ((0, LANES - emb), (0, 0)))
    ffn = w1.shape[1]
    deltas = jnp.abs(jnp.linspace(math.log(HY_TARGET) / HY_SLOW_PCT, math.log(HY_TARGET) / HY_FAST_PCT,
                                  width, dtype=F32)).reshape(1, width)
    ncol = w3.shape[1]
    tt = min(512, n)
    full = lambda i: (0, 0)
    return pl.pallas_call(
        _filter_kernel,
        out_shape=(jax.ShapeDtypeStruct((n, ncol), BF16), jax.ShapeDtypeStruct((SUBLANES, ncol), F32)),
        grid=(n // tt,),
        in_specs=[
            pl.BlockSpec((tt, LANES), lambda i: (i, 0)),
            pl.BlockSpec((LANES, ffn), full),
            pl.BlockSpec((1, ffn), full),
            pl.BlockSpec((ffn, ffn), full),
            pl.BlockSpec((1, ffn), full),
            pl.BlockSpec((ffn, ncol), full),
            pl.BlockSpec((1, ncol), full),
            pl.BlockSpec((2, ffn), full),
            pl.BlockSpec((1, width), full),
        ],
        out_specs=(pl.BlockSpec((tt, ncol), lambda i: (i, 0)),
                   pl.BlockSpec((SUBLANES, ncol), full)),
        compiler_params=_cparams(("arbitrary",)),
        name="hyena_filter_taps",
    )(feat, w1p, b1.reshape(1, ffn).astype(F32), w2.astype(F32), b2.reshape(1, ffn).astype(F32),
      w3.astype(F32), b3.reshape(1, ncol).astype(F32), freq.astype(F32), deltas)


def _dft_tables(n):
    blk = 64
    hi = jnp.arange(n // blk, dtype=jnp.int32)[:, None]
    lo = jnp.arange(blk, dtype=jnp.int32)[:, None]
    other = jnp.arange(n, dtype=jnp.int32)[None, :]
    period = 4 * n
    unit = math.pi / (2 * n)

    def cs(m):
        a = (m % period).astype(F32) * unit
        return jnp.cos(a), jnp.sin(a)

    def combine(ca, sa, cb, sb):
        c = ca[:, None, :] * cb[None, :, :] - sa[:, None, :] * sb[None, :, :]
        s = sa[:, None, :] * cb[None, :, :] + ca[:, None, :] * sb[None, :, :]
        return c.reshape(n, n).astype(BF16), (-s).reshape(n, n).astype(BF16)

    fc, fs = combine(*cs(2 * blk * hi * other), *cs((2 * lo + 1) * other))
    tc, ts = combine(*cs((2 * other + 1) * blk * hi), *cs((2 * other + 1) * lo))
    return fc, fs, tc, ts


def _dft_fwd_kernel(fc_ref, fs_ref, z_ref, o_re_ref, o_im_ref):
    z = z_ref[...].astype(BF16)
    o_re_ref[...] = _dot(fc_ref[...], z)
    o_im_ref[...] = _dot(fs_ref[...], z)


def _filter_spectrum(fc, fs, taps):
    n, ncol = taps.shape
    tf = min(512, n)
    tcn = 512
    return pl.pallas_call(
        _dft_fwd_kernel,
        out_shape=(jax.ShapeDtypeStruct((n, ncol), F32), jax.ShapeDtypeStruct((n, ncol), F32)),
        grid=(ncol // tcn, n // tf),
        in_specs=[
            pl.BlockSpec((tf, n), lambda j, f: (f, 0)),
            pl.BlockSpec((tf, n), lambda j, f: (f, 0)),
            pl.BlockSpec((n, tcn), lambda j, f: (0, j)),
        ],
        out_specs=(pl.BlockSpec((tf, tcn), lambda j, f: (f, j)),
                   pl.BlockSpec((tf, tcn), lambda j, f: (f, j))),
        compiler_params=_cparams(("arbitrary", "arbitrary")),
        name="hyena_filter_spectrum",
    )(fc, fs, taps)


def _dft_fwd_mul_kernel(fc_ref, fs_ref, z_ref, kr_ref, ki_ref, pr_ref, pi_ref):
    z = z_ref[0]
    ur = _dot(fc_ref[...], z)
    ui = _dot(fs_ref[...], z)
    kr = kr_ref[...]
    ki = ki_ref[...]
    pr_ref[0] = (ur * kr - ui * ki).astype(pr_ref.dtype)
    pi_ref[0] = (ur * ki + ui * kr).astype(pi_ref.dtype)


def _dft_fwd_mul(fc, fs, z, kr, ki, *, n, row_blk, col_blk0):
    bsz = z.shape[0]
    width = kr.shape[1]
    tf = min(512, n)
    tcn = 512
    return pl.pallas_call(
        _dft_fwd_mul_kernel,
        out_shape=(jax.ShapeDtypeStruct((bsz, n, width), BF16), jax.ShapeDtypeStruct((bsz, n, width), BF16)),
        grid=(width // tcn, n // tf, bsz),
        in_specs=[
            pl.BlockSpec((tf, n), lambda j, f, b: (f, 0)),
            pl.BlockSpec((tf, n), lambda j, f, b: (f, 0)),
            pl.BlockSpec((1, n, tcn), lambda j, f, b: (b, row_blk, col_blk0 + j)),
            pl.BlockSpec((tf, tcn), lambda j, f, b: (f, j)),
            pl.BlockSpec((tf, tcn), lambda j, f, b: (f, j)),
        ],
        out_specs=(pl.BlockSpec((1, tf, tcn), lambda j, f, b: (b, f, j)),
                   pl.BlockSpec((1, tf, tcn), lambda j, f, b: (b, f, j))),
        compiler_params=_cparams(("arbitrary", "arbitrary", "arbitrary")),
        name="hyena_dft_fwd",
    )(fc, fs, z, kr, ki)


def _dft_inv_gate_kernel(tc_ref, ts_ref, pr_ref, pi_ref, z_ref, xg_ref, skip_ref, o_ref):
    y = _dot(tc_ref[...], pr_ref[0]) + _dot(ts_ref[...], pi_ref[0])
    z = z_ref[0].astype(F32)
    xg = xg_ref[0].astype(F32)
    o_ref[0] = (xg * (y + z * skip_ref[...])).astype(o_ref.dtype)


def _dft_inv_gate(tc, ts, pr, pi, z, xg, skip, *, n, z_row_blk, z_col_blk0, xg_row_blk, xg_col_blk0):
    bsz, _, width = pr.shape
    tt = min(512, n)
    tcn = 512
    zrb = z_row_blk * (n // tt)
    xrb = xg_row_blk * (n // tt)
    return pl.pallas_call(
        _dft_inv_gate_kernel,
        out_shape=jax.ShapeDtypeStruct((bsz, n, width), BF16),
        grid=(width // tcn, bsz, n // tt),
        in_specs=[
            pl.BlockSpec((tt, n), lambda j, b, i: (i, 0)),
            pl.BlockSpec((tt, n), lambda j, b, i: (i, 0)),
            pl.BlockSpec((1, n, tcn), lambda j, b, i: (b, 0, j)),
            pl.BlockSpec((1, n, tcn), lambda j, b, i: (b, 0, j)),
            pl.BlockSpec((1, tt, tcn), lambda j, b, i: (b, zrb + i, z_col_blk0 + j)),
            pl.BlockSpec((1, tt, tcn), lambda j, b, i: (b, xrb + i, xg_col_blk0 + j)),
            pl.BlockSpec((1, tcn), lambda j, b, i: (0, j)),
        ],
        out_specs=pl.BlockSpec((1, tt, tcn), lambda j, b, i: (b, i, j)),
        compiler_params=_cparams(("arbitrary", "arbitrary", "arbitrary")),
        name="hyena_dft_inv",
    )(tc, ts, pr, pi, z, xg, skip)


FFT_N2 = 256
FFT_GROUP = 16


def _fft_tables(n):
    n1 = (2 * n) // FFT_N2
    h1 = n1 // 2
    nf2 = n // n1
    f1 = jnp.arange(n1, dtype=jnp.int32)
    t1 = jnp.arange(h1, dtype=jnp.int32)
    th = ((2 * f1[:, None] + 1) * t1[None, :] % (2 * n1)).astype(F32) * (math.pi / n1)
    base = jnp.stack([jnp.cos(th), -jnp.sin(th)])
    eye = jnp.eye(FFT_GROUP, dtype=F32)
    m1 = jnp.einsum('pft,jk->pfjtk', base, eye).reshape(2 * n1 * FFT_GROUP, h1 * FFT_GROUP)
    f = f1[:, None, None] + n1 * jnp.arange(nf2, dtype=jnp.int32)[None, :, None]
    t2 = jnp.arange(FFT_N2, dtype=jnp.int32)[None, None, :]
    psi = ((2 * f + 1) * t2 % (4 * n)).astype(F32) * (math.pi / (2 * n))
    cs = jnp.concatenate([jnp.cos(psi), jnp.sin(psi)], axis=1)
    return (m1.astype(BF16), m1.T.astype(BF16), cs.astype(BF16), jnp.swapaxes(cs, 1, 2).astype(BF16))


def _fft_stage1(z_ref, m1_ref, a_ref, n1):
    h1 = n1 // 2

    def body(g, carry):
        r0 = pl.multiple_of(g * FFT_GROUP, FFT_GROUP)
        xg = jnp.concatenate(
            [z_ref[pl.ds(pl.multiple_of(FFT_N2 * t1 + r0, FFT_GROUP), FFT_GROUP), :] for t1 in range(h1)], axis=0)
        out = _dot(m1_ref[...], xg).astype(BF16)
        for part in range(2):
            for f1 in range(n1):
                row = (part * n1 + f1) * FFT_GROUP
                a_ref[part, f1, pl.ds(r0, FFT_GROUP), :] = out[row:row + FFT_GROUP]
        return carry

    lax.fori_loop(0, FFT_N2 // FFT_GROUP, body, 0, unroll=4)


def _fft_stage2(a_ref, cs_ref, f1):
    rhs = jnp.concatenate([a_ref[0, f1], a_ref[1, f1]], axis=1)
    r = _dot(cs_ref[f1], rhs)
    nf2 = r.shape[0] // 2
    w = r.shape[1] // 2
    xr = r[0:nf2, 0:w] + r[nf2:, w:]
    xi = r[0:nf2, w:] - r[nf2:, 0:w]
    return xr, xi


def _fft_filter_kernel(hf_ref, hb_ref, l1f_ref, l1b_ref, m1_ref, cs_ref, o_ref, sd_ref, a_ref, *, n1, n):
    hf = hf_ref[...].astype(F32)
    hb = hb_ref[...].astype(F32)
    sd_ref[0] = (hf + hb).astype(BF16)
    sd_ref[1] = (hf - hb).astype(BF16)
    _fft_stage1(sd_ref.at[0], m1_ref, a_ref.at[0], n1)
    _fft_stage1(sd_ref.at[1], m1_ref, a_ref.at[1], n1)
    l1 = jnp.sum(l1f_ref[...], axis=0, keepdims=True) + jnp.sum(l1b_ref[...], axis=0, keepdims=True)
    norm = (1.0 / n) / l1

    def body(f1, carry):
        cs = cs_ref[f1]
        nf2 = cs.shape[0] // 2
        c, s = cs[0:nf2], cs[nf2:]
        kr = _dot(c, a_ref[0, 0, f1]) + _dot(s, a_ref[0, 1, f1])
        ki = _dot(c, a_ref[1, 1, f1]) - _dot(s, a_ref[1, 0, f1])
        rows = pl.ds(pl.multiple_of(f1 * nf2, nf2), nf2)
        o_ref[0, 0, rows, :] = (kr * norm).astype(o_ref.dtype)
        o_ref[0, 1, rows, :] = (ki * norm).astype(o_ref.dtype)
        return carry

    lax.fori_loop(0, n1, body, 0, unroll=min(4, n1))


def _const_spec(arr):
    nd = arr.ndim
    return pl.BlockSpec(arr.shape, lambda *_: (0,) * nd)


def _fft_filter_spectrum(taps, l1, m1, cs, width):
    n, _ = taps.shape
    n1 = cs.shape[0]
    tcn = 256
    cb = width // tcn
    return pl.pallas_call(
        functools.partial(_fft_filter_kernel, n1=n1, n=n),
        out_shape=jax.ShapeDtypeStruct((HY_ORDER, 2, n, width), BF16),
        grid=(HY_ORDER, cb),
        in_specs=[
            pl.BlockSpec((n, tcn), lambda i, j: (0, (2 * i) * cb + j)),
            pl.BlockSpec((n, tcn), lambda i, j: (0, (2 * i + 1) * cb + j)),
            pl.BlockSpec((SUBLANES, tcn), lambda i, j: (0, (2 * i) * cb + j)),
            pl.BlockSpec((SUBLANES, tcn), lambda i, j: (0, (2 * i + 1) * cb + j)),
            _const_spec(m1), _const_spec(cs),
        ],
        out_specs=pl.BlockSpec((1, 2, n, tcn), lambda i, j: (i, 0, 0, j)),
        scratch_shapes=[pltpu.VMEM((2, n, tcn), BF16), pltpu.VMEM((2, 2, n1, FFT_N2, tcn), BF16)],
        compiler_params=_cparams(("arbitrary", "arbitrary")),
        name="hyena_fft_filter_spectrum",
    )(taps, taps, l1, l1, m1, cs)


def _fft_conv_kernel(z_ref, xg_ref, k_ref, skip_ref, m1_ref, m1t_ref, cs_ref, cst_ref, o_ref, a_ref, *, n1):
    h1 = n1 // 2
    _fft_stage1(z_ref.at[0], m1_ref, a_ref, n1)

    def mid(f1, carry):
        xr, xi = _fft_stage2(a_ref, cs_ref, f1)
        nf2 = xr.shape[0]
        rows = pl.ds(pl.multiple_of(f1 * nf2, nf2), nf2)
        kr = k_ref[0, 0, rows, :].astype(F32)
        ki = k_ref[0, 1, rows, :].astype(F32)
        pr = xr * kr - xi * ki
        pi = xr * ki + xi * kr
        rhs = jnp.concatenate([jnp.concatenate([pr, pi], axis=1), jnp.concatenate([-pi, pr], axis=1)],
                              axis=0).astype(BF16)
        d = _dot(cst_ref[f1], rhs).astype(BF16)
        w = d.shape[1] // 2
        a_ref[0, f1] = d[:, 0:w]
        a_ref[1, f1] = d[:, w:]
        return carry

    lax.fori_loop(0, n1, mid, 0, unroll=min(8, n1))

    skip = skip_ref[...]

    def last(g, carry):
        r0 = pl.multiple_of(g * FFT_GROUP, FFT_GROUP)
        dg = jnp.concatenate(
            [a_ref[part, f1, pl.ds(r0, FFT_GROUP), :] for part in range(2) for f1 in range(n1)], axis=0)
        yg = _dot(m1t_ref[...], dg)
        for t1 in range(h1):
            rows = pl.ds(pl.multiple_of(FFT_N2 * t1 + r0, FFT_GROUP), FFT_GROUP)
            z = z_ref[0, rows, :].astype(F32)
            xg = xg_ref[0, rows, :].astype(F32)
            o_ref[0, rows, :] = (xg * (yg[t1 * FFT_GROUP:(t1 + 1) * FFT_GROUP] + z * skip)).astype(o_ref.dtype)
        return carry

    lax.fori_loop(0, FFT_N2 // FFT_GROUP, last, 0, unroll=4)


def _fft_conv_gate(z, xg, kspec, order, skip, tables, *, n, z_row_blk, z_col_blk0, xg_row_blk, xg_col_blk0):
    m1, m1t, cs, cst = tables
    bsz = z.shape[0]
    width = kspec.shape[3]
    n1 = cs.shape[0]
    tcn = 256
    return pl.pallas_call(
        functools.partial(_fft_conv_kernel, n1=n1),
        out_shape=jax.ShapeDtypeStruct((bsz, n, width), BF16),
        grid=(width // tcn, bsz),
        in_specs=[
            pl.BlockSpec((1, n, tcn), lambda j, b: (b, z_row_blk, z_col_blk0 + j)),
            pl.BlockSpec((1, n, tcn), lambda j, b: (b, xg_row_blk, xg_col_blk0 + j)),
            pl.BlockSpec((1, 2, n, tcn), lambda j, b: (order, 0, 0, j)),
            pl.BlockSpec((1, tcn), lambda j, b: (0, j)),
            _const_spec(m1), _const_spec(m1t), _const_spec(cs), _const_spec(cst),
        ],
        out_specs=pl.BlockSpec((1, n, tcn), lambda j, b: (b, 0, j)),
        scratch_shapes=[pltpu.VMEM((2, n1, FFT_N2, tcn), BF16)],
        compiler_params=_cparams(("arbitrary", "arbitrary")),
        name="hyena_fft_conv",
    )(z, xg, kspec, skip, m1, m1t, cs, cst)


def _hyena_seq_fft(sc, n, seq_row_blk, filt, skip, width):
    w1, b1, w2, b2, w3, b3, freq = filt
    taps, l1 = _hyena_filter_taps(n, w1, b1, w2, b2, w3, b3, freq, width)
    tables = _fft_tables(n)
    kspec = _fft_filter_spectrum(taps, l1, tables[0], tables[2], width)
    cb = width // 256
    y = None
    for i in range(HY_ORDER):
        if i == 0:
            z, zrb, zcb = sc, seq_row_blk, 0
        else:
            z, zrb, zcb = y, 0, 0
        y = _fft_conv_gate(z, sc, kspec, i, skip[i].reshape(1, width).astype(F32), tables, n=n,
                           z_row_blk=zrb, z_col_blk0=zcb, xg_row_blk=seq_row_blk, xg_col_blk0=(i + 1) * cb)
    return y


def _hyena_seq(sc, n, seq_row_blk, filt, skip, width):
    if n % (2 * FFT_N2) == 0:
        return _hyena_seq_fft(sc, n, seq_row_blk, filt, skip, width)
    w1, b1, w2, b2, w3, b3, freq = filt
    taps, l1 = _hyena_filter_taps(n, w1, b1, w2, b2, w3, b3, freq, width)
    fc, fs, tc, ts = _dft_tables(n)
    sr, si = _filter_spectrum(fc, fs, taps)
    sr = sr.reshape(n, HY_ORDER, 2, width)
    si = si.reshape(n, HY_ORDER, 2, width)
    l1 = jnp.sum(l1, axis=0).reshape(HY_ORDER, 2, width).sum(axis=1)
    norm = (1.0 / n) / l1
    cb = width // 512
    y = None
    for i in range(HY_ORDER):
        kr = (sr[:, i, 0] + sr[:, i, 1]) * norm[i]
        ki = (si[:, i, 0] - si[:, i, 1]) * norm[i]
        if i == 0:
            z, zrb, zcb = sc, seq_row_blk, 0
        else:
            z, zrb, zcb = y, 0, 0
        pr, pi = _dft_fwd_mul(fc, fs, z, kr, ki, n=n, row_blk=zrb, col_blk0=zcb)
        y = _dft_inv_gate(tc, ts, pr, pi, z, sc, skip[i].reshape(1, width).astype(F32), n=n,
                          z_row_blk=zrb, z_col_blk0=zcb, xg_row_blk=seq_row_blk, xg_col_blk0=(i + 1) * cb)
    return y


def _s5_kernel(u_ref, bblk_ref, cblk_ref, lr_ref, li_ref, y_ref, st_ref, cr_ref, ci_ref, *, bsz, chunk):
    d = pl.program_id(0)
    s = pl.program_id(1)
    t_len = chunk
    ngh = bblk_ref.shape[1]
    kb = bblk_ref.shape[2]
    half = bblk_ref.shape[3] // 2
    nslab = half // LANES
    per_seq = ngh // 2

    @pl.when(s == 0)
    def _():
        cr_ref[...] = jnp.zeros(cr_ref.shape, F32)
        ci_ref[...] = jnp.zeros(ci_ref.shape, F32)

    for b in range(bsz):
        for gh in range(ngh):
            hsel, k = divmod(gh, per_seq)
            q = hsel * bsz + b
            bu = _dot(u_ref[b, :, gh * kb:(gh + 1) * kb], bblk_ref[0, gh])
            for lb in range(2 * nslab):
                st_ref[k * 2 * nslab + lb, q * S5_PITCH:q * S5_PITCH + t_len, :] = bu[:, lb * LANES:(lb + 1) * LANES]

    for k in range(per_seq):
        lr = lr_ref[0, :, k * half:(k + 1) * half]
        li = li_ref[0, :, k * half:(k + 1) * half]
        base = k * 2 * nslab

        def step(i, carry, base=base, lr=lr, li=li):
            sr, si = carry
            t = jnp.where(d == 0, i, t_len - 1 - i)
            rows = pl.ds(t, SUBLANES, stride=S5_PITCH)
            xr = jnp.concatenate([st_ref[base + lb, rows, :] for lb in range(nslab)], axis=1)
            xi = jnp.concatenate([st_ref[base + nslab + lb, rows, :] for lb in range(nslab)], axis=1)
            nr = lr * sr - li * si + xr
            ni = lr * si + li * sr + xi
            for lb in range(nslab):
                st_ref[base + lb, rows, :] = nr[:, lb * LANES:(lb + 1) * LANES]
                st_ref[base + nslab + lb, rows, :] = ni[:, lb * LANES:(lb + 1) * LANES]
            return nr, ni

        sr0 = cr_ref[:, k * half:(k + 1) * half]
        si0 = ci_ref[:, k * half:(k + 1) * half]
        sr1, si1 = lax.fori_loop(0, t_len, step, (sr0, si0), unroll=2)
        cr_ref[:, k * half:(k + 1) * half] = sr1
        ci_ref[:, k * half:(k + 1) * half] = si1

    for b in range(bsz):
        for gh in range(ngh):
            hsel, k = divmod(gh, per_seq)
            q = hsel * bsz + b
            st = jnp.concatenate(
                [st_ref[k * 2 * nslab + lb, q * S5_PITCH:q * S5_PITCH + t_len, :] for lb in range(2 * nslab)],
                axis=1).astype(BF16)
            y_ref[0, b, :, gh * kb:(gh + 1) * kb] = _dot(st, cblk_ref[0, gh])


def _s5_scan(p0, bblk, cblk, lr, li, *, n_lat, n_ctx, col_blk):
    bsz, r, _ = p0.shape
    t_len = S5_CHUNK
    nl, nc = n_lat // t_len, n_ctx // t_len
    ngh, kb, two_half = bblk.shape[1], bblk.shape[2], bblk.shape[3]
    width = ngh * kb

    def chunk_idx(d, s):
        fwd = jnp.where(s < nc, nl + s, s - nc)
        rev = jnp.where(s < nc, nl + nc - 1 - s, nl - 1 - (s - nc))
        return jnp.where(d == 0, fwd, rev)

    nrows = SUBLANES
    kern = functools.partial(_s5_kernel, bsz=bsz, chunk=t_len)
    return pl.pallas_call(
        kern,
        out_shape=jax.ShapeDtypeStruct((2, bsz, r, width), F32),
        grid=(2, nl + nc),
        in_specs=[
            pl.BlockSpec((bsz, t_len, width), lambda d, s: (0, chunk_idx(d, s), col_blk)),
            pl.BlockSpec((1,) + bblk.shape[1:], lambda d, s: (d, 0, 0, 0)),
            pl.BlockSpec((1,) + cblk.shape[1:], lambda d, s: (d, 0, 0, 0)),
            pl.BlockSpec((1,) + lr.shape[1:], lambda d, s: (d, 0, 0)),
            pl.BlockSpec((1,) + li.shape[1:], lambda d, s: (d, 0, 0)),
        ],
        out_specs=pl.BlockSpec((1, bsz, t_len, width), lambda d, s: (d, 0, chunk_idx(d, s), 0)),
        scratch_shapes=[
            pltpu.VMEM((two_half // LANES * (ngh // 2), nrows * S5_PITCH, LANES), F32),
            pltpu.VMEM((nrows, (ngh // 2) * (two_half // 2)), F32),
            pltpu.VMEM((nrows, (ngh // 2) * (two_half // 2)), F32),
        ],
        compiler_params=_cparams(("arbitrary", "arbitrary")),
        name="s5_scan",
    )(p0, bblk, cblk, lr, li)


def _s5_operators(a_re, a_im, log_dt, b_re, b_im, c_re, c_im, bsz):
    f32 = F32
    ndir, g, p = a_re.shape
    cin = b_re.shape[-1]
    gl = 16
    ngh = g // gl
    ar, ai = a_re.astype(f32), a_im.astype(f32)
    dt = jnp.exp(log_dt.astype(f32))[..., None]
    mag = jnp.exp(ar * dt)
    lr = mag * jnp.cos(ai * dt)
    li = mag * jnp.sin(ai * dt)
    den = ar * ar + ai * ai
    zr = ((lr - 1.0) * ar + li * ai) / den
    zi = (li * ar - (lr - 1.0) * ai) / den
    br, bi = b_re.astype(f32), b_im.astype(f32)
    bbr = zr[..., None] * br - zi[..., None] * bi
    bbi = zr[..., None] * bi + zi[..., None] * br
    eye = jnp.eye(gl, dtype=f32)

    def in_block(m):
        m = m.reshape(ndir, ngh, gl, p, cin)
        return jnp.einsum('dhgpc,gk->dhgckp', m, eye).reshape(ndir, ngh, gl * cin, gl * p)

    bblk = jnp.concatenate([in_block(bbr), in_block(bbi)], axis=-1).astype(BF16)

    def out_block(m):
        m = m.reshape(ndir, ngh, gl, cin, p)
        return jnp.einsum('dhgcp,gk->dhgpkc', m, eye).reshape(ndir, ngh, gl * p, gl * cin)

    cblk = jnp.concatenate([out_block(c_re.astype(f32)), out_block(-c_im.astype(f32))], axis=-2).astype(BF16)

    def rows(v):
        per_seq = ngh // 2
        v = v.reshape(ndir, 2, 1, per_seq * gl * p)
        return jnp.broadcast_to(v, (ndir, 2, bsz, per_seq * gl * p)).reshape(ndir, 2 * bsz, per_seq * gl * p)

    return bblk, cblk, rows(lr), rows(li)


def _s5_glu_kernel(yf_ref, yr_ref, u_ref, sg_ref, d_ref, w_ref, b_ref, o_ref):
    y = u_ref[0].astype(F32) * d_ref[...] + yf_ref[0, 0] + yr_ref[0, 0]
    g = 0.5 * y * (1.0 + jnp.tanh(math.sqrt(2.0 / math.pi) * (y + 0.044715 * (y * y * y))))
    z = _dot(g.astype(BF16), w_ref[...]) + b_ref[...]
    sg = sg_ref[0].astype(F32)
    o_ref[0] = (g * jax.nn.sigmoid(z) * _silu(sg)).astype(o_ref.dtype)


def _s5_glu(ydir, p0, d, glu_w, glu_b, *, u_col_blk, sg_col_blk):
    _, bsz, r, width = ydir.shape
    tm = ROW_TILE
    return pl.pallas_call(
        _s5_glu_kernel,
        out_shape=jax.ShapeDtypeStruct((bsz, r, width), BF16),
        grid=(bsz, r // tm),
        in_specs=[
            pl.BlockSpec((1, 1, tm, width), lambda b, i: (0, b, i, 0)),
            pl.BlockSpec((1, 1, tm, width), lambda b, i: (1, b, i, 0)),
            pl.BlockSpec((1, tm, width), lambda b, i: (b, i, u_col_blk)),
            pl.BlockSpec((1, tm, width), lambda b, i: (b, i, sg_col_blk)),
            pl.BlockSpec((1, width), lambda b, i: (0, 0)),
            pl.BlockSpec((width, width), lambda b, i: (0, 0)),
            pl.BlockSpec((1, width), lambda b, i: (0, 0)),
        ],
        out_specs=pl.BlockSpec((1, tm, width), lambda b, i: (b, i, 0)),
        compiler_params=_cparams(("arbitrary", "arbitrary")),
        name="s5_glu",
    )(ydir, ydir, p0, p0, d.reshape(1, width).astype(F32), glu_w.astype(BF16),
      glu_b.reshape(1, width).astype(F32))


def _outproj0_kernel(hyl_ref, hyc_ref, hg_ref, s5_ref, x_ref, ctx_ref, modv_ref, w_ref, o_ref,
                     *, nl_tiles, bsz, d, hw):
    b = pl.program_id(0)
    i = pl.program_id(1)
    is_lat = i < nl_tiles
    hy = jnp.where(is_lat, hyl_ref[0], hyc_ref[0]).astype(F32)
    a = (hy * _silu(hg_ref[0].astype(F32))).astype(BF16)
    acc = _dot(a, w_ref[0:hw, :]) + _dot(s5_ref[0], w_ref[hw:, :])
    row = jnp.where(is_lat, b, bsz)
    gate = modv_ref[pl.ds(row, 1), :][:, 2 * d:3 * d]
    xin = jnp.where(is_lat, x_ref[0], ctx_ref[0])
    o_ref[0] = xin + gate * acc


def _outproj0(hy_l, hy_c, p0, s5g, x, ctx, modv, w, *, hg_col_blk):
    bsz, n, d = x.shape
    nc = ctx.shape[1]
    tm = ROW_TILE
    nl_t, nc_t = n // tm, nc // tm
    hw = hy_l.shape[2]
    kern = functools.partial(_outproj0_kernel, nl_tiles=nl_t, bsz=bsz, d=d, hw=hw)
    lat = lambda b, i: (b, jnp.minimum(i, nl_t - 1), 0)
    cx = lambda b, i: (b, jnp.maximum(i - nl_t, 0), 0)
    return pl.pallas_call(
        kern,
        out_shape=jax.ShapeDtypeStruct((bsz, n + nc, d), F32),
        grid=(bsz, nl_t + nc_t),
        in_specs=[
            pl.BlockSpec((1, tm, hw), lat),
            pl.BlockSpec((1, tm, hw), cx),
            pl.BlockSpec((1, tm, hw), lambda b, i: (b, i, hg_col_blk)),
            pl.BlockSpec((1, tm, hw), lambda b, i: (b, i, 0)),
            pl.BlockSpec((1, tm, d), lat),
            pl.BlockSpec((1, tm, d), cx),
            pl.BlockSpec(modv.shape, lambda b, i: (0, 0)),
            pl.BlockSpec(w.shape, lambda b, i: (0, 0)),
        ],
        out_specs=pl.BlockSpec((1, tm, d), lambda b, i: (b, i, 0)),
        compiler_params=_cparams(("arbitrary", "arbitrary")),
        name="outproj_even",
    )(hy_l, hy_c, p0, s5g, x, ctx, modv, w)


ATTN_SAFE_LOG2 = 57.0
ATTN_KEY_CHUNK = 256


def _attn_prepare(k_ref, v_ref, vt_ref, kn_ref):
    hd = v_ref.shape[2]
    vt_ref[0:hd, :] = v_ref[0].astype(F32).T.astype(BF16)
    vt_ref[hd:, :] = jnp.ones((vt_ref.shape[0] - hd, vt_ref.shape[1]), BF16)
    kn_ref[0:1, :] = _max_subhead_sqnorm(k_ref[0])


def _max_subhead_sqnorm(x):
    hd = x.shape[1]
    sel = ((lax.broadcasted_iota(jnp.int32, (hd, hd), 0) // DA_HEAD)
           == lax.broadcasted_iota(jnp.int32, (hd, hd), 1)).astype(BF16)
    return jnp.max(_dot(x * x, sel), axis=0, keepdims=True)


def _attn_kernel(q_ref, k_ref, v_ref, g_ref, lvec_ref, subln_ref, o_ref, vt_ref, kn_ref, st_ref, *, lam_init):
    @pl.when(pl.program_id(2) == 0)
    def _():
        _attn_prepare(k_ref, v_ref, vt_ref, kn_ref)

    q = q_ref[0]
    hd = q.shape[1]
    lane = lax.broadcasted_iota(jnp.int32, q.shape, 1)
    zero = jnp.zeros_like(q)
    qs = (jnp.where(lane < DA_HEAD, q, zero), jnp.where(lane >= DA_HEAD, q, zero))
    lv = lvec_ref[...]
    lam = (jnp.exp(jnp.sum(lv[0:1] * lv[1:2], axis=-1, keepdims=True))
           - jnp.exp(jnp.sum(lv[2:3] * lv[3:4], axis=-1, keepdims=True)) + lam_init)

    bound_sq = _max_subhead_sqnorm(q) * kn_ref[0:1, :]
    safe = jnp.max(bound_sq) <= ATTN_SAFE_LOG2 * ATTN_SAFE_LOG2

    nkc = k_ref.shape[1] // ATTN_KEY_CHUNK

    def scores_t(c, m):
        kc = k_ref[0, c * ATTN_KEY_CHUNK:(c + 1) * ATTN_KEY_CHUNK, :]
        return lax.dot_general(kc, qs[m], (((1,), (1,)), ((), ())), preferred_element_type=F32)

    def attend(subtract_max):
        shifts = [None, None]
        if subtract_max:
            for m in range(2):
                for c in range(nkc):
                    cm = jnp.max(scores_t(c, m), axis=0, keepdims=True)
                    shifts[m] = cm if shifts[m] is None else jnp.maximum(shifts[m], cm)
        accs = [None, None]

        def stage(c):
            for m in range(2):
                st_ref[c % 2, m] = scores_t(c, m)

        stage(0)
        for c in range(nkc):
            if c + 1 < nkc:
                stage(c + 1)
            for m in range(2):
                st = st_ref[c % 2, m]
                if subtract_max:
                    st = st - shifts[m]
                p = jnp.exp2(st).astype(BF16)
                part = _dot(vt_ref[:, c * ATTN_KEY_CHUNK:(c + 1) * ATTN_KEY_CHUNK], p)
                accs[m] = part if accs[m] is None else accs[m] + part
        outs = [acc[0:hd] * (1.0 / acc[hd:hd + 1]) for acc in accs]
        o = (outs[0] - lam * outs[1]).T
        ms = jnp.mean(o * o, axis=-1, keepdims=True)
        o = o * lax.rsqrt(ms + EPS) * subln_ref[...] * (1.0 - lam_init)
        g = g_ref[0].astype(F32)
        o_ref[0] = (o * _silu(g)).astype(o_ref.dtype)

    @pl.when(safe)
    def _():
        attend(False)

    @pl.when(jnp.logical_not(safe))
    def _():
        attend(True)


def _attention(p1, lvec, subln, *, n_lat, heads, lam_init):
    bsz, r, _ = p1.shape
    hd = 2 * DA_HEAD
    tq = 512
    kern = functools.partial(_attn_kernel, lam_init=lam_init)
    return pl.pallas_call(
        kern,
        out_shape=jax.ShapeDtypeStruct((bsz, n_lat, heads * hd), BF16),
        grid=(bsz, heads, n_lat // tq),
        in_specs=[
            pl.BlockSpec((1, tq, hd), lambda b, h, i: (b, i, h)),
            pl.BlockSpec((1, r, hd), lambda b, h, i: (b, 0, heads + h)),
            pl.BlockSpec((1, r, hd), lambda b, h, i: (b, 0, 2 * heads + h)),
            pl.BlockSpec((1, tq, hd), lambda b, h, i: (b, i, 3 * heads + h)),
            pl.BlockSpec(lvec.shape, lambda b, h, i: (0, 0)),
            pl.BlockSpec((1, hd), lambda b, h, i: (0, 0)),
        ],
        out_specs=pl.BlockSpec((1, tq, hd), lambda b, h, i: (b, i, h)),
        scratch_shapes=[
            pltpu.VMEM((hd + 2 * SUBLANES, r), BF16),
            pltpu.VMEM((SUBLANES, LANES), F32),
            pltpu.VMEM((2, 2, ATTN_KEY_CHUNK, tq), F32),
        ],
        compiler_params=_cparams(("arbitrary", "arbitrary", "arbitrary")),
        name="diff_attention",
    )(p1, p1, p1, p1, lvec, subln)


def _outproj1_kernel(o_ref, x_ref, modv_ref, w_ref, out_ref, *, d):
    b = pl.program_id(0)
    acc = _dot(o_ref[0], w_ref[...])
    gate = modv_ref[pl.ds(b, 1), :][:, 2 * d:3 * d]
    out_ref[0] = x_ref[0] + gate * acc


def _outproj1(o, xc, modv, w):
    bsz, n, dv = o.shape
    d = xc.shape[2]
    tm = ROW_TILE
    return pl.pallas_call(
        functools.partial(_outproj1_kernel, d=d),
        out_shape=jax.ShapeDtypeStruct((bsz, n, d), F32),
        grid=(bsz, n // tm),
        in_specs=[
            pl.BlockSpec((1, tm, dv), lambda b, i: (b, i, 0)),
            pl.BlockSpec((1, tm, d), lambda b, i: (b, i, 0)),
            pl.BlockSpec(modv.shape, lambda b, i: (0, 0)),
            pl.BlockSpec(w.shape, lambda b, i: (0, 0)),
        ],
        out_specs=pl.BlockSpec((1, tm, d), lambda b, i: (b, i, 0)),
        compiler_params=_cparams(("arbitrary", "arbitrary")),
        name="outproj_odd",
    )(o, xc, modv, w)


def _rope_tables(n_lat, n_ctx):
    quarter = DA_HEAD // 4
    rows = n_lat // GRID_W
    row = jnp.broadcast_to(jnp.arange(rows, dtype=F32)[:, None], (rows, GRID_W)).reshape(n_lat)
    col = jnp.broadcast_to(jnp.arange(GRID_W, dtype=F32)[None, :], (rows, GRID_W)).reshape(n_lat)
    freqs = ROPE_BASE ** (-jnp.arange(quarter, dtype=F32) / quarter)
    ar = row[:, None] * freqs[None, :]
    ac = col[:, None] * freqs[None, :]
    cos = jnp.concatenate([jnp.cos(ar), jnp.cos(ar), jnp.cos(ac), jnp.cos(ac)], axis=-1)
    sin = jnp.concatenate([-jnp.sin(ar), jnp.sin(ar), -jnp.sin(ac), jnp.sin(ac)], axis=-1)
    cos = jnp.concatenate([cos, cos], axis=-1)
    sin = jnp.concatenate([sin, sin], axis=-1)
    cos = jnp.concatenate([cos, jnp.ones((n_ctx, 2 * DA_HEAD), F32)], axis=0)
    sin = jnp.concatenate([sin, jnp.zeros((n_ctx, 2 * DA_HEAD), F32)], axis=0)
    return cos, sin


def kernel(x, c, ctx, c_ctx, mod_w, mod_b, norm_w, ev_in_w, ev_out_w, hy_conv_w, hy_conv_b, hy_w1, hy_b1, hy_w2, hy_b2, hy_w3, hy_b3, hy_freq, hy_skip, s5_a_re, s5_a_im, s5_log_dt, s5_b_re, s5_b_im, s5_c_re, s5_c_im, s5_d, s5_glu_w, s5_glu_b, od_in_w, od_out_w, da_q_norm, da_k_norm, da_lq1, da_lk1, da_lq2, da_lk2, da_subln):
    bsz, n, d = x.shape
    nc = ctx.shape[1]
    assert n % ROW_TILE == 0 and nc % ROW_TILE == 0 and 2 * bsz == SUBLANES
    depth = mod_w.shape[0]
    assert depth == 2

    npad = SUBLANES * ((bsz + 1 + SUBLANES - 1) // SUBLANES)
    cvec = jnp.concatenate([c, c_ctx[None, :], jnp.zeros((npad - bsz - 1, d), F32)], axis=0)
    modv = _mod_vectors(cvec, mod_w, mod_b)

    hw = hy_skip.shape[-1]
    sw = s5_d.shape[-1]
    p0 = _inproj0(x, ctx, modv[0], norm_w[0:1], ev_in_w[0].astype(BF16))
    sc = _shortconv(p0, hy_conv_w[0].astype(F32), hy_conv_b[0].astype(F32), n)
    filt = (hy_w1[0], hy_b1[0], hy_w2[0], hy_b2[0], hy_w3[0], hy_b3[0], hy_freq[0])
    hy_l = _hyena_seq(sc, n, 0, filt, hy_skip[0], hw)
    hy_c = _hyena_seq(sc, nc, n // nc, filt, hy_skip[0], hw)

    bblk, cblk, lr, li = _s5_operators(s5_a_re[0], s5_a_im[0], s5_log_dt[0], s5_b_re[0], s5_b_im[0],
                                       s5_c_re[0], s5_c_im[0], bsz)
    su_blk = ((HY_ORDER + 2) * hw) // sw
    ydir = _s5_scan(p0, bblk, cblk, lr, li, n_lat=n, n_ctx=nc, col_blk=su_blk)
    s5g = _s5_glu(ydir, p0, s5_d[0], s5_glu_w[0], s5_glu_b[0], u_col_blk=su_blk, sg_col_blk=su_blk + 1)
    x1 = _outproj0(hy_l, hy_c, p0, s5g, x, ctx, modv[0], ev_out_w[0].astype(BF16),
                   hg_col_blk=(HY_ORDER + 1))

    heads = d // (2 * DA_HEAD)
    reps = d // DA_HEAD
    qscale = DA_HEAD ** -0.5 * math.log2(math.e)
    qkw = jnp.stack([jnp.tile(da_q_norm[0].astype(F32), reps) * qscale, jnp.tile(da_k_norm[0].astype(F32), reps)])
    qkw = jnp.concatenate([qkw, jnp.zeros((SUBLANES - 2, d), F32)], axis=0)
    gidx = jnp.arange(2 * LANES) // DA_HEAD
    gm = (gidx[:, None] == gidx[None, :]).astype(BF16) * (1.0 / DA_HEAD)
    cos_t, sin_t = _rope_tables(n, nc)
    lidx = jnp.arange(2 * LANES)
    pm = (lidx[:, None] == (lidx[None, :] ^ (DA_HEAD // 4))).astype(BF16)
    p1 = _inproj1(x1, modv[1], norm_w[1:2], od_in_w[0].astype(BF16), qkw, gm.astype(BF16), pm, cos_t, sin_t, n)
    lam_init = 0.8 - 0.6 * math.exp(-0.3 * 1)
    lvec = jnp.stack([da_lq1[0], da_lk1[0], da_lq2[0], da_lk2[0]]).astype(F32)
    lvec = jnp.pad(lvec, ((0, SUBLANES - 4), (0, LANES - lvec.shape[1])))
    o = _attention(p1, lvec, da_subln[0].reshape(1, 2 * DA_HEAD).astype(F32), n_lat=n, heads=heads,
                   lam_init=lam_init)
    return _outproj1(o, x1, modv[1], od_out_w[0].astype(BF16))
```

```python
import functools
import math

import jax
import jax.numpy as jnp
from jax import lax
from jax.experimental import pallas as pl
from jax.experimental.pallas import tpu as pltpu

F32 = jnp.float32
BF16 = jnp.bfloat16
HIGHEST = lax.Precision.HIGHEST

EPS = 1e-6
ROW_TILE = 256
LANES = 128
SUBLANES = 8
VMEM_LIMIT = 56 * 1024 * 1024

HY_ORDER = 2
HY_BANDS = 16
HY_TARGET = 1e-2
HY_FAST_PCT = 0.3
HY_SLOW_PCT = 1.5
S5_GROUP = 16
S5_STATE = 64
DA_HEAD = 64
GRID_W = 64
ROPE_BASE = 10000.0

S5_CHUNK = 128
S5_PITCH = S5_CHUNK + 4


def _cparams(sem):
    return pltpu.CompilerParams(dimension_semantics=sem, vmem_limit_bytes=VMEM_LIMIT)


def _silu(x):
    return x * jax.nn.sigmoid(x)


def _dot(a, b):
    return jnp.dot(a, b, preferred_element_type=F32)


def _mod_kernel(c_ref, w_ref, b_ref, o_ref):
    a = _silu(c_ref[...])
    o_ref[0] = jnp.dot(a, w_ref[0], precision=HIGHEST, preferred_element_type=F32) + b_ref[0]


def _mod_vectors(cvec, mod_w, mod_b):
    depth, d, d3 = mod_w.shape
    tn = 1024
    return pl.pallas_call(
        _mod_kernel,
        out_shape=jax.ShapeDtypeStruct((depth, cvec.shape[0], d3), F32),
        grid=(depth, d3 // tn),
        in_specs=[
            pl.BlockSpec(cvec.shape, lambda l, j: (0, 0)),
            pl.BlockSpec((1, d, tn), lambda l, j: (l, 0, j)),
            pl.BlockSpec((1, 1, tn), lambda l, j: (l, 0, j)),
        ],
        out_specs=pl.BlockSpec((1, cvec.shape[0], tn), lambda l, j: (l, 0, j)),
        compiler_params=_cparams(("arbitrary", "arbitrary")),
        name="mod_vectors",
    )(cvec, mod_w, mod_b.reshape(depth, 1, d3))


def _normed(x, modv_ref, nw_ref, row, d):
    m = modv_ref[pl.ds(row, 1), :]
    shift = m[:, 0:d]
    scale = m[:, d:2 * d]
    ms = jnp.mean(x * x, axis=-1, keepdims=True)
    y = x * lax.rsqrt(ms + EPS) * nw_ref[...]
    return (y * (1.0 + scale) + shift).astype(BF16)


PROJ_COLS = 2048


def _resident_spec(arr):
    zeros = (0,) * arr.ndim
    return pl.BlockSpec(arr.shape, lambda *_: zeros, pipeline_mode=pl.Buffered(1))


def _inproj0_kernel(x_ref, ctx_ref, modv_ref, nw_ref, w_ref, o_ref, *, nl_tiles, bsz, d):
    b = pl.program_id(0)
    i = pl.program_id(1)
    is_lat = i < nl_tiles
    x = jnp.where(is_lat, x_ref[0], ctx_ref[0])
    row = jnp.where(is_lat, b, bsz)
    h = _normed(x, modv_ref, nw_ref, row, d)
    for j in range(w_ref.shape[1] // PROJ_COLS):
        cols = slice(j * PROJ_COLS, (j + 1) * PROJ_COLS)
        o_ref[0, :, cols] = _dot(h, w_ref[:, cols]).astype(o_ref.dtype)


def _inproj0(x, ctx, modv, nw, w):
    bsz, n, d = x.shape
    nc = ctx.shape[1]
    tm = ROW_TILE
    nl_t, nc_t = n // tm, nc // tm
    nout = w.shape[1]
    kern = functools.partial(_inproj0_kernel, nl_tiles=nl_t, bsz=bsz, d=d)
    return pl.pallas_call(
        kern,
        out_shape=jax.ShapeDtypeStruct((bsz, n + nc, nout), BF16),
        grid=(bsz, nl_t + nc_t),
        in_specs=[
            pl.BlockSpec((1, tm, d), lambda b, i: (b, jnp.minimum(i, nl_t - 1), 0)),
            pl.BlockSpec((1, tm, d), lambda b, i: (b, jnp.maximum(i - nl_t, 0), 0)),
            pl.BlockSpec(modv.shape, lambda b, i: (0, 0)),
            pl.BlockSpec((1, d), lambda b, i: (0, 0)),
            _resident_spec(w),
        ],
        out_specs=pl.BlockSpec((1, tm, nout), lambda b, i: (b, i, 0)),
        compiler_params=_cparams(("arbitrary", "arbitrary")),
        name="inproj_even",
    )(x, ctx, modv, nw, w)


def _inproj1_kernel(x_ref, modv_ref, nw_ref, w_ref, qkw_ref, gm_ref, pm_ref, cos_ref, sin_ref, o_ref,
                    *, nl_tiles, bsz, d):
    b = pl.program_id(0)
    i = pl.program_id(1)
    is_lat = i < nl_tiles
    row = jnp.where(is_lat, b, bsz)
    h = _normed(x_ref[0], modv_ref, nw_ref, row, d)

    def plain(j):
        cols = slice(j * PROJ_COLS, (j + 1) * PROJ_COLS)
        o_ref[0, :, cols] = _dot(h, w_ref[:, cols]).astype(o_ref.dtype)

    def normed_rotary(j):
        cols = slice(j * PROJ_COLS, (j + 1) * PROJ_COLS)
        acc = _dot(h, w_ref[:, cols])
        sq = (acc * acc).astype(BF16)
        gm = gm_ref[...]
        width = gm.shape[0]
        tiles = range(acc.shape[1] // width)
        ms = jnp.concatenate([_dot(sq[:, t * width:(t + 1) * width], gm) for t in tiles], axis=1)
        yn = acc * lax.rsqrt(ms + EPS) * qkw_ref[j:j + 1, :]
        reps = acc.shape[1] // cos_ref.shape[1]
        cos = jnp.concatenate([cos_ref[...]] * reps, axis=1)
        sin = jnp.concatenate([sin_ref[...]] * reps, axis=1)
        ynb = yn.astype(BF16)
        pm = pm_ref[...]
        swapped = jnp.concatenate([_dot(ynb[:, t * width:(t + 1) * width], pm) for t in tiles], axis=1)
        o_ref[0, :, cols] = (yn * cos + swapped * sin).astype(o_ref.dtype)

    normed_rotary(1)
    plain(2)

    @pl.when(is_lat)
    def _():
        normed_rotary(0)
        plain(3)

    @pl.when(jnp.logical_not(is_lat))
    def _():
        zeros = jnp.zeros((o_ref.shape[1], PROJ_COLS), o_ref.dtype)
        o_ref[0, :, 0:PROJ_COLS] = zeros
        o_ref[0, :, 3 * PROJ_COLS:4 * PROJ_COLS] = zeros


def _inproj1(xc, modv, nw, w, qkw, gm, pm, cos_t, sin_t, n_lat):
    bsz, r, d = xc.shape
    tm = ROW_TILE
    nl_t = n_lat // tm
    nout = w.shape[1]
    assert nout == 4 * PROJ_COLS
    kern = functools.partial(_inproj1_kernel, nl_tiles=nl_t, bsz=bsz, d=d)
    return pl.pallas_call(
        kern,
        out_shape=jax.ShapeDtypeStruct((bsz, r, nout), BF16),
        grid=(bsz, r // tm),
        in_specs=[
            pl.BlockSpec((1, tm, d), lambda b, i: (b, i, 0)),
            pl.BlockSpec(modv.shape, lambda b, i: (0, 0)),
            pl.BlockSpec((1, d), lambda b, i: (0, 0)),
            _resident_spec(w),
            pl.BlockSpec(qkw.shape, lambda b, i: (0, 0)),
            pl.BlockSpec(gm.shape, lambda b, i: (0, 0)),
            pl.BlockSpec(pm.shape, lambda b, i: (0, 0)),
            pl.BlockSpec((tm, cos_t.shape[1]), lambda b, i: (i, 0)),
            pl.BlockSpec((tm, sin_t.shape[1]), lambda b, i: (i, 0)),
        ],
        out_specs=pl.BlockSpec((1, tm, nout), lambda b, i: (b, i, 0)),
        compiler_params=_cparams(("arbitrary", "arbitrary")),
        name="inproj_odd",
    )(xc, modv, nw, w, qkw, gm, pm, cos_t, sin_t)


def _shortconv_kernel(p_ref, w_ref, b_ref, o_ref, *, n_lat):
    p = p_ref[0].astype(F32)
    r = p.shape[0]
    row = lax.broadcasted_iota(jnp.int32, p.shape, 0)
    prev = pltpu.roll(p, 1, axis=0)
    nxt = pltpu.roll(p, r - 1, axis=0)
    prev = jnp.where((row == 0) | (row == n_lat), 0.0, prev)
    nxt = jnp.where((row == n_lat - 1) | (row == r - 1), 0.0, nxt)
    w = w_ref[...]
    o_ref[0] = (prev * w[0:1] + p * w[1:2] + nxt * w[2:3] + b_ref[...]).astype(o_ref.dtype)


def _shortconv(p0, conv_w, conv_b, n_lat):
    bsz, r, _ = p0.shape
    width = conv_w.shape[1]
    tc = 512
    return pl.pallas_call(
        functools.partial(_shortconv_kernel, n_lat=n_lat),
        out_shape=jax.ShapeDtypeStruct((bsz, r, width), BF16),
        grid=(bsz, width // tc),
        in_specs=[
            pl.BlockSpec((1, r, tc), lambda b, j: (b, 0, j)),
            pl.BlockSpec((3, tc), lambda b, j: (0, j)),
            pl.BlockSpec((1, tc), lambda b, j: (0, j)),
        ],
        out_specs=pl.BlockSpec((1, r, tc), lambda b, j: (b, 0, j)),
        compiler_params=_cparams(("arbitrary", "arbitrary")),
        name="hyena_shortconv",
    )(p0, conv_w, conv_b.reshape(1, width))


def _filter_kernel(feat_ref, w1_ref, b1_ref, w2_ref, b2_ref, w3_ref, b3_ref, freq_ref, delta_ref,
                   h_ref, l1_ref):
    ti = pl.program_id(0)
    feat = feat_ref[...]
    tt = feat.shape[0]
    width = delta_ref.shape[1]
    z1 = jnp.dot(feat, w1_ref[...], precision=HIGHEST, preferred_element_type=F32) + b1_ref[...]
    h1 = jnp.sin(freq_ref[0:1, :] * z1)
    z2 = jnp.dot(h1, w2_ref[...], precision=HIGHEST, preferred_element_type=F32) + b2_ref[...]
    h2 = jnp.sin(freq_ref[1:2, :] * z2)
    decay = jnp.exp(-feat[:, 0:1] * delta_ref[...])
    row = lax.broadcasted_iota(jnp.int32, decay.shape, 0) + ti * tt

    @pl.when(ti == 0)
    def _():
        l1_ref[...] = jnp.zeros(l1_ref.shape, l1_ref.dtype)

    for q in range(w3_ref.shape[1] // width):
        cols = slice(q * width, (q + 1) * width)
        z3 = jnp.dot(h2, w3_ref[:, cols], precision=HIGHEST, preferred_element_type=F32) + b3_ref[:, cols]
        hf = z3 * decay
        if q % 2 == 1:
            hf = jnp.where(row == 0, 0.0, hf)
        h_ref[:, cols] = hf.astype(h_ref.dtype)
        l1_ref[:, cols] += jnp.sum(jnp.abs(hf).reshape(tt // SUBLANES, SUBLANES, width), axis=0)


def _hyena_filter_taps(n, w1, b1, w2, b2, w3, b3, freq, width):
    t = jnp.arange(n, dtype=F32)
    tn = t / n
    bands = jnp.linspace(1e-4, HY_BANDS - 1, HY_BANDS, dtype=F32)
    ang = (2.0 * math.pi / n) * t[:, None] * bands[None, :]
    feat = jnp.concatenate([tn[:, None], jnp.cos(ang), -jnp.sin(ang)], axis=-1)
    emb = feat.shape[1]
    feat = jnp.pad(feat, ((0, 0), (0, LANES - emb)))
    w1p = jnp.pad(w1.astype(F32), ((0, LANES - emb), (0, 0)))
    ffn = w1.shape[1]
    deltas = jnp.abs(jnp.linspace(math.log(HY_TARGET) / HY_SLOW_PCT, math.log(HY_TARGET) / HY_FAST_PCT,
                                  width, dtype=F32)).reshape(1, width)
    ncol = w3.shape[1]
    tt = min(512, n)
    full = lambda i: (0, 0)
    return pl.pallas_call(
        _filter_kernel,
        out_shape=(jax.ShapeDtypeStruct((n, ncol), BF16), jax.ShapeDtypeStruct((SUBLANES, ncol), F32)),
        grid=(n // tt,),
        in_specs=[
            pl.BlockSpec((tt, LANES), lambda i: (i, 0)),
            pl.BlockSpec((LANES, ffn), full),
            pl.BlockSpec((1, ffn), full),
            pl.BlockSpec((ffn, ffn), full),
            pl.BlockSpec((1, ffn), full),
            pl.BlockSpec((ffn, ncol), full),
            pl.BlockSpec((1, ncol), full),
            pl.BlockSpec((2, ffn), full),
            pl.BlockSpec((1, width), full),
        ],
        out_specs=(pl.BlockSpec((tt, ncol), lambda i: (i, 0)),
                   pl.BlockSpec((SUBLANES, ncol), full)),
        compiler_params=_cparams(("arbitrary",)),
        name="hyena_filter_taps",
    )(feat, w1p, b1.reshape(1, ffn).astype(F32), w2.astype(F32), b2.reshape(1, ffn).astype(F32),
      w3.astype(F32), b3.reshape(1, ncol).astype(F32), freq.astype(F32), deltas)


def _dft_tables(n):
    blk = 64
    hi = jnp.arange(n // blk, dtype=jnp.int32)[:, None]
    lo = jnp.arange(blk, dtype=jnp.int32)[:, None]
    other = jnp.arange(n, dtype=jnp.int32)[None, :]
    period = 4 * n
    unit = math.pi / (2 * n)

    def cs(m):
        a = (m % period).astype(F32) * unit
        return jnp.cos(a), jnp.sin(a)

    def combine(ca, sa, cb, sb):
        c = ca[:, None, :] * cb[None, :, :] - sa[:, None, :] * sb[None, :, :]
        s = sa[:, None, :] * cb[None, :, :] + ca[:, None, :] * sb[None, :, :]
        return c.reshape(n, n).astype(BF16), (-s).reshape(n, n).astype(BF16)

    fc, fs = combine(*cs(2 * blk * hi * other), *cs((2 * lo + 1) * other))
    tc, ts = combine(*cs((2 * other + 1) * blk * hi), *cs((2 * other + 1) * lo))
    return fc, fs, tc, ts


def _dft_fwd_kernel(fc_ref, fs_ref, z_ref, o_re_ref, o_im_ref):
    z = z_ref[...].astype(BF16)
    o_re_ref[...] = _dot(fc_ref[...], z)
    o_im_ref[...] = _dot(fs_ref[...], z)


def _filter_spectrum(fc, fs, taps):
    n, ncol = taps.shape
    tf = min(512, n)
    tcn = 512
    return pl.pallas_call(
        _dft_fwd_kernel,
        out_shape=(jax.ShapeDtypeStruct((n, ncol), F32), jax.ShapeDtypeStruct((n, ncol), F32)),
        grid=(ncol // tcn, n // tf),
        in_specs=[
            pl.BlockSpec((tf, n), lambda j, f: (f, 0)),
            pl.BlockSpec((tf, n), lambda j, f: (f, 0)),
            pl.BlockSpec((n, tcn), lambda j, f: (0, j)),
        ],
        out_specs=(pl.BlockSpec((tf, tcn), lambda j, f: (f, j)),
                   pl.BlockSpec((tf, tcn), lambda j, f: (f, j))),
        compiler_params=_cparams(("arbitrary", "arbitrary")),
        name="hyena_filter_spectrum",
    )(fc, fs, taps)


def _dft_fwd_mul_kernel(fc_ref, fs_ref, z_ref, kr_ref, ki_ref, pr_ref, pi_ref):
    z = z_ref[0]
    ur = _dot(fc_ref[...], z)
    ui = _dot(fs_ref[...], z)
    kr = kr_ref[...]
    ki = ki_ref[...]
    pr_ref[0] = (ur * kr - ui * ki).astype(pr_ref.dtype)
    pi_ref[0] = (ur * ki + ui * kr).astype(pi_ref.dtype)


def _dft_fwd_mul(fc, fs, z, kr, ki, *, n, row_blk, col_blk0):
    bsz = z.shape[0]
    width = kr.shape[1]
    tf = min(512, n)
    tcn = 512
    return pl.pallas_call(
        _dft_fwd_mul_kernel,
        out_shape=(jax.ShapeDtypeStruct((bsz, n, width), BF16), jax.ShapeDtypeStruct((bsz, n, width), BF16)),
        grid=(width // tcn, n // tf, bsz),
        in_specs=[
            pl.BlockSpec((tf, n), lambda j, f, b: (f, 0)),
            pl.BlockSpec((tf, n), lambda j, f, b: (f, 0)),
            pl.BlockSpec((1, n, tcn), lambda j, f, b: (b, row_blk, col_blk0 + j)),
            pl.BlockSpec((tf, tcn), lambda j, f, b: (f, j)),
            pl.BlockSpec((tf, tcn), lambda j, f, b: (f, j)),
        ],
        out_specs=(pl.BlockSpec((1, tf, tcn), lambda j, f, b: (b, f, j)),
                   pl.BlockSpec((1, tf, tcn), lambda j, f, b: (b, f, j))),
        compiler_params=_cparams(("arbitrary", "arbitrary", "arbitrary")),
        name="hyena_dft_fwd",
    )(fc, fs, z, kr, ki)


def _dft_inv_gate_kernel(tc_ref, ts_ref, pr_ref, pi_ref, z_ref, xg_ref, skip_ref, o_ref):
    y = _dot(tc_ref[...], pr_ref[0]) + _dot(ts_ref[...], pi_ref[0])
    z = z_ref[0].astype(F32)
    xg = xg_ref[0].astype(F32)
    o_ref[0] = (xg * (y + z * skip_ref[...])).astype(o_ref.dtype)


def _dft_inv_gate(tc, ts, pr, pi, z, xg, skip, *, n, z_row_blk, z_col_blk0, xg_row_blk, xg_col_blk0):
    bsz, _, width = pr.shape
    tt = min(512, n)
    tcn = 512
    zrb = z_row_blk * (n // tt)
    xrb = xg_row_blk * (n // tt)
    return pl.pallas_call(
        _dft_inv_gate_kernel,
        out_shape=jax.ShapeDtypeStruct((bsz, n, width), BF16),
        grid=(width // tcn, bsz, n // tt),
        in_specs=[
            pl.BlockSpec((tt, n), lambda j, b, i: (i, 0)),
            pl.BlockSpec((tt, n), lambda j, b, i: (i, 0)),
            pl.BlockSpec((1, n, tcn), lambda j, b, i: (b, 0, j)),
            pl.BlockSpec((1, n, tcn), lambda j, b, i: (b, 0, j)),
            pl.BlockSpec((1, tt, tcn), lambda j, b, i: (b, zrb + i, z_col_blk0 + j)),
            pl.BlockSpec((1, tt, tcn), lambda j, b, i: (b, xrb + i, xg_col_blk0 + j)),
            pl.BlockSpec((1, tcn), lambda j, b, i: (0, j)),
        ],
        out_specs=pl.BlockSpec((1, tt, tcn), lambda j, b, i: (b, i, j)),
        compiler_params=_cparams(("arbitrary", "arbitrary", "arbitrary")),
        name="hyena_dft_inv",
    )(tc, ts, pr, pi, z, xg, skip)


FFT_N2 = 256
FFT_GROUP = 16


def _fft_tables(n):
    n1 = (2 * n) // FFT_N2
    h1 = n1 // 2
    nf2 = n // n1
    f1 = jnp.arange(n1, dtype=jnp.int32)
    t1 = jnp.arange(h1, dtype=jnp.int32)
    th = ((2 * f1[:, None] + 1) * t1[None, :] % (2 * n1)).astype(F32) * (math.pi / n1)
    base = jnp.stack([jnp.cos(th), -jnp.sin(th)])
    eye = jnp.eye(FFT_GROUP, dtype=F32)
    m1 = jnp.einsum('pft,jk->pfjtk', base, eye).reshape(2 * n1 * FFT_GROUP, h1 * FFT_GROUP)
    f = f1[:, None, None] + n1 * jnp.arange(nf2, dtype=jnp.int32)[None, :, None]
    t2 = jnp.arange(FFT_N2, dtype=jnp.int32)[None, None, :]
    psi = ((2 * f + 1) * t2 % (4 * n)).astype(F32) * (math.pi / (2 * n))
    cs = jnp.concatenate([jnp.cos(psi), jnp.sin(psi)], axis=1)
    return (m1.astype(BF16), m1.T.astype(BF16), cs.astype(BF16), jnp.swapaxes(cs, 1, 2).astype(BF16))


def _fft_stage1(z_ref, m1_ref, a_ref, n1):
    h1 = n1 // 2

    def body(g, carry):
        r0 = pl.multiple_of(g * FFT_GROUP, FFT_GROUP)
        xg = jnp.concatenate(
            [z_ref[pl.ds(pl.multiple_of(FFT_N2 * t1 + r0, FFT_GROUP), FFT_GROUP), :] for t1 in range(h1)], axis=0)
        out = _dot(m1_ref[...], xg).astype(BF16)
        for part in range(2):
            for f1 in range(n1):
                row = (part * n1 + f1) * FFT_GROUP
                a_ref[part, f1, pl.ds(r0, FFT_GROUP), :] = out[row:row + FFT_GROUP]
        return carry

    lax.fori_loop(0, FFT_N2 // FFT_GROUP, body, 0, unroll=4)


def _fft_stage2(a_ref, cs_ref, f1):
    rhs = jnp.concatenate([a_ref[0, f1], a_ref[1, f1]], axis=1)
    r = _dot(cs_ref[f1], rhs)
    nf2 = r.shape[0] // 2
    w = r.shape[1] // 2
    xr = r[0:nf2, 0:w] + r[nf2:, w:]
    xi = r[0:nf2, w:] - r[nf2:, 0:w]
    return xr, xi


def _fft_filter_kernel(hf_ref, hb_ref, l1f_ref, l1b_ref, m1_ref, cs_ref, o_ref, sd_ref, a_ref, *, n1, n):
    hf = hf_ref[...].astype(F32)
    hb = hb_ref[...].astype(F32)
    sd_ref[0] = (hf + hb).astype(BF16)
    sd_ref[1] = (hf - hb).astype(BF16)
    _fft_stage1(sd_ref.at[0], m1_ref, a_ref.at[0], n1)
    _fft_stage1(sd_ref.at[1], m1_ref, a_ref.at[1], n1)
    l1 = jnp.sum(l1f_ref[...], axis=0, keepdims=True) + jnp.sum(l1b_ref[...], axis=0, keepdims=True)
    norm = (1.0 / n) / l1

    def body(f1, carry):
        cs = cs_ref[f1]
        nf2 = cs.shape[0] // 2
        c, s = cs[0:nf2], cs[nf2:]
        kr = _dot(c, a_ref[0, 0, f1]) + _dot(s, a_ref[0, 1, f1])
        ki = _dot(c, a_ref[1, 1, f1]) - _dot(s, a_ref[1, 0, f1])
        rows = pl.ds(pl.multiple_of(f1 * nf2, nf2), nf2)
        o_ref[0, 0, rows, :] = (kr * norm).astype(o_ref.dtype)
        o_ref[0, 1, rows, :] = (ki * norm).astype(o_ref.dtype)
        return carry

    lax.fori_loop(0, n1, body, 0, unroll=min(4, n1))


def _const_spec(arr):
    nd = arr.ndim
    return pl.BlockSpec(arr.shape, lambda *_: (0,) * nd)


def _fft_filter_spectrum(taps, l1, m1, cs, width):
    n, _ = taps.shape
    n1 = cs.shape[0]
    tcn = 256
    cb = width // tcn
    return pl.pallas_call(
        functools.partial(_fft_filter_kernel, n1=n1, n=n),
        out_shape=jax.ShapeDtypeStruct((HY_ORDER, 2, n, width), BF16),
        grid=(HY_ORDER, cb),
        in_specs=[
            pl.BlockSpec((n, tcn), lambda i, j: (0, (2 * i) * cb + j)),
            pl.BlockSpec((n, tcn), lambda i, j: (0, (2 * i + 1) * cb + j)),
            pl.BlockSpec((SUBLANES, tcn), lambda i, j: (0, (2 * i) * cb + j)),
            pl.BlockSpec((SUBLANES, tcn), lambda i, j: (0, (2 * i + 1) * cb + j)),
            _const_spec(m1), _const_spec(cs),
        ],
        out_specs=pl.BlockSpec((1, 2, n, tcn), lambda i, j: (i, 0, 0, j)),
        scratch_shapes=[pltpu.VMEM((2, n, tcn), BF16), pltpu.VMEM((2, 2, n1, FFT_N2, tcn), BF16)],
        compiler_params=_cparams(("arbitrary", "arbitrary")),
        name="hyena_fft_filter_spectrum",
    )(taps, taps, l1, l1, m1, cs)


def _fft_conv_kernel(z_ref, xg_ref, k_ref, skip_ref, m1_ref, m1t_ref, cs_ref, cst_ref, o_ref, a_ref, *, n1):
    h1 = n1 // 2
    _fft_stage1(z_ref.at[0], m1_ref, a_ref, n1)

    def mid(f1, carry):
        xr, xi = _fft_stage2(a_ref, cs_ref, f1)
        nf2 = xr.shape[0]
        rows = pl.ds(pl.multiple_of(f1 * nf2, nf2), nf2)
        kr = k_ref[0, 0, rows, :].astype(F32)
        ki = k_ref[0, 1, rows, :].astype(F32)
        pr = xr * kr - xi * ki
        pi = xr * ki + xi * kr
        rhs = jnp.concatenate([jnp.concatenate([pr, pi], axis=1), jnp.concatenate([-pi, pr], axis=1)],
                              axis=0).astype(BF16)
        d = _dot(cst_ref[f1], rhs).astype(BF16)
        w = d.shape[1] // 2
        a_ref[0, f1] = d[:, 0:w]
        a_ref[1, f1] = d[:, w:]
        return carry

    lax.fori_loop(0, n1, mid, 0, unroll=min(8, n1))

    skip = skip_ref[...]

    def last(g, carry):
        r0 = pl.multiple_of(g * FFT_GROUP, FFT_GROUP)
        dg = jnp.concatenate(
            [a_ref[part, f1, pl.ds(r0, FFT_GROUP), :] for part in range(2) for f1 in range(n1)], axis=0)
        yg = _dot(m1t_ref[...], dg)
        for t1 in range(h1):
            rows = pl.ds(pl.multiple_of(FFT_N2 * t1 + r0, FFT_GROUP), FFT_GROUP)
            z = z_ref[0, rows, :].astype(F32)
            xg = xg_ref[0, rows, :].astype(F32)
            o_ref[0, rows, :] = (xg * (yg[t1 * FFT_GROUP:(t1 + 1) * FFT_GROUP] + z * skip)).astype(o_ref.dtype)
        return carry

    lax.fori_loop(0, FFT_N2 // FFT_GROUP, last, 0, unroll=4)


def _fft_conv_gate(z, xg, kspec, order, skip, tables, *, n, z_row_blk, z_col_blk0, xg_row_blk, xg_col_blk0):
    m1, m1t, cs, cst = tables
    bsz = z.shape[0]
    width = kspec.shape[3]
    n1 = cs.shape[0]
    tcn = 256
    return pl.pallas_call(
        functools.partial(_fft_conv_kernel, n1=n1),
        out_shape=jax.ShapeDtypeStruct((bsz, n, width), BF16),
        grid=(width // tcn, bsz),
        in_specs=[
            pl.BlockSpec((1, n, tcn), lambda j, b: (b, z_row_blk, z_col_blk0 + j)),
            pl.BlockSpec((1, n, tcn), lambda j, b: (b, xg_row_blk, xg_col_blk0 + j)),
            pl.BlockSpec((1, 2, n, tcn), lambda j, b: (order, 0, 0, j)),
            pl.BlockSpec((1, tcn), lambda j, b: (0, j)),
            _const_spec(m1), _const_spec(m1t), _const_spec(cs), _const_spec(cst),
        ],
        out_specs=pl.BlockSpec((1, n, tcn), lambda j, b: (b, 0, j)),
        scratch_shapes=[pltpu.VMEM((2, n1, FFT_N2, tcn), BF16)],
        compiler_params=_cparams(("arbitrary", "arbitrary")),
        name="hyena_fft_conv",
    )(z, xg, kspec, skip, m1, m1t, cs, cst)


def _hyena_seq_fft(sc, n, seq_row_blk, filt, skip, width):
    w1, b1, w2, b2, w3, b3, freq = filt
    taps, l1 = _hyena_filter_taps(n, w1, b1, w2, b2, w3, b3, freq, width)
    tables = _fft_tables(n)
    kspec = _fft_filter_spectrum(taps, l1, tables[0], tables[2], width)
    cb = width // 256
    y = None
    for i in range(HY_ORDER):
        if i == 0:
            z, zrb, zcb = sc, seq_row_blk, 0
        else:
            z, zrb, zcb = y, 0, 0
        y = _fft_conv_gate(z, sc, kspec, i, skip[i].reshape(1, width).astype(F32), tables, n=n,
                           z_row_blk=zrb, z_col_blk0=zcb, xg_row_blk=seq_row_blk, xg_col_blk0=(i + 1) * cb)
    return y


def _hyena_seq(sc, n, seq_row_blk, filt, skip, width):
    if n % (2 * FFT_N2) == 0:
        return _hyena_seq_fft(sc, n, seq_row_blk, filt, skip, width)
    w1, b1, w2, b2, w3, b3, freq = filt
    taps, l1 = _hyena_filter_taps(n, w1, b1, w2, b2, w3, b3, freq, width)
    fc, fs, tc, ts = _dft_tables(n)
    sr, si = _filter_spectrum(fc, fs, taps)
    sr = sr.reshape(n, HY_ORDER, 2, width)
    si = si.reshape(n, HY_ORDER, 2, width)
    l1 = jnp.sum(l1, axis=0).reshape(HY_ORDER, 2, width).sum(axis=1)
    norm = (1.0 / n) / l1
    cb = width // 512
    y = None
    for i in range(HY_ORDER):
        kr = (sr[:, i, 0] + sr[:, i, 1]) * norm[i]
        ki = (si[:, i, 0] - si[:, i, 1]) * norm[i]
        if i == 0:
            z, zrb, zcb = sc, seq_row_blk, 0
        else:
            z, zrb, zcb = y, 0, 0
        pr, pi = _dft_fwd_mul(fc, fs, z, kr, ki, n=n, row_blk=zrb, col_blk0=zcb)
        y = _dft_inv_gate(tc, ts, pr, pi, z, sc, skip[i].reshape(1, width).astype(F32), n=n,
                          z_row_blk=zrb, z_col_blk0=zcb, xg_row_blk=seq_row_blk, xg_col_blk0=(i + 1) * cb)
    return y


def _s5_kernel(u_ref, bblk_ref, cblk_ref, lr_ref, li_ref, y_ref, st_ref, cr_ref, ci_ref, *, bsz, chunk):
    d = pl.program_id(0)
    s = pl.program_id(1)
    t_len = chunk
    ngh = bblk_ref.shape[1]
    kb = bblk_ref.shape[2]
    half = bblk_ref.shape[3] // 2
    nslab = half // LANES
    per_seq = ngh // 2

    @pl.when(s == 0)
    def _():
        cr_ref[...] = jnp.zeros(cr_ref.shape, F32)
        ci_ref[...] = jnp.zeros(ci_ref.shape, F32)

    for b in range(bsz):
        for gh in range(ngh):
            hsel, k = divmod(gh, per_seq)
            q = hsel * bsz + b
            bu = _dot(u_ref[b, :, gh * kb:(gh + 1) * kb], bblk_ref[0, gh])
            for lb in range(2 * nslab):
                st_ref[k * 2 * nslab + lb, q * S5_PITCH:q * S5_PITCH + t_len, :] = bu[:, lb * LANES:(lb + 1) * LANES]

    for k in range(per_seq):
        lr = lr_ref[0, :, k * half:(k + 1) * half]
        li = li_ref[0, :, k * half:(k + 1) * half]
        base = k * 2 * nslab

        def step(i, carry, base=base, lr=lr, li=li):
            sr, si = carry
            t = jnp.where(d == 0, i, t_len - 1 - i)
            rows = pl.ds(t, SUBLANES, stride=S5_PITCH)
            xr = jnp.concatenate([st_ref[base + lb, rows, :] for lb in range(nslab)], axis=1)
            xi = jnp.concatenate([st_ref[base + nslab + lb, rows, :] for lb in range(nslab)], axis=1)
            nr = lr * sr - li * si + xr
            ni = lr * si + li * sr + xi
            for lb in range(nslab):
                st_ref[base + lb, rows, :] = nr[:, lb * LANES:(lb + 1) * LANES]
                st_ref[base + nslab + lb, rows, :] = ni[:, lb * LANES:(lb + 1) * LANES]
            return nr, ni

        sr0 = cr_ref[:, k * half:(k + 1) * half]
        si0 = ci_ref[:, k * half:(k + 1) * half]
        sr1, si1 = lax.fori_loop(0, t_len, step, (sr0, si0), unroll=2)
        cr_ref[:, k * half:(k + 1) * half] = sr1
        ci_ref[:, k * half:(k + 1) * half] = si1

    for b in range(bsz):
        for gh in range(ngh):
            hsel, k = divmod(gh, per_seq)
            q = hsel * bsz + b
            st = jnp.concatenate(
                [st_ref[k * 2 * nslab + lb, q * S5_PITCH:q * S5_PITCH + t_len, :] for lb in range(2 * nslab)],
                axis=1).astype(BF16)
            y_ref[0, b, :, gh * kb:(gh + 1) * kb] = _dot(st, cblk_ref[0, gh])


def _s5_scan(p0, bblk, cblk, lr, li, *, n_lat, n_ctx, col_blk):
    bsz, r, _ = p0.shape
    t_len = S5_CHUNK
    nl, nc = n_lat // t_len, n_ctx // t_len
    ngh, kb, two_half = bblk.shape[1], bblk.shape[2], bblk.shape[3]
    width = ngh * kb

    def chunk_idx(d, s):
        fwd = jnp.where(s < nc, nl + s, s - nc)
        rev = jnp.where(s < nc, nl + nc - 1 - s, nl - 1 - (s - nc))
        return jnp.where(d == 0, fwd, rev)

    nrows = SUBLANES
    kern = functools.partial(_s5_kernel, bsz=bsz, chunk=t_len)
    return pl.pallas_call(
        kern,
        out_shape=jax.ShapeDtypeStruct((2, bsz, r, width), F32),
        grid=(2, nl + nc),
        in_specs=[
            pl.BlockSpec((bsz, t_len, width), lambda d, s: (0, chunk_idx(d, s), col_blk)),
            pl.BlockSpec((1,) + bblk.shape[1:], lambda d, s: (d, 0, 0, 0)),
            pl.BlockSpec((1,) + cblk.shape[1:], lambda d, s: (d, 0, 0, 0)),
            pl.BlockSpec((1,) + lr.shape[1:], lambda d, s: (d, 0, 0)),
            pl.BlockSpec((1,) + li.shape[1:], lambda d, s: (d, 0, 0)),
        ],
        out_specs=pl.BlockSpec((1, bsz, t_len, width), lambda d, s: (d, 0, chunk_idx(d, s), 0)),
        scratch_shapes=[
            pltpu.VMEM((two_half // LANES * (ngh // 2), nrows * S5_PITCH, LANES), F32),
            pltpu.VMEM((nrows, (ngh // 2) * (two_half // 2)), F32),
            pltpu.VMEM((nrows, (ngh // 2) * (two_half // 2)), F32),
        ],
        compiler_params=_cparams(("arbitrary", "arbitrary")),
        name="s5_scan",
    )(p0, bblk, cblk, lr, li)


def _s5_operators(a_re, a_im, log_dt, b_re, b_im, c_re, c_im, bsz):
    f32 = F32
    ndir, g, p = a_re.shape
    cin = b_re.shape[-1]
    gl = 16
    ngh = g // gl
    ar, ai = a_re.astype(f32), a_im.astype(f32)
    dt = jnp.exp(log_dt.astype(f32))[..., None]
    mag = jnp.exp(ar * dt)
    lr = mag * jnp.cos(ai * dt)
    li = mag * jnp.sin(ai * dt)
    den = ar * ar + ai * ai
    zr = ((lr - 1.0) * ar + li * ai) / den
    zi = (li * ar - (lr - 1.0) * ai) / den
    br, bi = b_re.astype(f32), b_im.astype(f32)
    bbr = zr[..., None] * br - zi[..., None] * bi
    bbi = zr[..., None] * bi + zi[..., None] * br
    eye = jnp.eye(gl, dtype=f32)

    def in_block(m):
        m = m.reshape(ndir, ngh, gl, p, cin)
        return jnp.einsum('dhgpc,gk->dhgckp', m, eye).reshape(ndir, ngh, gl * cin, gl * p)

    bblk = jnp.concatenate([in_block(bbr), in_block(bbi)], axis=-1).astype(BF16)

    def out_block(m):
        m = m.reshape(ndir, ngh, gl, cin, p)
        return jnp.einsum('dhgcp,gk->dhgpkc', m, eye).reshape(ndir, ngh, gl * p, gl * cin)

    cblk = jnp.concatenate([out_block(c_re.astype(f32)), out_block(-c_im.astype(f32))], axis=-2).astype(BF16)

    def rows(v):
        per_seq = ngh // 2
        v = v.reshape(ndir, 2, 1, per_seq * gl * p)
        return jnp.broadcast_to(v, (ndir, 2, bsz, per_seq * gl * p)).reshape(ndir, 2 * bsz, per_seq * gl * p)

    return bblk, cblk, rows(lr), rows(li)


def _s5_glu_kernel(yf_ref, yr_ref, u_ref, sg_ref, d_ref, w_ref, b_ref, o_ref):
    y = u_ref[0].astype(F32) * d_ref[...] + yf_ref[0, 0] + yr_ref[0, 0]
    g = 0.5 * y * (1.0 + jnp.tanh(math.sqrt(2.0 / math.pi) * (y + 0.044715 * (y * y * y))))
    z = _dot(g.astype(BF16), w_ref[...]) + b_ref[...]
    sg = sg_ref[0].astype(F32)
    o_ref[0] = (g * jax.nn.sigmoid(z) * _silu(sg)).astype(o_ref.dtype)


def _s5_glu(ydir, p0, d, glu_w, glu_b, *, u_col_blk, sg_col_blk):
    _, bsz, r, width = ydir.shape
    tm = ROW_TILE
    return pl.pallas_call(
        _s5_glu_kernel,
        out_shape=jax.ShapeDtypeStruct((bsz, r, width), BF16),
        grid=(bsz, r // tm),
        in_specs=[
            pl.BlockSpec((1, 1, tm, width), lambda b, i: (0, b, i, 0)),
            pl.BlockSpec((1, 1, tm, width), lambda b, i: (1, b, i, 0)),
            pl.BlockSpec((1, tm, width), lambda b, i: (b, i, u_col_blk)),
            pl.BlockSpec((1, tm, width), lambda b, i: (b, i, sg_col_blk)),
            pl.BlockSpec((1, width), lambda b, i: (0, 0)),
            pl.BlockSpec((width, width), lambda b, i: (0, 0)),
            pl.BlockSpec((1, width), lambda b, i: (0, 0)),
        ],
        out_specs=pl.BlockSpec((1, tm, width), lambda b, i: (b, i, 0)),
        compiler_params=_cparams(("arbitrary", "arbitrary")),
        name="s5_glu",
    )(ydir, ydir, p0, p0, d.reshape(1, width).astype(F32), glu_w.astype(BF16),
      glu_b.reshape(1, width).astype(F32))


def _outproj0_kernel(hyl_ref, hyc_ref, hg_ref, s5_ref, x_ref, ctx_ref, modv_ref, w_ref, o_ref,
                     *, nl_tiles, bsz, d, hw):
    b = pl.program_id(0)
    i = pl.program_id(1)
    is_lat = i < nl_tiles
    hy = jnp.where(is_lat, hyl_ref[0], hyc_ref[0]).astype(F32)
    a = (hy * _silu(hg_ref[0].astype(F32))).astype(BF16)
    acc = _dot(a, w_ref[0:hw, :]) + _dot(s5_ref[0], w_ref[hw:, :])
    row = jnp.where(is_lat, b, bsz)
    gate = modv_ref[pl.ds(row, 1), :][:, 2 * d:3 * d]
    xin = jnp.where(is_lat, x_ref[0], ctx_ref[0])
    o_ref[0] = xin + gate * acc


def _outproj0(hy_l, hy_c, p0, s5g, x, ctx, modv, w, *, hg_col_blk):
    bsz, n, d = x.shape
    nc = ctx.shape[1]
    tm = ROW_TILE
    nl_t, nc_t = n // tm, nc // tm
    hw = hy_l.shape[2]
    kern = functools.partial(_outproj0_kernel, nl_tiles=nl_t, bsz=bsz, d=d, hw=hw)
    lat = lambda b, i: (b, jnp.minimum(i, nl_t - 1), 0)
    cx = lambda b, i: (b, jnp.maximum(i - nl_t, 0), 0)
    return pl.pallas_call(
        kern,
        out_shape=jax.ShapeDtypeStruct((bsz, n + nc, d), F32),
        grid=(bsz, nl_t + nc_t),
        in_specs=[
            pl.BlockSpec((1, tm, hw), lat),
            pl.BlockSpec((1, tm, hw), cx),
            pl.BlockSpec((1, tm, hw), lambda b, i: (b, i, hg_col_blk)),
            pl.BlockSpec((1, tm, hw), lambda b, i: (b, i, 0)),
            pl.BlockSpec((1, tm, d), lat),
            pl.BlockSpec((1, tm, d), cx),
            pl.BlockSpec(modv.shape, lambda b, i: (0, 0)),
            pl.BlockSpec(w.shape, lambda b, i: (0, 0)),
        ],
        out_specs=pl.BlockSpec((1, tm, d), lambda b, i: (b, i, 0)),
        compiler_params=_cparams(("arbitrary", "arbitrary")),
        name="outproj_even",
    )(hy_l, hy_c, p0, s5g, x, ctx, modv, w)


ATTN_SAFE_LOG2 = 57.0
ATTN_KEY_CHUNK = 256


def _attn_prepare(k_ref, v_ref, vt_ref, kn_ref):
    hd = v_ref.shape[2]
    vt_ref[0:hd, :] = v_ref[0].astype(F32).T.astype(BF16)
    vt_ref[hd:, :] = jnp.ones((vt_ref.shape[0] - hd, vt_ref.shape[1]), BF16)
    kn_ref[0:1, :] = _max_subhead_sqnorm(k_ref[0])


def _max_subhead_sqnorm(x):
    hd = x.shape[1]
    sel = ((lax.broadcasted_iota(jnp.int32, (hd, hd), 0) // DA_HEAD)
           == lax.broadcasted_iota(jnp.int32, (hd, hd), 1)).astype(BF16)
    return jnp.max(_dot(x * x, sel), axis=0, keepdims=True)


def _attn_kernel(q_ref, k_ref, v_ref, g_ref, lvec_ref, subln_ref, o_ref, vt_ref, kn_ref, st_ref, *, lam_init):
    @pl.when(pl.program_id(2) == 0)
    def _():
        _attn_prepare(k_ref, v_ref, vt_ref, kn_ref)

    q = q_ref[0]
    hd = q.shape[1]
    lane = lax.broadcasted_iota(jnp.int32, q.shape, 1)
    zero = jnp.zeros_like(q)
    qs = (jnp.where(lane < DA_HEAD, q, zero), jnp.where(lane >= DA_HEAD, q, zero))
    lv = lvec_ref[...]
    lam = (jnp.exp(jnp.sum(lv[0:1] * lv[1:2], axis=-1, keepdims=True))
           - jnp.exp(jnp.sum(lv[2:3] * lv[3:4], axis=-1, keepdims=True)) + lam_init)

    bound_sq = _max_subhead_sqnorm(q) * kn_ref[0:1, :]
    safe = jnp.max(bound_sq) <= ATTN_SAFE_LOG2 * ATTN_SAFE_LOG2

    nkc = k_ref.shape[1] // ATTN_KEY_CHUNK

    def scores_t(c, m):
        kc = k_ref[0, c * ATTN_KEY_CHUNK:(c + 1) * ATTN_KEY_CHUNK, :]
        return lax.dot_general(kc, qs[m], (((1,), (1,)), ((), ())), preferred_element_type=F32)

    def attend(subtract_max):
        shifts = [None, None]
        if subtract_max:
            for m in range(2):
                for c in range(nkc):
                    cm = jnp.max(scores_t(c, m), axis=0, keepdims=True)
                    shifts[m] = cm if shifts[m] is None else jnp.maximum(shifts[m], cm)
        accs = [None, None]

        def stage(c):
            for m in range(2):
                st_ref[c % 2, m] = scores_t(c, m)

        stage(0)
        for c in range(nkc):
            if c + 1 < nkc:
                stage(c + 1)
            for m in range(2):
                st = st_ref[c % 2, m]
                if subtract_max:
                    st = st - shifts[m]
                p = jnp.exp2(st).astype(BF16)
                part = _dot(vt_ref[:, c * ATTN_KEY_CHUNK:(c + 1) * ATTN_KEY_CHUNK], p)
                accs[m] = part if accs[m] is None else accs[m] + part
        outs = [acc[0:hd] * (1.0 / acc[hd:hd + 1]) for acc in accs]
        o = (outs[0] - lam * outs[1]).T
        ms = jnp.mean(o * o, axis=-1, keepdims=True)
        o = o * lax.rsqrt(ms + EPS) * subln_ref[...] * (1.0 - lam_init)
        g = g_ref[0].astype(F32)
        o_ref[0] = (o * _silu(g)).astype(o_ref.dtype)

    @pl.when(safe)
    def _():
        attend(False)

    @pl.when(jnp.logical_not(safe))
    def _():
        attend(True)


def _attention(p1, lvec, subln, *, n_lat, heads, lam_init):
    bsz, r, _ = p1.shape
    hd = 2 * DA_HEAD
    tq = min(1024, n_lat)
    kern = functools.partial(_attn_kernel, lam_init=lam_init)
    return pl.pallas_call(
        kern,
        out_shape=jax.ShapeDtypeStruct((bsz, n_lat, heads * hd), BF16),
        grid=(bsz, heads, n_lat // tq),
        in_specs=[
            pl.BlockSpec((1, tq, hd), lambda b, h, i: (b, i, h)),
            pl.BlockSpec((1, r, hd), lambda b, h, i: (b, 0, heads + h)),
            pl.BlockSpec((1, r, hd), lambda b, h, i: (b, 0, 2 * heads + h)),
            pl.BlockSpec((1, tq, hd), lambda b, h, i: (b, i, 3 * heads + h)),
            pl.BlockSpec(lvec.shape, lambda b, h, i: (0, 0)),
            pl.BlockSpec((1, hd), lambda b, h, i: (0, 0)),
        ],
        out_specs=pl.BlockSpec((1, tq, hd), lambda b, h, i: (b, i, h)),
        scratch_shapes=[
            pltpu.VMEM((hd + 2 * SUBLANES, r), BF16),
            pltpu.VMEM((SUBLANES, LANES), F32),
            pltpu.VMEM((2, 2, ATTN_KEY_CHUNK, tq), F32),
        ],
        compiler_params=_cparams(("arbitrary", "arbitrary", "arbitrary")),
        name="diff_attention",
    )(p1, p1, p1, p1, lvec, subln)


def _outproj1_kernel(o_ref, x_ref, modv_ref, w_ref, out_ref, *, d):
    b = pl.program_id(0)
    acc = _dot(o_ref[0], w_ref[...])
    gate = modv_ref[pl.ds(b, 1), :][:, 2 * d:3 * d]
    out_ref[0] = x_ref[0] + gate * acc


def _outproj1(o, xc, modv, w):
    bsz, n, dv = o.shape
    d = xc.shape[2]
    tm = ROW_TILE
    return pl.pallas_call(
        functools.partial(_outproj1_kernel, d=d),
        out_shape=jax.ShapeDtypeStruct((bsz, n, d), F32),
        grid=(bsz, n // tm),
        in_specs=[
            pl.BlockSpec((1, tm, dv), lambda b, i: (b, i, 0)),
            pl.BlockSpec((1, tm, d), lambda b, i: (b, i, 0)),
            pl.BlockSpec(modv.shape, lambda b, i: (0, 0)),
            pl.BlockSpec(w.shape, lambda b, i: (0, 0)),
        ],
        out_specs=pl.BlockSpec((1, tm, d), lambda b, i: (b, i, 0)),
        compiler_params=_cparams(("arbitrary", "arbitrary")),
        name="outproj_odd",
    )(o, xc, modv, w)


def _rope_tables(n_lat, n_ctx):
    quarter = DA_HEAD // 4
    rows = n_lat // GRID_W
    row = jnp.broadcast_to(jnp.arange(rows, dtype=F32)[:, None], (rows, GRID_W)).reshape(n_lat)
    col = jnp.broadcast_to(jnp.arange(GRID_W, dtype=F32)[None, :], (rows, GRID_W)).reshape(n_lat)
    freqs = ROPE_BASE ** (-jnp.arange(quarter, dtype=F32) / quarter)
    ar = row[:, None] * freqs[None, :]
    ac = col[:, None] * freqs[None, :]
    cos = jnp.concatenate([jnp.cos(ar), jnp.cos(ar), jnp.cos(ac), jnp.cos(ac)], axis=-1)
    sin = jnp.concatenate([-jnp.sin(ar), jnp.sin(ar), -jnp.sin(ac), jnp.sin(ac)], axis=-1)
    cos = jnp.concatenate([cos, cos], axis=-1)
    sin = jnp.concatenate([sin, sin], axis=-1)
    cos = jnp.concatenate([cos, jnp.ones((n_ctx, 2 * DA_HEAD), F32)], axis=0)
    sin = jnp.concatenate([sin, jnp.zeros((n_ctx, 2 * DA_HEAD), F32)], axis=0)
    return cos, sin


def kernel(x, c, ctx, c_ctx, mod_w, mod_b, norm_w, ev_in_w, ev_out_w, hy_conv_w, hy_conv_b, hy_w1, hy_b1, hy_w2, hy_b2, hy_w3, hy_b3, hy_freq, hy_skip, s5_a_re, s5_a_im, s5_log_dt, s5_b_re, s5_b_im, s5_c_re, s5_c_im, s5_d, s5_glu_w, s5_glu_b, od_in_w, od_out_w, da_q_norm, da_k_norm, da_lq1, da_lk1, da_lq2, da_lk2, da_subln):
    bsz, n, d = x.shape
    nc = ctx.shape[1]
    assert n % ROW_TILE == 0 and nc % ROW_TILE == 0 and 2 * bsz == SUBLANES
    depth = mod_w.shape[0]
    assert depth == 2

    npad = SUBLANES * ((bsz + 1 + SUBLANES - 1) // SUBLANES)
    cvec = jnp.concatenate([c, c_ctx[None, :], jnp.zeros((npad - bsz - 1, d), F32)], axis=0)
    modv = _mod_vectors(cvec, mod_w, mod_b)

    hw = hy_skip.shape[-1]
    sw = s5_d.shape[-1]
    p0 = _inproj0(x, ctx, modv[0], norm_w[0:1], ev_in_w[0].astype(BF16))
    sc = _shortconv(p0, hy_conv_w[0].astype(F32), hy_conv_b[0].astype(F32), n)
    filt = (hy_w1[0], hy_b1[0], hy_w2[0], hy_b2[0], hy_w3[0], hy_b3[0], hy_freq[0])
    hy_l = _hyena_seq(sc, n, 0, filt, hy_skip[0], hw)
    hy_c = _hyena_seq(sc, nc, n // nc, filt, hy_skip[0], hw)

    bblk, cblk, lr, li = _s5_operators(s5_a_re[0], s5_a_im[0], s5_log_dt[0], s5_b_re[0], s5_b_im[0],
                                       s5_c_re[0], s5_c_im[0], bsz)
    su_blk = ((HY_ORDER + 2) * hw) // sw
    ydir = _s5_scan(p0, bblk, cblk, lr, li, n_lat=n, n_ctx=nc, col_blk=su_blk)
    s5g = _s5_glu(ydir, p0, s5_d[0], s5_glu_w[0], s5_glu_b[0], u_col_blk=su_blk, sg_col_blk=su_blk + 1)
    x1 = _outproj0(hy_l, hy_c, p0, s5g, x, ctx, modv[0], ev_out_w[0].astype(BF16),
                   hg_col_blk=(HY_ORDER + 1))

    heads = d // (2 * DA_HEAD)
    reps = d // DA_HEAD
    qscale = DA_HEAD ** -0.5 * math.log2(math.e)
    qkw = jnp.stack([jnp.tile(da_q_norm[0].astype(F32), reps) * qscale, jnp.tile(da_k_norm[0].astype(F32), reps)])
    qkw = jnp.concatenate([qkw, jnp.zeros((SUBLANES - 2, d), F32)], axis=0)
    gidx = jnp.arange(2 * LANES) // DA_HEAD
    gm = (gidx[:, None] == gidx[None, :]).astype(BF16) * (1.0 / DA_HEAD)
    cos_t, sin_t = _rope_tables(n, nc)
    lidx = jnp.arange(2 * LANES)
    pm = (lidx[:, None] == (lidx[None, :] ^ (DA_HEAD // 4))).astype(BF16)
    p1 = _inproj1(x1, modv[1], norm_w[1:2], od_in_w[0].astype(BF16), qkw, gm.astype(BF16), pm, cos_t, sin_t, n)
    lam_init = 0.8 - 0.6 * math.exp(-0.3 * 1)
    lvec = jnp.stack([da_lq1[0], da_lk1[0], da_lq2[0], da_lk2[0]]).astype(F32)
    lvec = jnp.pad(lvec, ((0, SUBLANES - 4), (0, LANES - lvec.shape[1])))
    o = _attention(p1, lvec, da_subln[0].reshape(1, 2 * DA_HEAD).astype(F32), n_lat=n, heads=heads,
                   lam_init=lam_init)
    return _outproj1(o, x1, modv[1], od_out_w[0].astype(BF16))
```

```python
import functools
import math

import jax
import jax.numpy as jnp
from jax import lax
from jax.experimental import pallas as pl
from jax.experimental.pallas import tpu as pltpu

F32 = jnp.float32
BF16 = jnp.bfloat16
HIGHEST = lax.Precision.HIGHEST

EPS = 1e-6
ROW_TILE = 256
LANES = 128
SUBLANES = 8
VMEM_LIMIT = 56 * 1024 * 1024

HY_ORDER = 2
HY_BANDS = 16
HY_TARGET = 1e-2
HY_FAST_PCT = 0.3
HY_SLOW_PCT = 1.5
S5_GROUP = 16
S5_STATE = 64
DA_HEAD = 64
GRID_W = 64
ROPE_BASE = 10000.0


def _cparams(sem):
    return pltpu.CompilerParams(dimension_semantics=sem, vmem_limit_bytes=VMEM_LIMIT)


def _silu(x):
    return x * jax.nn.sigmoid(x)


def _dot(a, b):
    return jnp.dot(a, b, preferred_element_type=F32)


def _mod_kernel(c_ref, w_ref, b_ref, o_ref):
    a = _silu(c_ref[...])
    o_ref[0] = jnp.dot(a, w_ref[0], precision=HIGHEST, preferred_element_type=F32) + b_ref[0]


def _mod_vectors(cvec, mod_w, mod_b):
    depth, d, d3 = mod_w.shape
    tn = 1024
    return pl.pallas_call(
        _mod_kernel,
        out_shape=jax.ShapeDtypeStruct((depth, cvec.shape[0], d3), F32),
        grid=(depth, d3 // tn),
        in_specs=[
            pl.BlockSpec(cvec.shape, lambda l, j: (0, 0)),
            pl.BlockSpec((1, d, tn), lambda l, j: (l, 0, j)),
            pl.BlockSpec((1, 1, tn), lambda l, j: (l, 0, j)),
        ],
        out_specs=pl.BlockSpec((1, cvec.shape[0], tn), lambda l, j: (l, 0, j)),
        compiler_params=_cparams(("arbitrary", "arbitrary")),
        name="mod_vectors",
    )(cvec, mod_w, mod_b.reshape(depth, 1, d3))


def _normed(x, modv_ref, nw_ref, row, d):
    m = modv_ref[pl.ds(row, 1), :]
    shift = m[:, 0:d]
    scale = m[:, d:2 * d]
    ms = jnp.mean(x * x, axis=-1, keepdims=True)
    y = x * lax.rsqrt(ms + EPS) * nw_ref[...]
    return (y * (1.0 + scale) + shift).astype(BF16)


PROJ_COLS = 2048


def _resident_spec(arr):
    zeros = (0,) * arr.ndim
    return pl.BlockSpec(arr.shape, lambda *_: zeros, pipeline_mode=pl.Buffered(1))


def _inproj0_kernel(x_ref, ctx_ref, modv_ref, nw_ref, w_ref, o_ref, *, nl_tiles, bsz, d):
    b = pl.program_id(0)
    i = pl.program_id(1)
    is_lat = i < nl_tiles
    x = jnp.where(is_lat, x_ref[0], ctx_ref[0])
    row = jnp.where(is_lat, b, bsz)
    h = _normed(x, modv_ref, nw_ref, row, d)
    for j in range(w_ref.shape[1] // PROJ_COLS):
        cols = slice(j * PROJ_COLS, (j + 1) * PROJ_COLS)
        o_ref[0, :, cols] = _dot(h, w_ref[:, cols]).astype(o_ref.dtype)


def _inproj0(x, ctx, modv, nw, w):
    bsz, n, d = x.shape
    nc = ctx.shape[1]
    tm = ROW_TILE
    nl_t, nc_t = n // tm, nc // tm
    nout = w.shape[1]
    kern = functools.partial(_inproj0_kernel, nl_tiles=nl_t, bsz=bsz, d=d)
    return pl.pallas_call(
        kern,
        out_shape=jax.ShapeDtypeStruct((bsz, n + nc, nout), BF16),
        grid=(bsz, nl_t + nc_t),
        in_specs=[
            pl.BlockSpec((1, tm, d), lambda b, i: (b, jnp.minimum(i, nl_t - 1), 0)),
            pl.BlockSpec((1, tm, d), lambda b, i: (b, jnp.maximum(i - nl_t, 0), 0)),
            pl.BlockSpec(modv.shape, lambda b, i: (0, 0)),
            pl.BlockSpec((1, d), lambda b, i: (0, 0)),
            _resident_spec(w),
        ],
        out_specs=pl.BlockSpec((1, tm, nout), lambda b, i: (b, i, 0)),
        compiler_params=_cparams(("arbitrary", "arbitrary")),
        name="inproj_even",
    )(x, ctx, modv, nw, w)


def _inproj1_kernel(x_ref, modv_ref, nw_ref, w_ref, qkw_ref, gm_ref, pm_ref, cos_ref, sin_ref, o_ref,
                    *, nl_tiles, bsz, d):
    b = pl.program_id(0)
    i = pl.program_id(1)
    is_lat = i < nl_tiles
    row = jnp.where(is_lat, b, bsz)
    h = _normed(x_ref[0], modv_ref, nw_ref, row, d)

    def plain(j):
        cols = slice(j * PROJ_COLS, (j + 1) * PROJ_COLS)
        o_ref[0, :, cols] = _dot(h, w_ref[:, cols]).astype(o_ref.dtype)

    def normed_rotary(j):
        cols = slice(j * PROJ_COLS, (j + 1) * PROJ_COLS)
        acc = _dot(h, w_ref[:, cols])
        sq = (acc * acc).astype(BF16)
        gm = gm_ref[...]
        width = gm.shape[0]
        tiles = range(acc.shape[1] // width)
        ms = jnp.concatenate([_dot(sq[:, t * width:(t + 1) * width], gm) for t in tiles], axis=1)
        yn = acc * lax.rsqrt(ms + EPS) * qkw_ref[j:j + 1, :]
        reps = acc.shape[1] // cos_ref.shape[1]
        cos = jnp.concatenate([cos_ref[...]] * reps, axis=1)
        sin = jnp.concatenate([sin_ref[...]] * reps, axis=1)
        ynb = yn.astype(BF16)
        pm = pm_ref[...]
        swapped = jnp.concatenate([_dot(ynb[:, t * width:(t + 1) * width], pm) for t in tiles], axis=1)
        o_ref[0, :, cols] = (yn * cos + swapped * sin).astype(o_ref.dtype)

    normed_rotary(1)
    plain(2)

    @pl.when(is_lat)
    def _():
        normed_rotary(0)
        plain(3)

    @pl.when(jnp.logical_not(is_lat))
    def _():
        zeros = jnp.zeros((o_ref.shape[1], PROJ_COLS), o_ref.dtype)
        o_ref[0, :, 0:PROJ_COLS] = zeros
        o_ref[0, :, 3 * PROJ_COLS:4 * PROJ_COLS] = zeros


def _inproj1(xc, modv, nw, w, qkw, gm, pm, cos_t, sin_t, n_lat):
    bsz, r, d = xc.shape
    tm = ROW_TILE
    nl_t = n_lat // tm
    nout = w.shape[1]
    assert nout == 4 * PROJ_COLS
    kern = functools.partial(_inproj1_kernel, nl_tiles=nl_t, bsz=bsz, d=d)
    return pl.pallas_call(
        kern,
        out_shape=jax.ShapeDtypeStruct((bsz, r, nout), BF16),
        grid=(bsz, r // tm),
        in_specs=[
            pl.BlockSpec((1, tm, d), lambda b, i: (b, i, 0)),
            pl.BlockSpec(modv.shape, lambda b, i: (0, 0)),
            pl.BlockSpec((1, d), lambda b, i: (0, 0)),
            _resident_spec(w),
            pl.BlockSpec(qkw.shape, lambda b, i: (0, 0)),
            pl.BlockSpec(gm.shape, lambda b, i: (0, 0)),
            pl.BlockSpec(pm.shape, lambda b, i: (0, 0)),
            pl.BlockSpec((tm, cos_t.shape[1]), lambda b, i: (i, 0)),
            pl.BlockSpec((tm, sin_t.shape[1]), lambda b, i: (i, 0)),
        ],
        out_specs=pl.BlockSpec((1, tm, nout), lambda b, i: (b, i, 0)),
        compiler_params=_cparams(("arbitrary", "arbitrary")),
        name="inproj_odd",
    )(xc, modv, nw, w, qkw, gm, pm, cos_t, sin_t)


def _shortconv_kernel(p_ref, w_ref, b_ref, o_ref, *, n_lat):
    p = p_ref[0].astype(F32)
    r = p.shape[0]
    row = lax.broadcasted_iota(jnp.int32, p.shape, 0)
    prev = pltpu.roll(p, 1, axis=0)
    nxt = pltpu.roll(p, r - 1, axis=0)
    prev = jnp.where((row == 0) | (row == n_lat), 0.0, prev)
    nxt = jnp.where((row == n_lat - 1) | (row == r - 1), 0.0, nxt)
    w = w_ref[...]
    o_ref[0] = (prev * w[0:1] + p * w[1:2] + nxt * w[2:3] + b_ref[...]).astype(o_ref.dtype)


def _shortconv(p0, conv_w, conv_b, n_lat):
    bsz, r, _ = p0.shape
    width = conv_w.shape[1]
    tc = 512
    return pl.pallas_call(
        functools.partial(_shortconv_kernel, n_lat=n_lat),
        out_shape=jax.ShapeDtypeStruct((bsz, r, width), BF16),
        grid=(bsz, width // tc),
        in_specs=[
            pl.BlockSpec((1, r, tc), lambda b, j: (b, 0, j)),
            pl.BlockSpec((3, tc), lambda b, j: (0, j)),
            pl.BlockSpec((1, tc), lambda b, j: (0, j)),
        ],
        out_specs=pl.BlockSpec((1, r, tc), lambda b, j: (b, 0, j)),
        compiler_params=_cparams(("arbitrary", "arbitrary")),
        name="hyena_shortconv",
    )(p0, conv_w, conv_b.reshape(1, width))


def _filter_kernel(feat_ref, w1_ref, b1_ref, w2_ref, b2_ref, w3_ref, b3_ref, freq_ref, delta_ref,
                   h_ref, l1_ref):
    ti = pl.program_id(0)
    feat = feat_ref[...]
    tt = feat.shape[0]
    width = delta_ref.shape[1]
    z1 = jnp.dot(feat, w1_ref[...], precision=HIGHEST, preferred_element_type=F32) + b1_ref[...]
    h1 = jnp.sin(freq_ref[0:1, :] * z1)
    z2 = jnp.dot(h1, w2_ref[...], precision=HIGHEST, preferred_element_type=F32) + b2_ref[...]
    h2 = jnp.sin(freq_ref[1:2, :] * z2)
    decay = jnp.exp(-feat[:, 0:1] * delta_ref[...])
    row = lax.broadcasted_iota(jnp.int32, decay.shape, 0) + ti * tt

    @pl.when(ti == 0)
    def _():
        l1_ref[...] = jnp.zeros(l1_ref.shape, l1_ref.dtype)

    for q in range(w3_ref.shape[1] // width):
        cols = slice(q * width, (q + 1) * width)
        z3 = jnp.dot(h2, w3_ref[:, cols], precision=HIGHEST, preferred_element_type=F32) + b3_ref[:, cols]
        hf = z3 * decay
        if q % 2 == 1:
            hf = jnp.where(row == 0, 0.0, hf)
        h_ref[:, cols] = hf.astype(h_ref.dtype)
        l1_ref[:, cols] += jnp.sum(jnp.abs(hf).reshape(tt // SUBLANES, SUBLANES, width), axis=0)


def _hyena_filter_taps(n, w1, b1, w2, b2, w3, b3, freq, width):
    t = jnp.arange(n, dtype=F32)
    tn = t / n
    bands = jnp.linspace(1e-4, HY_BANDS - 1, HY_BANDS, dtype=F32)
    ang = (2.0 * math.pi / n) * t[:, None] * bands[None, :]
    feat = jnp.concatenate([tn[:, None], jnp.cos(ang), -jnp.sin(ang)], axis=-1)
    emb = feat.shape[1]
    feat = jnp.pad(feat, ((0, 0), (0, LANES - emb)))
    w1p = jnp.pad(w1.astype(F32), ((0, LANES - emb), (0, 0)))
    ffn = w1.shape[1]
    deltas = jnp.abs(jnp.linspace(math.log(HY_TARGET) / HY_SLOW_PCT, math.log(HY_TARGET) / HY_FAST_PCT,
                                  width, dtype=F32)).reshape(1, width)
    ncol = w3.shape[1]
    tt = min(512, n)
    full = lambda i: (0, 0)
    return pl.pallas_call(
        _filter_kernel,
        out_shape=(jax.ShapeDtypeStruct((n, ncol), BF16), jax.ShapeDtypeStruct((SUBLANES, ncol), F32)),
        grid=(n // tt,),
        in_specs=[
            pl.BlockSpec((tt, LANES), lambda i: (i, 0)),
            pl.BlockSpec((LANES, ffn), full),
            pl.BlockSpec((1, ffn), full),
            pl.BlockSpec((ffn, ffn), full),
            pl.BlockSpec((1, ffn), full),
            pl.BlockSpec((ffn, ncol), full),
            pl.BlockSpec((1, ncol), full),
            pl.BlockSpec((2, ffn), full),
            pl.BlockSpec((1, width), full),
        ],
        out_specs=(pl.BlockSpec((tt, ncol), lambda i: (i, 0)),
                   pl.BlockSpec((SUBLANES, ncol), full)),
        compiler_params=_cparams(("arbitrary",)),
        name="hyena_filter_taps",
    )(feat, w1p, b1.reshape(1, ffn).astype(F32), w2.astype(F32), b2.reshape(1, ffn).astype(F32),
      w3.astype(F32), b3.reshape(1, ncol).astype(F32), freq.astype(F32), deltas)


def _dft_tables(n):
    blk = 64
    hi = jnp.arange(n // blk, dtype=jnp.int32)[:, None]
    lo = jnp.arange(blk, dtype=jnp.int32)[:, None]
    other = jnp.arange(n, dtype=jnp.int32)[None, :]
    period = 4 * n
    unit = math.pi / (2 * n)

    def cs(m):
        a = (m % period).astype(F32) * unit
        return jnp.cos(a), jnp.sin(a)

    def combine(ca, sa, cb, sb):
        c = ca[:, None, :] * cb[None, :, :] - sa[:, None, :] * sb[None, :, :]
        s = sa[:, None, :] * cb[None, :, :] + ca[:, None, :] * sb[None, :, :]
        return c.reshape(n, n).astype(BF16), (-s).reshape(n, n).astype(BF16)

    fc, fs = combine(*cs(2 * blk * hi * other), *cs((2 * lo + 1) * other))
    tc, ts = combine(*cs((2 * other + 1) * blk * hi), *cs((2 * other + 1) * lo))
    return fc, fs, tc, ts


def _dft_fwd_kernel(fc_ref, fs_ref, z_ref, o_re_ref, o_im_ref):
    z = z_ref[...].astype(BF16)
    o_re_ref[...] = _dot(fc_ref[...], z)
    o_im_ref[...] = _dot(fs_ref[...], z)


def _filter_spectrum(fc, fs, taps):
    n, ncol = taps.shape
    tf = min(512, n)
    tcn = 512
    return pl.pallas_call(
        _dft_fwd_kernel,
        out_shape=(jax.ShapeDtypeStruct((n, ncol), F32), jax.ShapeDtypeStruct((n, ncol), F32)),
        grid=(ncol // tcn, n // tf),
        in_specs=[
            pl.BlockSpec((tf, n), lambda j, f: (f, 0)),
            pl.BlockSpec((tf, n), lambda j, f: (f, 0)),
            pl.BlockSpec((n, tcn), lambda j, f: (0, j)),
        ],
        out_specs=(pl.BlockSpec((tf, tcn), lambda j, f: (f, j)),
                   pl.BlockSpec((tf, tcn), lambda j, f: (f, j))),
        compiler_params=_cparams(("arbitrary", "arbitrary")),
        name="hyena_filter_spectrum",
    )(fc, fs, taps)


def _dft_fwd_mul_kernel(fc_ref, fs_ref, z_ref, kr_ref, ki_ref, pr_ref, pi_ref):
    z = z_ref[0]
    ur = _dot(fc_ref[...], z)
    ui = _dot(fs_ref[...], z)
    kr = kr_ref[...]
    ki = ki_ref[...]
    pr_ref[0] = (ur * kr - ui * ki).astype(pr_ref.dtype)
    pi_ref[0] = (ur * ki + ui * kr).astype(pi_ref.dtype)


def _dft_fwd_mul(fc, fs, z, kr, ki, *, n, row_blk, col_blk0):
    bsz = z.shape[0]
    width = kr.shape[1]
    tf = min(512, n)
    tcn = 512
    return pl.pallas_call(
        _dft_fwd_mul_kernel,
        out_shape=(jax.ShapeDtypeStruct((bsz, n, width), BF16), jax.ShapeDtypeStruct((bsz, n, width), BF16)),
        grid=(width // tcn, n // tf, bsz),
        in_specs=[
            pl.BlockSpec((tf, n), lambda j, f, b: (f, 0)),
            pl.BlockSpec((tf, n), lambda j, f, b: (f, 0)),
            pl.BlockSpec((1, n, tcn), lambda j, f, b: (b, row_blk, col_blk0 + j)),
            pl.BlockSpec((tf, tcn), lambda j, f, b: (f, j)),
            pl.BlockSpec((tf, tcn), lambda j, f, b: (f, j)),
        ],
        out_specs=(pl.BlockSpec((1, tf, tcn), lambda j, f, b: (b, f, j)),
                   pl.BlockSpec((1, tf, tcn), lambda j, f, b: (b, f, j))),
        compiler_params=_cparams(("arbitrary", "arbitrary", "arbitrary")),
        name="hyena_dft_fwd",
    )(fc, fs, z, kr, ki)


def _dft_inv_gate_kernel(tc_ref, ts_ref, pr_ref, pi_ref, z_ref, xg_ref, skip_ref, o_ref):
    y = _dot(tc_ref[...], pr_ref[0]) + _dot(ts_ref[...], pi_ref[0])
    z = z_ref[0].astype(F32)
    xg = xg_ref[0].astype(F32)
    o_ref[0] = (xg * (y + z * skip_ref[...])).astype(o_ref.dtype)


def _dft_inv_gate(tc, ts, pr, pi, z, xg, skip, *, n, z_row_blk, z_col_blk0, xg_row_blk, xg_col_blk0):
    bsz, _, width = pr.shape
    tt = min(512, n)
    tcn = 512
    zrb = z_row_blk * (n // tt)
    xrb = xg_row_blk * (n // tt)
    return pl.pallas_call(
        _dft_inv_gate_kernel,
        out_shape=jax.ShapeDtypeStruct((bsz, n, width), BF16),
        grid=(width // tcn, bsz, n // tt),
        in_specs=[
            pl.BlockSpec((tt, n), lambda j, b, i: (i, 0)),
            pl.BlockSpec((tt, n), lambda j, b, i: (i, 0)),
            pl.BlockSpec((1, n, tcn), lambda j, b, i: (b, 0, j)),
            pl.BlockSpec((1, n, tcn), lambda j, b, i: (b, 0, j)),
            pl.BlockSpec((1, tt, tcn), lambda j, b, i: (b, zrb + i, z_col_blk0 + j)),
            pl.BlockSpec((1, tt, tcn), lambda j, b, i: (b, xrb + i, xg_col_blk0 + j)),
            pl.BlockSpec((1, tcn), lambda j, b, i: (0, j)),
        ],
        out_specs=pl.BlockSpec((1, tt, tcn), lambda j, b, i: (b, i, j)),
        compiler_params=_cparams(("arbitrary", "arbitrary", "arbitrary")),
        name="hyena_dft_inv",
    )(tc, ts, pr, pi, z, xg, skip)


FFT_N2 = 256
FFT_GROUP = 16


def _fft_tables(n):
    n1 = (2 * n) // FFT_N2
    h1 = n1 // 2
    nf2 = n // n1
    f1 = jnp.arange(n1, dtype=jnp.int32)
    t1 = jnp.arange(h1, dtype=jnp.int32)
    th = ((2 * f1[:, None] + 1) * t1[None, :] % (2 * n1)).astype(F32) * (math.pi / n1)
    base = jnp.stack([jnp.cos(th), -jnp.sin(th)])
    eye = jnp.eye(FFT_GROUP, dtype=F32)
    m1 = jnp.einsum('pft,jk->pfjtk', base, eye).reshape(2 * n1 * FFT_GROUP, h1 * FFT_GROUP)
    f = f1[:, None, None] + n1 * jnp.arange(nf2, dtype=jnp.int32)[None, :, None]
    t2 = jnp.arange(FFT_N2, dtype=jnp.int32)[None, None, :]
    psi = ((2 * f + 1) * t2 % (4 * n)).astype(F32) * (math.pi / (2 * n))
    cs = jnp.concatenate([jnp.cos(psi), jnp.sin(psi)], axis=1)
    return (m1.astype(BF16), m1.T.astype(BF16), cs.astype(BF16), jnp.swapaxes(cs, 1, 2).astype(BF16))


def _fft_stage1(z_ref, m1_ref, a_ref, n1):
    h1 = n1 // 2

    def body(g, carry):
        r0 = pl.multiple_of(g * FFT_GROUP, FFT_GROUP)
        xg = jnp.concatenate(
            [z_ref[pl.ds(pl.multiple_of(FFT_N2 * t1 + r0, FFT_GROUP), FFT_GROUP), :] for t1 in range(h1)], axis=0)
        out = _dot(m1_ref[...], xg).astype(BF16)
        for part in range(2):
            for f1 in range(n1):
                row = (part * n1 + f1) * FFT_GROUP
                a_ref[part, f1, pl.ds(r0, FFT_GROUP), :] = out[row:row + FFT_GROUP]
        return carry

    lax.fori_loop(0, FFT_N2 // FFT_GROUP, body, 0, unroll=4)


def _fft_stage2(a_ref, cs_ref, f1):
    rhs = jnp.concatenate([a_ref[0, f1], a_ref[1, f1]], axis=1)
    r = _dot(cs_ref[f1], rhs)
    nf2 = r.shape[0] // 2
    w = r.shape[1] // 2
    xr = r[0:nf2, 0:w] + r[nf2:, w:]
    xi = r[0:nf2, w:] - r[nf2:, 0:w]
    return xr, xi


def _fft_filter_kernel(hf_ref, hb_ref, l1f_ref, l1b_ref, m1_ref, cs_ref, o_ref, sd_ref, a_ref, *, n1, n):
    hf = hf_ref[...].astype(F32)
    hb = hb_ref[...].astype(F32)
    sd_ref[0] = (hf + hb).astype(BF16)
    sd_ref[1] = (hf - hb).astype(BF16)
    _fft_stage1(sd_ref.at[0], m1_ref, a_ref.at[0], n1)
    _fft_stage1(sd_ref.at[1], m1_ref, a_ref.at[1], n1)
    l1 = jnp.sum(l1f_ref[...], axis=0, keepdims=True) + jnp.sum(l1b_ref[...], axis=0, keepdims=True)
    norm = (1.0 / n) / l1

    def body(f1, carry):
        cs = cs_ref[f1]
        nf2 = cs.shape[0] // 2
        c, s = cs[0:nf2], cs[nf2:]
        kr = _dot(c, a_ref[0, 0, f1]) + _dot(s, a_ref[0, 1, f1])
        ki = _dot(c, a_ref[1, 1, f1]) - _dot(s, a_ref[1, 0, f1])
        rows = pl.ds(pl.multiple_of(f1 * nf2, nf2), nf2)
        o_ref[0, 0, rows, :] = (kr * norm).astype(o_ref.dtype)
        o_ref[0, 1, rows, :] = (ki * norm).astype(o_ref.dtype)
        return carry

    lax.fori_loop(0, n1, body, 0, unroll=min(4, n1))


def _const_spec(arr):
    nd = arr.ndim
    return pl.BlockSpec(arr.shape, lambda *_: (0,) * nd)


def _fft_filter_spectrum(taps, l1, m1, cs, width):
    n, _ = taps.shape
    n1 = cs.shape[0]
    tcn = 256
    cb = width // tcn
    return pl.pallas_call(
        functools.partial(_fft_filter_kernel, n1=n1, n=n),
        out_shape=jax.ShapeDtypeStruct((HY_ORDER, 2, n, width), BF16),
        grid=(HY_ORDER, cb),
        in_specs=[
            pl.BlockSpec((n, tcn), lambda i, j: (0, (2 * i) * cb + j)),
            pl.BlockSpec((n, tcn), lambda i, j: (0, (2 * i + 1) * cb + j)),
            pl.BlockSpec((SUBLANES, tcn), lambda i, j: (0, (2 * i) * cb + j)),
            pl.BlockSpec((SUBLANES, tcn), lambda i, j: (0, (2 * i + 1) * cb + j)),
            _const_spec(m1), _const_spec(cs),
        ],
        out_specs=pl.BlockSpec((1, 2, n, tcn), lambda i, j: (i, 0, 0, j)),
        scratch_shapes=[pltpu.VMEM((2, n, tcn), BF16), pltpu.VMEM((2, 2, n1, FFT_N2, tcn), BF16)],
        compiler_params=_cparams(("arbitrary", "arbitrary")),
        name="hyena_fft_filter_spectrum",
    )(taps, taps, l1, l1, m1, cs)


def _fft_conv_kernel(z_ref, xg_ref, k_ref, skip_ref, m1_ref, m1t_ref, cs_ref, cst_ref, o_ref, a_ref, *, n1):
    h1 = n1 // 2
    _fft_stage1(z_ref.at[0], m1_ref, a_ref, n1)

    def mid(f1, carry):
        xr, xi = _fft_stage2(a_ref, cs_ref, f1)
        nf2 = xr.shape[0]
        rows = pl.ds(pl.multiple_of(f1 * nf2, nf2), nf2)
        kr = k_ref[0, 0, rows, :].astype(F32)
        ki = k_ref[0, 1, rows, :].astype(F32)
        pr = xr * kr - xi * ki
        pi = xr * ki + xi * kr
        rhs = jnp.concatenate([jnp.concatenate([pr, pi], axis=1), jnp.concatenate([-pi, pr], axis=1)],
                              axis=0).astype(BF16)
        d = _dot(cst_ref[f1], rhs).astype(BF16)
        w = d.shape[1] // 2
        a_ref[0, f1] = d[:, 0:w]
        a_ref[1, f1] = d[:, w:]
        return carry

    lax.fori_loop(0, n1, mid, 0, unroll=min(8, n1))

    skip = skip_ref[...]

    def last(g, carry):
        r0 = pl.multiple_of(g * FFT_GROUP, FFT_GROUP)
        dg = jnp.concatenate(
            [a_ref[part, f1, pl.ds(r0, FFT_GROUP), :] for part in range(2) for f1 in range(n1)], axis=0)
        yg = _dot(m1t_ref[...], dg)
        for t1 in range(h1):
            rows = pl.ds(pl.multiple_of(FFT_N2 * t1 + r0, FFT_GROUP), FFT_GROUP)
            z = z_ref[0, rows, :].astype(F32)
            xg = xg_ref[0, rows, :].astype(F32)
            o_ref[0, rows, :] = (xg * (yg[t1 * FFT_GROUP:(t1 + 1) * FFT_GROUP] + z * skip)).astype(o_ref.dtype)
        return carry

    lax.fori_loop(0, FFT_N2 // FFT_GROUP, last, 0, unroll=4)


def _fft_conv_gate(z, xg, kspec, order, skip, tables, *, n, z_row_blk, z_col_blk0, xg_row_blk, xg_col_blk0):
    m1, m1t, cs, cst = tables
    bsz = z.shape[0]
    width = kspec.shape[3]
    n1 = cs.shape[0]
    tcn = 256
    return pl.pallas_call(
        functools.partial(_fft_conv_kernel, n1=n1),
        out_shape=jax.ShapeDtypeStruct((bsz, n, width), BF16),
        grid=(width // tcn, bsz),
        in_specs=[
            pl.BlockSpec((1, n, tcn), lambda j, b: (b, z_row_blk, z_col_blk0 + j)),
            pl.BlockSpec((1, n, tcn), lambda j, b: (b, xg_row_blk, xg_col_blk0 + j)),
            pl.BlockSpec((1, 2, n, tcn), lambda j, b: (order, 0, 0, j)),
            pl.BlockSpec((1, tcn), lambda j, b: (0, j)),
            _const_spec(m1), _const_spec(m1t), _const_spec(cs), _const_spec(cst),
        ],
        out_specs=pl.BlockSpec((1, n, tcn), lambda j, b: (b, 0, j)),
        scratch_shapes=[pltpu.VMEM((2, n1, FFT_N2, tcn), BF16)],
        compiler_params=_cparams(("arbitrary", "arbitrary")),
        name="hyena_fft_conv",
    )(z, xg, kspec, skip, m1, m1t, cs, cst)


def _hyena_seq_fft(sc, n, seq_row_blk, filt, skip, width):
    w1, b1, w2, b2, w3, b3, freq = filt
    taps, l1 = _hyena_filter_taps(n, w1, b1, w2, b2, w3, b3, freq, width)
    tables = _fft_tables(n)
    kspec = _fft_filter_spectrum(taps, l1, tables[0], tables[2], width)
    cb = width // 256
    y = None
    for i in range(HY_ORDER):
        if i == 0:
            z, zrb, zcb = sc, seq_row_blk, 0
        else:
            z, zrb, zcb = y, 0, 0
        y = _fft_conv_gate(z, sc, kspec, i, skip[i].reshape(1, width).astype(F32), tables, n=n,
                           z_row_blk=zrb, z_col_blk0=zcb, xg_row_blk=seq_row_blk, xg_col_blk0=(i + 1) * cb)
    return y


def _hyena_seq(sc, n, seq_row_blk, filt, skip, width):
    if n % (2 * FFT_N2) == 0:
        return _hyena_seq_fft(sc, n, seq_row_blk, filt, skip, width)
    w1, b1, w2, b2, w3, b3, freq = filt
    taps, l1 = _hyena_filter_taps(n, w1, b1, w2, b2, w3, b3, freq, width)
    fc, fs, tc, ts = _dft_tables(n)
    sr, si = _filter_spectrum(fc, fs, taps)
    sr = sr.reshape(n, HY_ORDER, 2, width)
    si = si.reshape(n, HY_ORDER, 2, width)
    l1 = jnp.sum(l1, axis=0).reshape(HY_ORDER, 2, width).sum(axis=1)
    norm = (1.0 / n) / l1
    cb = width // 512
    y = None
    for i in range(HY_ORDER):
        kr = (sr[:, i, 0] + sr[:, i, 1]) * norm[i]
        ki = (si[:, i, 0] - si[:, i, 1]) * norm[i]
        if i == 0:
            z, zrb, zcb = sc, seq_row_blk, 0
        else:
            z, zrb, zcb = y, 0, 0
        pr, pi = _dft_fwd_mul(fc, fs, z, kr, ki, n=n, row_blk=zrb, col_blk0=zcb)
        y = _dft_inv_gate(tc, ts, pr, pi, z, sc, skip[i].reshape(1, width).astype(F32), n=n,
                          z_row_blk=zrb, z_col_blk0=zcb, xg_row_blk=seq_row_blk, xg_col_blk0=(i + 1) * cb)
    return y


S5_CHUNK = 16
S5_BLOCK_GROUPS = 8


def _piece_transpose(arrs):
    arrs = list(arrs)
    lane = lax.broadcasted_iota(jnp.int32, arrs[0].shape, arrs[0].ndim - 1)
    piece = lane // S5_GROUP
    axis = arrs[0].ndim - 1
    for s in (4, 2, 1):
        low = (piece & s) == 0
        nxt = list(arrs)
        for r in range(len(arrs)):
            if r & s:
                continue
            lo, hi = arrs[r], arrs[r + s]
            nxt[r] = jnp.where(low, lo, pltpu.roll(hi, s * S5_GROUP, axis=axis))
            nxt[r + s] = jnp.where(low, pltpu.roll(lo, LANES - s * S5_GROUP, axis=axis), hi)
        arrs = nxt
    return arrs


def _s5_kernel(u_ref, t_ref, wb_ref, v_ref, l16_ref, y_ref, uf_ref, x_ref, yi_ref, d_ref, *, nl, nc):
    nch = nl + nc
    ng = S5_BLOCK_GROUPS
    npair = ng // 2
    pitch = nch + 4
    half = LANES

    uf_ref[...] = u_ref[0].astype(F32)
    for h in range(2):
        steps = [uf_ref[pl.ds(h * ng + j, nch, stride=S5_CHUNK), :] for j in range(ng)]
        for gl, xg in enumerate(_piece_transpose(steps)):
            x_ref[gl, :, h * half:(h + 1) * half] = xg.astype(BF16)

    for q in range(ng // 2):
        g0, g1 = 2 * q, 2 * q + 1
        x0, x1 = x_ref[g0], x_ref[g1]
        yi_ref[g0] = _dot(x0, t_ref[0, g0])
        yi_ref[g1] = _dot(x1, t_ref[0, g1])
        dp = _dot(jnp.concatenate([x0, x1], axis=1), wb_ref[0, q])
        for k in range(4):
            d_ref[k, q * pitch:q * pitch + nch, :] = dp[:, k * LANES:(k + 1) * LANES]

    lrf, lif, lrr, lir = l16_ref[0, 0], l16_ref[0, 1], l16_ref[0, 2], l16_ref[0, 3]

    def step(s, carry):
        srf, sif, srr, sir = carry
        rows_f = pl.ds(jnp.where(s < nc, nl + s, s - nc), npair, stride=pitch)
        rows_r = pl.ds(nch - 1 - s, npair, stride=pitch)
        xrf = d_ref[0, rows_f, :]
        xif = d_ref[1, rows_f, :]
        xrr = d_ref[2, rows_r, :]
        xir = d_ref[3, rows_r, :]
        d_ref[0, rows_f, :] = srf
        d_ref[1, rows_f, :] = sif
        d_ref[2, rows_r, :] = srr
        d_ref[3, rows_r, :] = sir
        return (lrf * srf - lif * sif + xrf, lrf * sif + lif * srf + xif,
                lrr * srr - lir * sir + xrr, lrr * sir + lir * srr + xir)

    zero = jnp.zeros((npair, LANES), F32)
    lax.fori_loop(0, nch, step, (zero, zero, zero, zero), unroll=2)

    for q in range(ng // 2):
        rows = slice(q * pitch, q * pitch + nch)
        sf = jnp.concatenate([d_ref[0, rows, :], d_ref[1, rows, :]], axis=1).astype(BF16)
        sr = jnp.concatenate([d_ref[2, rows, :], d_ref[3, rows, :]], axis=1).astype(BF16)
        yq = _dot(sf, v_ref[0, 0, q]) + _dot(sr, v_ref[0, 1, q])
        w = yq.shape[1] // 2
        yi_ref[2 * q] += yq[:, 0:w]
        yi_ref[2 * q + 1] += yq[:, w:]
    for h in range(2):
        groups = [yi_ref[gl, :, h * half:(h + 1) * half] for gl in range(ng)]
        for j, yj in enumerate(_piece_transpose(groups)):
            uf_ref[pl.ds(h * ng + j, nch, stride=S5_CHUNK), :] = yj
    y_ref[0] = uf_ref[...].astype(y_ref.dtype)


def _s5_ssm(p0, tsum, wbp, vp, l16, *, n_lat, n_ctx, col_blk0):
    bsz, r, _ = p0.shape
    nblk = tsum.shape[0]
    nl, nc = n_lat // S5_CHUNK, n_ctx // S5_CHUNK
    nch = nl + nc
    ng = S5_BLOCK_GROUPS
    gw = S5_CHUNK * S5_GROUP
    return pl.pallas_call(
        functools.partial(_s5_kernel, nl=nl, nc=nc),
        out_shape=jax.ShapeDtypeStruct((bsz, r, nblk * LANES), BF16),
        grid=(nblk, bsz),
        in_specs=[
            pl.BlockSpec((1, r, LANES), lambda k, b: (b, 0, col_blk0 + k)),
            pl.BlockSpec((1,) + tsum.shape[1:], lambda k, b: (k, 0, 0, 0)),
            pl.BlockSpec((1,) + wbp.shape[1:], lambda k, b: (k, 0, 0, 0)),
            pl.BlockSpec((1,) + vp.shape[1:], lambda k, b: (k, 0, 0, 0, 0)),
            pl.BlockSpec((1,) + l16.shape[1:], lambda k, b: (k, 0, 0, 0)),
        ],
        out_specs=pl.BlockSpec((1, r, LANES), lambda k, b: (b, 0, k)),
        scratch_shapes=[
            pltpu.VMEM((r, LANES), F32),
            pltpu.VMEM((ng, nch, gw), BF16),
            pltpu.VMEM((ng, nch, gw), F32),
            pltpu.VMEM((4, (ng // 2) * (nch + 4), LANES), F32),
        ],
        compiler_params=_cparams(("arbitrary", "arbitrary")),
        name="s5_ssm",
    )(p0, tsum, wbp, vp, l16)


def _s5_operators(a_re, a_im, log_dt, b_re, b_im, c_re, c_im):
    f32 = F32
    ndir, g, p = a_re.shape
    cin = b_re.shape[-1]
    L = S5_CHUNK
    ar, ai = a_re.astype(f32), a_im.astype(f32)
    dt = jnp.exp(log_dt.astype(f32))[..., None]
    ls = jnp.arange(L + 1, dtype=f32)[:, None, None, None]
    mag = jnp.exp(ls * (ar * dt)[None])
    pr = mag * jnp.cos(ls * (ai * dt)[None])
    pi = mag * jnp.sin(ls * (ai * dt)[None])
    lr, li = pr[1], pi[1]
    den = ar * ar + ai * ai
    zr = ((lr - 1.0) * ar + li * ai) / den
    zi = (li * ar - (lr - 1.0) * ai) / den
    br, bi = b_re.astype(f32), b_im.astype(f32)
    bbr = zr[..., None] * br - zi[..., None] * bi
    bbi = zr[..., None] * bi + zi[..., None] * br
    cr, ci = c_re.astype(f32), c_im.astype(f32)

    lbr = pr[..., None] * bbr[None] - pi[..., None] * bbi[None]
    lbi = pr[..., None] * bbi[None] + pi[..., None] * bbr[None]
    kimp = jnp.einsum('dgkp,ldgpc->ldgkc', cr, lbr[:L]) - jnp.einsum('dgkp,ldgpc->ldgkc', ci, lbi[:L])
    idx = jnp.arange(L)
    lag_f = (idx[None, :] - idx[:, None])
    sel_f = (lag_f[None] == idx[:, None, None]).astype(f32)
    sel_r = jnp.swapaxes(sel_f, 1, 2)
    t_f = jnp.einsum('lij,lgkc->gicjk', sel_f, kimp[:, 0])
    t_r = jnp.einsum('lij,lgkc->gicjk', sel_r, kimp[:, 1])
    tsum = (t_f + t_r).reshape(g, L * cin, L * cin)

    wf_r = jnp.transpose(lbr[:L, 0][::-1], (1, 0, 3, 2)).reshape(g, L * cin, p)
    wf_i = jnp.transpose(lbi[:L, 0][::-1], (1, 0, 3, 2)).reshape(g, L * cin, p)
    wr_r = jnp.transpose(lbr[:L, 1], (1, 0, 3, 2)).reshape(g, L * cin, p)
    wr_i = jnp.transpose(lbi[:L, 1], (1, 0, 3, 2)).reshape(g, L * cin, p)

    def c_lam(d, powers_r, powers_i):
        re = cr[d][:, None] * powers_r[:, :, None, :] - ci[d][:, None] * powers_i[:, :, None, :]
        im = cr[d][:, None] * powers_i[:, :, None, :] + ci[d][:, None] * powers_r[:, :, None, :]
        return jnp.transpose(re, (0, 3, 1, 2)), jnp.transpose(-im, (0, 3, 1, 2))

    pf_r = jnp.transpose(pr[1:L + 1, 0], (1, 0, 2))
    pf_i = jnp.transpose(pi[1:L + 1, 0], (1, 0, 2))
    pr_r = jnp.transpose(pr[1:L + 1, 1][::-1], (1, 0, 2))
    pr_i = jnp.transpose(pi[1:L + 1, 1][::-1], (1, 0, 2))
    vf_re, vf_im = c_lam(0, pf_r, pf_i)
    vr_re, vr_im = c_lam(1, pr_r, pr_i)

    nblk = g // S5_BLOCK_GROUPS
    npair = S5_BLOCK_GROUPS // 2
    gw = L * cin

    def pair_cols(m):
        m = m.reshape(nblk, npair, 2, m.shape[1], m.shape[2])
        z = jnp.zeros_like(m[:, :, 0])
        top = jnp.concatenate([m[:, :, 0], z], axis=-1)
        bot = jnp.concatenate([z, m[:, :, 1]], axis=-1)
        return jnp.concatenate([top, bot], axis=-2)

    wbp = jnp.concatenate([pair_cols(wf_r), pair_cols(wf_i), pair_cols(wr_r), pair_cols(wr_i)], axis=-1)

    def v_pair(re, im):
        re = pair_cols(re.reshape(g, p, gw))
        im = pair_cols(im.reshape(g, p, gw))
        return jnp.concatenate([re, im], axis=-2)

    vp = jnp.stack([v_pair(vf_re, vf_im), v_pair(vr_re, vr_im)], axis=1)

    def rows(v):
        return v.reshape(nblk, npair, 2 * p)

    l16 = jnp.stack([rows(pr[L, 0]), rows(pi[L, 0]), rows(pr[L, 1]), rows(pi[L, 1])], axis=1)
    tsum = tsum.reshape(nblk, S5_BLOCK_GROUPS, gw, gw)
    return tsum.astype(BF16), wbp.astype(BF16), vp.astype(BF16), l16


def _s5_glu_kernel(y_ref, u_ref, sg_ref, d_ref, w_ref, b_ref, o_ref):
    y = u_ref[0].astype(F32) * d_ref[...] + y_ref[0].astype(F32)
    g = 0.5 * y * (1.0 + jnp.tanh(math.sqrt(2.0 / math.pi) * (y + 0.044715 * (y * y * y))))
    z = _dot(g.astype(BF16), w_ref[...]) + b_ref[...]
    sg = sg_ref[0].astype(F32)
    o_ref[0] = (g * jax.nn.sigmoid(z) * _silu(sg)).astype(o_ref.dtype)


def _s5_glu(yssm, p0, d, glu_w, glu_b, *, u_col_blk, sg_col_blk):
    bsz, r, width = yssm.shape
    tm = ROW_TILE
    return pl.pallas_call(
        _s5_glu_kernel,
        out_shape=jax.ShapeDtypeStruct((bsz, r, width), BF16),
        grid=(bsz, r // tm),
        in_specs=[
            pl.BlockSpec((1, tm, width), lambda b, i: (b, i, 0)),
            pl.BlockSpec((1, tm, width), lambda b, i: (b, i, u_col_blk)),
            pl.BlockSpec((1, tm, width), lambda b, i: (b, i, sg_col_blk)),
            pl.BlockSpec((1, width), lambda b, i: (0, 0)),
            pl.BlockSpec((width, width), lambda b, i: (0, 0)),
            pl.BlockSpec((1, width), lambda b, i: (0, 0)),
        ],
        out_specs=pl.BlockSpec((1, tm, width), lambda b, i: (b, i, 0)),
        compiler_params=_cparams(("arbitrary", "arbitrary")),
        name="s5_glu",
    )(yssm, p0, p0, d.reshape(1, width).astype(F32), glu_w.astype(BF16),
      glu_b.reshape(1, width).astype(F32))


def _outproj0_kernel(hyl_ref, hyc_ref, hg_ref, s5_ref, x_ref, ctx_ref, modv_ref, w_ref, o_ref,
                     *, nl_tiles, bsz, d, hw):
    b = pl.program_id(0)
    i = pl.program_id(1)
    is_lat = i < nl_tiles
    hy = jnp.where(is_lat, hyl_ref[0], hyc_ref[0]).astype(F32)
    a = (hy * _silu(hg_ref[0].astype(F32))).astype(BF16)
    acc = _dot(a, w_ref[0:hw, :]) + _dot(s5_ref[0], w_ref[hw:, :])
    row = jnp.where(is_lat, b, bsz)
    gate = modv_ref[pl.ds(row, 1), :][:, 2 * d:3 * d]
    xin = jnp.where(is_lat, x_ref[0], ctx_ref[0])
    o_ref[0] = xin + gate * acc


def _outproj0(hy_l, hy_c, p0, s5g, x, ctx, modv, w, *, hg_col_blk):
    bsz, n, d = x.shape
    nc = ctx.shape[1]
    tm = ROW_TILE
    nl_t, nc_t = n // tm, nc // tm
    hw = hy_l.shape[2]
    kern = functools.partial(_outproj0_kernel, nl_tiles=nl_t, bsz=bsz, d=d, hw=hw)
    lat = lambda b, i: (b, jnp.minimum(i, nl_t - 1), 0)
    cx = lambda b, i: (b, jnp.maximum(i - nl_t, 0), 0)
    return pl.pallas_call(
        kern,
        out_shape=jax.ShapeDtypeStruct((bsz, n + nc, d), F32),
        grid=(bsz, nl_t + nc_t),
        in_specs=[
            pl.BlockSpec((1, tm, hw), lat),
            pl.BlockSpec((1, tm, hw), cx),
            pl.BlockSpec((1, tm, hw), lambda b, i: (b, i, hg_col_blk)),
            pl.BlockSpec((1, tm, hw), lambda b, i: (b, i, 0)),
            pl.BlockSpec((1, tm, d), lat),
            pl.BlockSpec((1, tm, d), cx),
            pl.BlockSpec(modv.shape, lambda b, i: (0, 0)),
            pl.BlockSpec(w.shape, lambda b, i: (0, 0)),
        ],
        out_specs=pl.BlockSpec((1, tm, d), lambda b, i: (b, i, 0)),
        compiler_params=_cparams(("arbitrary", "arbitrary")),
        name="outproj_even",
    )(hy_l, hy_c, p0, s5g, x, ctx, modv, w)


ATTN_SAFE_LOG2 = 57.0
ATTN_KEY_CHUNK = 256


def _attn_prepare(k_ref, v_ref, vt_ref, kn_ref):
    hd = v_ref.shape[2]
    vt_ref[0:hd, :] = v_ref[0].astype(F32).T.astype(BF16)
    vt_ref[hd:, :] = jnp.ones((vt_ref.shape[0] - hd, vt_ref.shape[1]), BF16)
    kn_ref[0:1, :] = _max_subhead_sqnorm(k_ref[0])


def _max_subhead_sqnorm(x):
    hd = x.shape[1]
    sel = ((lax.broadcasted_iota(jnp.int32, (hd, hd), 0) // DA_HEAD)
           == lax.broadcasted_iota(jnp.int32, (hd, hd), 1)).astype(BF16)
    return jnp.max(_dot(x * x, sel), axis=0, keepdims=True)


def _attn_kernel(q_ref, k_ref, v_ref, g_ref, lvec_ref, subln_ref, o_ref, vt_ref, kn_ref, st_ref, *, lam_init):
    @pl.when(pl.program_id(2) == 0)
    def _():
        _attn_prepare(k_ref, v_ref, vt_ref, kn_ref)

    q = q_ref[0]
    hd = q.shape[1]
    lane = lax.broadcasted_iota(jnp.int32, q.shape, 1)
    zero = jnp.zeros_like(q)
    qs = (jnp.where(lane < DA_HEAD, q, zero), jnp.where(lane >= DA_HEAD, q, zero))
    lv = lvec_ref[...]
    lam = (jnp.exp(jnp.sum(lv[0:1] * lv[1:2], axis=-1, keepdims=True))
           - jnp.exp(jnp.sum(lv[2:3] * lv[3:4], axis=-1, keepdims=True)) + lam_init)

    bound_sq = _max_subhead_sqnorm(q) * kn_ref[0:1, :]
    safe = jnp.max(bound_sq) <= ATTN_SAFE_LOG2 * ATTN_SAFE_LOG2

    nkc = k_ref.shape[1] // ATTN_KEY_CHUNK

    def scores_t(c, m):
        kc = k_ref[0, c * ATTN_KEY_CHUNK:(c + 1) * ATTN_KEY_CHUNK, :]
        return lax.dot_general(kc, qs[m], (((1,), (1,)), ((), ())), preferred_element_type=F32)

    def attend(subtract_max):
        shifts = [None, None]
        if subtract_max:
            for m in range(2):
                for c in range(nkc):
                    cm = jnp.max(scores_t(c, m), axis=0, keepdims=True)
                    shifts[m] = cm if shifts[m] is None else jnp.maximum(shifts[m], cm)
        accs = [None, None]

        def stage(c):
            for m in range(2):
                st_ref[c % 2, m] = scores_t(c, m)

        stage(0)
        for c in range(nkc):
            if c + 1 < nkc:
                stage(c + 1)
            for m in range(2):
                st = st_ref[c % 2, m]
                if subtract_max:
                    st = st - shifts[m]
                p = jnp.exp2(st).astype(BF16)
                part = _dot(vt_ref[:, c * ATTN_KEY_CHUNK:(c + 1) * ATTN_KEY_CHUNK], p)
                accs[m] = part if accs[m] is None else accs[m] + part
        outs = [acc[0:hd] * (1.0 / acc[hd:hd + 1]) for acc in accs]
        o = (outs[0] - lam * outs[1]).T
        ms = jnp.mean(o * o, axis=-1, keepdims=True)
        o = o * lax.rsqrt(ms + EPS) * subln_ref[...] * (1.0 - lam_init)
        g = g_ref[0].astype(F32)
        o_ref[0] = (o * _silu(g)).astype(o_ref.dtype)

    @pl.when(safe)
    def _():
        attend(False)

    @pl.when(jnp.logical_not(safe))
    def _():
        attend(True)


def _attention(p1, lvec, subln, *, n_lat, heads, lam_init):
    bsz, r, _ = p1.shape
    hd = 2 * DA_HEAD
    tq = min(1024, n_lat)
    kern = functools.partial(_attn_kernel, lam_init=lam_init)
    return pl.pallas_call(
        kern,
        out_shape=jax.ShapeDtypeStruct((bsz, n_lat, heads * hd), BF16),
        grid=(bsz, heads, n_lat // tq),
        in_specs=[
            pl.BlockSpec((1, tq, hd), lambda b, h, i: (b, i, h)),
            pl.BlockSpec((1, r, hd), lambda b, h, i: (b, 0, heads + h)),
            pl.BlockSpec((1, r, hd), lambda b, h, i: (b, 0, 2 * heads + h)),
            pl.BlockSpec((1, tq, hd), lambda b, h, i: (b, i, 3 * heads + h)),
            pl.BlockSpec(lvec.shape, lambda b, h, i: (0, 0)),
            pl.BlockSpec((1, hd), lambda b, h, i: (0, 0)),
        ],
        out_specs=pl.BlockSpec((1, tq, hd), lambda b, h, i: (b, i, h)),
        scratch_shapes=[
            pltpu.VMEM((hd + 2 * SUBLANES, r), BF16),
            pltpu.VMEM((SUBLANES, LANES), F32),
            pltpu.VMEM((2, 2, ATTN_KEY_CHUNK, tq), F32),
        ],
        compiler_params=_cparams(("arbitrary", "arbitrary", "arbitrary")),
        name="diff_attention",
    )(p1, p1, p1, p1, lvec, subln)


def _outproj1_kernel(o_ref, x_ref, modv_ref, w_ref, out_ref, *, d):
    b = pl.program_id(0)
    acc = _dot(o_ref[0], w_ref[...])
    gate = modv_ref[pl.ds(b, 1), :][:, 2 * d:3 * d]
    out_ref[0] = x_ref[0] + gate * acc


def _outproj1(o, xc, modv, w):
    bsz, n, dv = o.shape
    d = xc.shape[2]
    tm = ROW_TILE
    return pl.pallas_call(
        functools.partial(_outproj1_kernel, d=d),
        out_shape=jax.ShapeDtypeStruct((bsz, n, d), F32),
        grid=(bsz, n // tm),
        in_specs=[
            pl.BlockSpec((1, tm, dv), lambda b, i: (b, i, 0)),
            pl.BlockSpec((1, tm, d), lambda b, i: (b, i, 0)),
            pl.BlockSpec(modv.shape, lambda b, i: (0, 0)),
            pl.BlockSpec(w.shape, lambda b, i: (0, 0)),
        ],
        out_specs=pl.BlockSpec((1, tm, d), lambda b, i: (b, i, 0)),
        compiler_params=_cparams(("arbitrary", "arbitrary")),
        name="outproj_odd",
    )(o, xc, modv, w)


def _rope_tables(n_lat, n_ctx):
    quarter = DA_HEAD // 4
    rows = n_lat // GRID_W
    row = jnp.broadcast_to(jnp.arange(rows, dtype=F32)[:, None], (rows, GRID_W)).reshape(n_lat)
    col = jnp.broadcast_to(jnp.arange(GRID_W, dtype=F32)[None, :], (rows, GRID_W)).reshape(n_lat)
    freqs = ROPE_BASE ** (-jnp.arange(quarter, dtype=F32) / quarter)
    ar = row[:, None] * freqs[None, :]
    ac = col[:, None] * freqs[None, :]
    cos = jnp.concatenate([jnp.cos(ar), jnp.cos(ar), jnp.cos(ac), jnp.cos(ac)], axis=-1)
    sin = jnp.concatenate([-jnp.sin(ar), jnp.sin(ar), -jnp.sin(ac), jnp.sin(ac)], axis=-1)
    cos = jnp.concatenate([cos, cos], axis=-1)
    sin = jnp.concatenate([sin, sin], axis=-1)
    cos = jnp.concatenate([cos, jnp.ones((n_ctx, 2 * DA_HEAD), F32)], axis=0)
    sin = jnp.concatenate([sin, jnp.zeros((n_ctx, 2 * DA_HEAD), F32)], axis=0)
    return cos, sin


def kernel(x, c, ctx, c_ctx, mod_w, mod_b, norm_w, ev_in_w, ev_out_w, hy_conv_w, hy_conv_b, hy_w1, hy_b1, hy_w2, hy_b2, hy_w3, hy_b3, hy_freq, hy_skip, s5_a_re, s5_a_im, s5_log_dt, s5_b_re, s5_b_im, s5_c_re, s5_c_im, s5_d, s5_glu_w, s5_glu_b, od_in_w, od_out_w, da_q_norm, da_k_norm, da_lq1, da_lk1, da_lq2, da_lk2, da_subln):
    bsz, n, d = x.shape
    nc = ctx.shape[1]
    assert n % ROW_TILE == 0 and nc % ROW_TILE == 0
    depth = mod_w.shape[0]
    assert depth == 2

    npad = SUBLANES * ((bsz + 1 + SUBLANES - 1) // SUBLANES)
    cvec = jnp.concatenate([c, c_ctx[None, :], jnp.zeros((npad - bsz - 1, d), F32)], axis=0)
    modv = _mod_vectors(cvec, mod_w, mod_b)

    hw = hy_skip.shape[-1]
    sw = s5_d.shape[-1]
    p0 = _inproj0(x, ctx, modv[0], norm_w[0:1], ev_in_w[0].astype(BF16))
    sc = _shortconv(p0, hy_conv_w[0].astype(F32), hy_conv_b[0].astype(F32), n)
    filt = (hy_w1[0], hy_b1[0], hy_w2[0], hy_b2[0], hy_w3[0], hy_b3[0], hy_freq[0])
    hy_l = _hyena_seq(sc, n, 0, filt, hy_skip[0], hw)
    hy_c = _hyena_seq(sc, nc, n // nc, filt, hy_skip[0], hw)

    s5_ops = _s5_operators(s5_a_re[0], s5_a_im[0], s5_log_dt[0], s5_b_re[0], s5_b_im[0], s5_c_re[0], s5_c_im[0])
    su_blk = ((HY_ORDER + 2) * hw) // sw
    yssm = _s5_ssm(p0, *s5_ops, n_lat=n, n_ctx=nc, col_blk0=su_blk * (sw // LANES))
    s5g = _s5_glu(yssm, p0, s5_d[0], s5_glu_w[0], s5_glu_b[0], u_col_blk=su_blk, sg_col_blk=su_blk + 1)
    x1 = _outproj0(hy_l, hy_c, p0, s5g, x, ctx, modv[0], ev_out_w[0].astype(BF16),
                   hg_col_blk=(HY_ORDER + 1))

    heads = d // (2 * DA_HEAD)
    reps = d // DA_HEAD
    qscale = DA_HEAD ** -0.5 * math.log2(math.e)
    qkw = jnp.stack([jnp.tile(da_q_norm[0].astype(F32), reps) * qscale, jnp.tile(da_k_norm[0].astype(F32), reps)])
    qkw = jnp.concatenate([qkw, jnp.zeros((SUBLANES - 2, d), F32)], axis=0)
    gidx = jnp.arange(2 * LANES) // DA_HEAD
    gm = (gidx[:, None] == gidx[None, :]).astype(BF16) * (1.0 / DA_HEAD)
    cos_t, sin_t = _rope_tables(n, nc)
    lidx = jnp.arange(2 * LANES)
    pm = (lidx[:, None] == (lidx[None, :] ^ (DA_HEAD // 4))).astype(BF16)
    p1 = _inproj1(x1, modv[1], norm_w[1:2], od_in_w[0].astype(BF16), qkw, gm.astype(BF16), pm, cos_t, sin_t, n)
    lam_init = 0.8 - 0.6 * math.exp(-0.3 * 1)
    lvec = jnp.stack([da_lq1[0], da_lk1[0], da_lq2[0], da_lk2[0]]).astype(F32)
    lvec = jnp.pad(lvec, ((0, SUBLANES - 4), (0, LANES - lvec.shape[1])))
    o = _attention(p1, lvec, da_subln[0].reshape(1, 2 * DA_HEAD).astype(F32), n_lat=n, heads=heads,
                   lam_init=lam_init)
    return _outproj1(o, x1, modv[1], od_out_w[0].astype(BF16))
```

```python
import functools
import math

import jax
import jax.numpy as jnp
import numpy as np
from jax import lax
from jax.experimental import pallas as pl
from jax.experimental.pallas import tpu as pltpu

F32 = jnp.float32
BF16 = jnp.bfloat16
HIGHEST = lax.Precision.HIGHEST

EPS = 1e-6
ROW_TILE = 256
LANES = 128
SUBLANES = 8
VMEM_LIMIT = 56 * 1024 * 1024

HY_ORDER = 2
HY_BANDS = 16
HY_TARGET = 1e-2
HY_FAST_PCT = 0.3
HY_SLOW_PCT = 1.5
S5_GROUP = 16
S5_STATE = 64
DA_HEAD = 64
GRID_W = 64
ROPE_BASE = 10000.0


def _cparams(sem):
    return pltpu.CompilerParams(dimension_semantics=sem, vmem_limit_bytes=VMEM_LIMIT)


def _silu(x):
    return x * jax.nn.sigmoid(x)


def _dot(a, b):
    return jnp.dot(a, b, preferred_element_type=F32)


def _mod_kernel(c_ref, w_ref, b_ref, o_ref):
    a = _silu(c_ref[...])
    o_ref[0] = jnp.dot(a, w_ref[0], precision=HIGHEST, preferred_element_type=F32) + b_ref[0]


def _mod_vectors(cvec, mod_w, mod_b):
    depth, d, d3 = mod_w.shape
    tn = 1024
    return pl.pallas_call(
        _mod_kernel,
        out_shape=jax.ShapeDtypeStruct((depth, cvec.shape[0], d3), F32),
        grid=(depth, d3 // tn),
        in_specs=[
            pl.BlockSpec(cvec.shape, lambda l, j: (0, 0)),
            pl.BlockSpec((1, d, tn), lambda l, j: (l, 0, j)),
            pl.BlockSpec((1, 1, tn), lambda l, j: (l, 0, j)),
        ],
        out_specs=pl.BlockSpec((1, cvec.shape[0], tn), lambda l, j: (l, 0, j)),
        compiler_params=_cparams(("arbitrary", "arbitrary")),
        name="mod_vectors",
    )(cvec, mod_w, mod_b.reshape(depth, 1, d3))


def _normed(x, modv_ref, nw_ref, row, d):
    m = modv_ref[pl.ds(row, 1), :]
    shift = m[:, 0:d]
    scale = m[:, d:2 * d]
    ms = jnp.mean(x * x, axis=-1, keepdims=True)
    y = x * lax.rsqrt(ms + EPS) * nw_ref[...]
    return (y * (1.0 + scale) + shift).astype(BF16)


PROJ_COLS = 2048


def _resident_spec(arr):
    zeros = (0,) * arr.ndim
    return pl.BlockSpec(arr.shape, lambda *_: zeros, pipeline_mode=pl.Buffered(1))


def _inproj0_kernel(x_ref, ctx_ref, modv_ref, nw_ref, w_ref, o_ref, *, nl_tiles, bsz, d):
    b = pl.program_id(0)
    i = pl.program_id(1)
    is_lat = i < nl_tiles
    x = jnp.where(is_lat, x_ref[0], ctx_ref[0])
    row = jnp.where(is_lat, b, bsz)
    h = _normed(x, modv_ref, nw_ref, row, d)
    for j in range(w_ref.shape[1] // PROJ_COLS):
        cols = slice(j * PROJ_COLS, (j + 1) * PROJ_COLS)
        o_ref[0, :, cols] = _dot(h, w_ref[:, cols]).astype(o_ref.dtype)


def _inproj0(x, ctx, modv, nw, w):
    bsz, n, d = x.shape
    nc = ctx.shape[1]
    tm = ROW_TILE
    nl_t, nc_t = n // tm, nc // tm
    nout = w.shape[1]
    kern = functools.partial(_inproj0_kernel, nl_tiles=nl_t, bsz=bsz, d=d)
    return pl.pallas_call(
        kern,
        out_shape=jax.ShapeDtypeStruct((bsz, n + nc, nout), BF16),
        grid=(bsz, nl_t + nc_t),
        in_specs=[
            pl.BlockSpec((1, tm, d), lambda b, i: (b, jnp.minimum(i, nl_t - 1), 0)),
            pl.BlockSpec((1, tm, d), lambda b, i: (b, jnp.maximum(i - nl_t, 0), 0)),
            pl.BlockSpec(modv.shape, lambda b, i: (0, 0)),
            pl.BlockSpec((1, d), lambda b, i: (0, 0)),
            _resident_spec(w),
        ],
        out_specs=pl.BlockSpec((1, tm, nout), lambda b, i: (b, i, 0)),
        compiler_params=_cparams(("arbitrary", "arbitrary")),
        name="inproj_even",
    )(x, ctx, modv, nw, w)


def _inproj1_kernel(x_ref, modv_ref, nw_ref, w_ref, qkw_ref, gm_ref, pm_ref, cos_ref, sin_ref, o_ref,
                    *, nl_tiles, bsz, d):
    b = pl.program_id(0)
    i = pl.program_id(1)
    is_lat = i < nl_tiles
    row = jnp.where(is_lat, b, bsz)
    h = _normed(x_ref[0], modv_ref, nw_ref, row, d)

    def plain(j):
        cols = slice(j * PROJ_COLS, (j + 1) * PROJ_COLS)
        o_ref[0, :, cols] = _dot(h, w_ref[:, cols]).astype(o_ref.dtype)

    def normed_rotary(j):
        cols = slice(j * PROJ_COLS, (j + 1) * PROJ_COLS)
        acc = _dot(h, w_ref[:, cols])
        sq = (acc * acc).astype(BF16)
        gm = gm_ref[...]
        width = gm.shape[0]
        tiles = range(acc.shape[1] // width)
        ms = jnp.concatenate([_dot(sq[:, t * width:(t + 1) * width], gm) for t in tiles], axis=1)
        yn = acc * lax.rsqrt(ms + EPS) * qkw_ref[j:j + 1, :]
        reps = acc.shape[1] // cos_ref.shape[1]
        cos = jnp.concatenate([cos_ref[...]] * reps, axis=1)
        sin = jnp.concatenate([sin_ref[...]] * reps, axis=1)
        ynb = yn.astype(BF16)
        pm = pm_ref[...]
        swapped = jnp.concatenate([_dot(ynb[:, t * width:(t + 1) * width], pm) for t in tiles], axis=1)
        o_ref[0, :, cols] = (yn * cos + swapped * sin).astype(o_ref.dtype)

    normed_rotary(1)
    plain(2)

    @pl.when(is_lat)
    def _():
        normed_rotary(0)
        plain(3)

    @pl.when(jnp.logical_not(is_lat))
    def _():
        zeros = jnp.zeros((o_ref.shape[1], PROJ_COLS), o_ref.dtype)
        o_ref[0, :, 0:PROJ_COLS] = zeros
        o_ref[0, :, 3 * PROJ_COLS:4 * PROJ_COLS] = zeros


def _inproj1(xc, modv, nw, w, qkw, gm, pm, cos_t, sin_t, n_lat):
    bsz, r, d = xc.shape
    tm = ROW_TILE
    nl_t = n_lat // tm
    nout = w.shape[1]
    assert nout == 4 * PROJ_COLS
    kern = functools.partial(_inproj1_kernel, nl_tiles=nl_t, bsz=bsz, d=d)
    return pl.pallas_call(
        kern,
        out_shape=jax.ShapeDtypeStruct((bsz, r, nout), BF16),
        grid=(bsz, r // tm),
        in_specs=[
            pl.BlockSpec((1, tm, d), lambda b, i: (b, i, 0)),
            pl.BlockSpec(modv.shape, lambda b, i: (0, 0)),
            pl.BlockSpec((1, d), lambda b, i: (0, 0)),
            _resident_spec(w),
            pl.BlockSpec(qkw.shape, lambda b, i: (0, 0)),
            pl.BlockSpec(gm.shape, lambda b, i: (0, 0)),
            pl.BlockSpec(pm.shape, lambda b, i: (0, 0)),
            pl.BlockSpec((tm, cos_t.shape[1]), lambda b, i: (i, 0)),
            pl.BlockSpec((tm, sin_t.shape[1]), lambda b, i: (i, 0)),
        ],
        out_specs=pl.BlockSpec((1, tm, nout), lambda b, i: (b, i, 0)),
        compiler_params=_cparams(("arbitrary", "arbitrary")),
        name="inproj_odd",
    )(xc, modv, nw, w, qkw, gm, pm, cos_t, sin_t)


def _shortconv_kernel(p_ref, w_ref, b_ref, o_ref, *, n_lat):
    p = p_ref[0].astype(F32)
    r = p.shape[0]
    row = lax.broadcasted_iota(jnp.int32, p.shape, 0)
    prev = pltpu.roll(p, 1, axis=0)
    nxt = pltpu.roll(p, r - 1, axis=0)
    prev = jnp.where((row == 0) | (row == n_lat), 0.0, prev)
    nxt = jnp.where((row == n_lat - 1) | (row == r - 1), 0.0, nxt)
    w = w_ref[...]
    o_ref[0] = (prev * w[0:1] + p * w[1:2] + nxt * w[2:3] + b_ref[...]).astype(o_ref.dtype)


def _shortconv(p0, conv_w, conv_b, n_lat):
    bsz, r, _ = p0.shape
    width = conv_w.shape[1]
    tc = 512
    return pl.pallas_call(
        functools.partial(_shortconv_kernel, n_lat=n_lat),
        out_shape=jax.ShapeDtypeStruct((bsz, r, width), BF16),
        grid=(bsz, width // tc),
        in_specs=[
            pl.BlockSpec((1, r, tc), lambda b, j: (b, 0, j)),
            pl.BlockSpec((3, tc), lambda b, j: (0, j)),
            pl.BlockSpec((1, tc), lambda b, j: (0, j)),
        ],
        out_specs=pl.BlockSpec((1, r, tc), lambda b, j: (b, 0, j)),
        compiler_params=_cparams(("arbitrary", "arbitrary")),
        name="hyena_shortconv",
    )(p0, conv_w, conv_b.reshape(1, width))


def _filter_kernel(feat_ref, w1_ref, b1_ref, w2_ref, b2_ref, w3_ref, b3_ref, freq_ref, delta_ref,
                   h_ref, l1_ref):
    ti = pl.program_id(0)
    feat = feat_ref[...]
    tt = feat.shape[0]
    width = delta_ref.shape[1]
    z1 = jnp.dot(feat, w1_ref[...], precision=HIGHEST, preferred_element_type=F32) + b1_ref[...]
    h1 = jnp.sin(freq_ref[0:1, :] * z1)
    z2 = jnp.dot(h1, w2_ref[...], precision=HIGHEST, preferred_element_type=F32) + b2_ref[...]
    h2 = jnp.sin(freq_ref[1:2, :] * z2)
    decay = jnp.exp(-feat[:, 0:1] * delta_ref[...])
    row = lax.broadcasted_iota(jnp.int32, decay.shape, 0) + ti * tt

    @pl.when(ti == 0)
    def _():
        l1_ref[...] = jnp.zeros(l1_ref.shape, l1_ref.dtype)

    for q in range(w3_ref.shape[1] // width):
        cols = slice(q * width, (q + 1) * width)
        z3 = jnp.dot(h2, w3_ref[:, cols], precision=HIGHEST, preferred_element_type=F32) + b3_ref[:, cols]
        hf = z3 * decay
        if q % 2 == 1:
            hf = jnp.where(row == 0, 0.0, hf)
        h_ref[:, cols] = hf.astype(h_ref.dtype)
        l1_ref[:, cols] += jnp.sum(jnp.abs(hf).reshape(tt // SUBLANES, SUBLANES, width), axis=0)


def _hyena_filter_taps(n, w1, b1, w2, b2, w3, b3, freq, width):
    t = np.arange(n, dtype=np.float64)
    bands = np.linspace(1e-4, HY_BANDS - 1, HY_BANDS)
    ang = (2.0 * math.pi / n) * t[:, None] * bands[None, :]
    feat = np.concatenate([(t / n)[:, None], np.cos(ang), -np.sin(ang)], axis=-1)
    emb = feat.shape[1]
    feat = jnp.asarray(np.pad(feat, ((0, 0), (0, LANES - emb))), F32)
    w1p = jnp.pad(w1.astype(F32), ((0, LANES - emb), (0, 0)))
    ffn = w1.shape[1]
    deltas = jnp.asarray(np.abs(np.linspace(math.log(HY_TARGET) / HY_SLOW_PCT, math.log(HY_TARGET) / HY_FAST_PCT,
                                            width)).reshape(1, width), F32)
    ncol = w3.shape[1]
    tt = min(512, n)
    full = lambda i: (0, 0)
    return pl.pallas_call(
        _filter_kernel,
        out_shape=(jax.ShapeDtypeStruct((n, ncol), BF16), jax.ShapeDtypeStruct((SUBLANES, ncol), F32)),
        grid=(n // tt,),
        in_specs=[
            pl.BlockSpec((tt, LANES), lambda i: (i, 0)),
            pl.BlockSpec((LANES, ffn), full),
            pl.BlockSpec((1, ffn), full),
            pl.BlockSpec((ffn, ffn), full),
            pl.BlockSpec((1, ffn), full),
            pl.BlockSpec((ffn, ncol), full),
            pl.BlockSpec((1, ncol), full),
            pl.BlockSpec((2, ffn), full),
            pl.BlockSpec((1, width), full),
        ],
        out_specs=(pl.BlockSpec((tt, ncol), lambda i: (i, 0)),
                   pl.BlockSpec((SUBLANES, ncol), full)),
        compiler_params=_cparams(("arbitrary",)),
        name="hyena_filter_taps",
    )(feat, w1p, b1.reshape(1, ffn).astype(F32), w2.astype(F32), b2.reshape(1, ffn).astype(F32),
      w3.astype(F32), b3.reshape(1, ncol).astype(F32), freq.astype(F32), deltas)


def _dft_tables(n):
    idx = np.arange(n)
    ang = ((2 * idx[:, None] + 1) * idx[None, :] % (4 * n)) * (math.pi / (2 * n))
    fc, fs = np.cos(ang), -np.sin(ang)
    return tuple(jnp.asarray(t.astype(BF16)) for t in (fc, fs, fc.T, fs.T))


def _dft_fwd_kernel(fc_ref, fs_ref, z_ref, o_re_ref, o_im_ref):
    z = z_ref[...].astype(BF16)
    o_re_ref[...] = _dot(fc_ref[...], z)
    o_im_ref[...] = _dot(fs_ref[...], z)


def _filter_spectrum(fc, fs, taps):
    n, ncol = taps.shape
    tf = min(512, n)
    tcn = 512
    return pl.pallas_call(
        _dft_fwd_kernel,
        out_shape=(jax.ShapeDtypeStruct((n, ncol), F32), jax.ShapeDtypeStruct((n, ncol), F32)),
        grid=(ncol // tcn, n // tf),
        in_specs=[
            pl.BlockSpec((tf, n), lambda j, f: (f, 0)),
            pl.BlockSpec((tf, n), lambda j, f: (f, 0)),
            pl.BlockSpec((n, tcn), lambda j, f: (0, j)),
        ],
        out_specs=(pl.BlockSpec((tf, tcn), lambda j, f: (f, j)),
                   pl.BlockSpec((tf, tcn), lambda j, f: (f, j))),
        compiler_params=_cparams(("arbitrary", "arbitrary")),
        name="hyena_filter_spectrum",
    )(fc, fs, taps)


def _dft_fwd_mul_kernel(fc_ref, fs_ref, z_ref, kr_ref, ki_ref, pr_ref, pi_ref):
    z = z_ref[0]
    ur = _dot(fc_ref[...], z)
    ui = _dot(fs_ref[...], z)
    kr = kr_ref[...]
    ki = ki_ref[...]
    pr_ref[0] = (ur * kr - ui * ki).astype(pr_ref.dtype)
    pi_ref[0] = (ur * ki + ui * kr).astype(pi_ref.dtype)


def _dft_fwd_mul(fc, fs, z, kr, ki, *, n, row_blk, col_blk0):
    bsz = z.shape[0]
    width = kr.shape[1]
    tf = min(512, n)
    tcn = 512
    return pl.pallas_call(
        _dft_fwd_mul_kernel,
        out_shape=(jax.ShapeDtypeStruct((bsz, n, width), BF16), jax.ShapeDtypeStruct((bsz, n, width), BF16)),
        grid=(width // tcn, n // tf, bsz),
        in_specs=[
            pl.BlockSpec((tf, n), lambda j, f, b: (f, 0)),
            pl.BlockSpec((tf, n), lambda j, f, b: (f, 0)),
            pl.BlockSpec((1, n, tcn), lambda j, f, b: (b, row_blk, col_blk0 + j)),
            pl.BlockSpec((tf, tcn), lambda j, f, b: (f, j)),
            pl.BlockSpec((tf, tcn), lambda j, f, b: (f, j)),
        ],
        out_specs=(pl.BlockSpec((1, tf, tcn), lambda j, f, b: (b, f, j)),
                   pl.BlockSpec((1, tf, tcn), lambda j, f, b: (b, f, j))),
        compiler_params=_cparams(("arbitrary", "arbitrary", "arbitrary")),
        name="hyena_dft_fwd",
    )(fc, fs, z, kr, ki)


def _dft_inv_gate_kernel(tc_ref, ts_ref, pr_ref, pi_ref, z_ref, xg_ref, skip_ref, o_ref):
    y = _dot(tc_ref[...], pr_ref[0]) + _dot(ts_ref[...], pi_ref[0])
    z = z_ref[0].astype(F32)
    xg = xg_ref[0].astype(F32)
    o_ref[0] = (xg * (y + z * skip_ref[...])).astype(o_ref.dtype)


def _dft_inv_gate(tc, ts, pr, pi, z, xg, skip, *, n, z_row_blk, z_col_blk0, xg_row_blk, xg_col_blk0):
    bsz, _, width = pr.shape
    tt = min(512, n)
    tcn = 512
    zrb = z_row_blk * (n // tt)
    xrb = xg_row_blk * (n // tt)
    return pl.pallas_call(
        _dft_inv_gate_kernel,
        out_shape=jax.ShapeDtypeStruct((bsz, n, width), BF16),
        grid=(width // tcn, bsz, n // tt),
        in_specs=[
            pl.BlockSpec((tt, n), lambda j, b, i: (i, 0)),
            pl.BlockSpec((tt, n), lambda j, b, i: (i, 0)),
            pl.BlockSpec((1, n, tcn), lambda j, b, i: (b, 0, j)),
            pl.BlockSpec((1, n, tcn), lambda j, b, i: (b, 0, j)),
            pl.BlockSpec((1, tt, tcn), lambda j, b, i: (b, zrb + i, z_col_blk0 + j)),
            pl.BlockSpec((1, tt, tcn), lambda j, b, i: (b, xrb + i, xg_col_blk0 + j)),
            pl.BlockSpec((1, tcn), lambda j, b, i: (0, j)),
        ],
        out_specs=pl.BlockSpec((1, tt, tcn), lambda j, b, i: (b, i, j)),
        compiler_params=_cparams(("arbitrary", "arbitrary", "arbitrary")),
        name="hyena_dft_inv",
    )(tc, ts, pr, pi, z, xg, skip)


FFT_N2 = 256
FFT_GROUP = 16


def _fft_tables(n):
    n1 = (2 * n) // FFT_N2
    h1 = n1 // 2
    nf2 = n // n1
    f1 = np.arange(n1)
    t1 = np.arange(h1)
    th = ((2 * f1[:, None] + 1) * t1[None, :] % (2 * n1)) * (math.pi / n1)
    base = np.stack([np.cos(th), -np.sin(th)])
    m1 = np.einsum('pft,jk->pfjtk', base, np.eye(FFT_GROUP)).reshape(2 * n1 * FFT_GROUP, h1 * FFT_GROUP)
    f = f1[:, None, None] + n1 * np.arange(nf2)[None, :, None]
    t2 = np.arange(FFT_N2)[None, None, :]
    psi = ((2 * f + 1) * t2 % (4 * n)) * (math.pi / (2 * n))
    cs = np.concatenate([np.cos(psi), np.sin(psi)], axis=1)
    tables = (m1, m1.T, cs, np.swapaxes(cs, 1, 2))
    return tuple(jnp.asarray(np.ascontiguousarray(t).astype(BF16)) for t in tables)


def _fft_stage1(z_ref, m1_ref, a_ref, n1):
    h1 = n1 // 2

    def body(g, carry):
        r0 = pl.multiple_of(g * FFT_GROUP, FFT_GROUP)
        xg = jnp.concatenate(
            [z_ref[pl.ds(pl.multiple_of(FFT_N2 * t1 + r0, FFT_GROUP), FFT_GROUP), :] for t1 in range(h1)], axis=0)
        out = _dot(m1_ref[...], xg).astype(BF16)
        for part in range(2):
            for f1 in range(n1):
                row = (part * n1 + f1) * FFT_GROUP
                a_ref[part, f1, pl.ds(r0, FFT_GROUP), :] = out[row:row + FFT_GROUP]
        return carry

    lax.fori_loop(0, FFT_N2 // FFT_GROUP, body, 0, unroll=4)


def _fft_stage2(a_ref, cs_ref, f1):
    rhs = jnp.concatenate([a_ref[0, f1], a_ref[1, f1]], axis=1)
    r = _dot(cs_ref[f1], rhs)
    nf2 = r.shape[0] // 2
    w = r.shape[1] // 2
    xr = r[0:nf2, 0:w] + r[nf2:, w:]
    xi = r[0:nf2, w:] - r[nf2:, 0:w]
    return xr, xi


def _fft_filter_kernel(hf_ref, hb_ref, l1f_ref, l1b_ref, m1_ref, cs_ref, o_ref, sd_ref, a_ref, *, n1, n):
    hf = hf_ref[...].astype(F32)
    hb = hb_ref[...].astype(F32)
    sd_ref[0] = (hf + hb).astype(BF16)
    sd_ref[1] = (hf - hb).astype(BF16)
    _fft_stage1(sd_ref.at[0], m1_ref, a_ref.at[0], n1)
    _fft_stage1(sd_ref.at[1], m1_ref, a_ref.at[1], n1)
    l1 = jnp.sum(l1f_ref[...], axis=0, keepdims=True) + jnp.sum(l1b_ref[...], axis=0, keepdims=True)
    norm = (1.0 / n) / l1

    def body(f1, carry):
        cs = cs_ref[f1]
        nf2 = cs.shape[0] // 2
        c, s = cs[0:nf2], cs[nf2:]
        kr = _dot(c, a_ref[0, 0, f1]) + _dot(s, a_ref[0, 1, f1])
        ki = _dot(c, a_ref[1, 1, f1]) - _dot(s, a_ref[1, 0, f1])
        rows = pl.ds(pl.multiple_of(f1 * nf2, nf2), nf2)
        o_ref[0, 0, rows, :] = (kr * norm).astype(o_ref.dtype)
        o_ref[0, 1, rows, :] = (ki * norm).astype(o_ref.dtype)
        return carry

    lax.fori_loop(0, n1, body, 0, unroll=min(4, n1))


def _const_spec(arr):
    nd = arr.ndim
    return pl.BlockSpec(arr.shape, lambda *_: (0,) * nd)


def _fft_filter_spectrum(taps, l1, m1, cs, width):
    n, _ = taps.shape
    n1 = cs.shape[0]
    tcn = 256
    cb = width // tcn
    return pl.pallas_call(
        functools.partial(_fft_filter_kernel, n1=n1, n=n),
        out_shape=jax.ShapeDtypeStruct((HY_ORDER, 2, n, width), BF16),
        grid=(HY_ORDER, cb),
        in_specs=[
            pl.BlockSpec((n, tcn), lambda i, j: (0, (2 * i) * cb + j)),
            pl.BlockSpec((n, tcn), lambda i, j: (0, (2 * i + 1) * cb + j)),
            pl.BlockSpec((SUBLANES, tcn), lambda i, j: (0, (2 * i) * cb + j)),
            pl.BlockSpec((SUBLANES, tcn), lambda i, j: (0, (2 * i + 1) * cb + j)),
            _const_spec(m1), _const_spec(cs),
        ],
        out_specs=pl.BlockSpec((1, 2, n, tcn), lambda i, j: (i, 0, 0, j)),
        scratch_shapes=[pltpu.VMEM((2, n, tcn), BF16), pltpu.VMEM((2, 2, n1, FFT_N2, tcn), BF16)],
        compiler_params=_cparams(("arbitrary", "arbitrary")),
        name="hyena_fft_filter_spectrum",
    )(taps, taps, l1, l1, m1, cs)


def _fft_conv_kernel(z_ref, xg_ref, k_ref, skip_ref, m1_ref, m1t_ref, cs_ref, cst_ref, o_ref, a_ref, *, n1):
    h1 = n1 // 2
    _fft_stage1(z_ref.at[0], m1_ref, a_ref, n1)

    def mid(f1, carry):
        xr, xi = _fft_stage2(a_ref, cs_ref, f1)
        nf2 = xr.shape[0]
        rows = pl.ds(pl.multiple_of(f1 * nf2, nf2), nf2)
        kr = k_ref[0, 0, rows, :].astype(F32)
        ki = k_ref[0, 1, rows, :].astype(F32)
        pr = xr * kr - xi * ki
        pi = xr * ki + xi * kr
        rhs = jnp.concatenate([jnp.concatenate([pr, pi], axis=1), jnp.concatenate([-pi, pr], axis=1)],
                              axis=0).astype(BF16)
        d = _dot(cst_ref[f1], rhs).astype(BF16)
        w = d.shape[1] // 2
        a_ref[0, f1] = d[:, 0:w]
        a_ref[1, f1] = d[:, w:]
        return carry

    lax.fori_loop(0, n1, mid, 0, unroll=min(8, n1))

    skip = skip_ref[...]

    def last(g, carry):
        r0 = pl.multiple_of(g * FFT_GROUP, FFT_GROUP)
        dg = jnp.concatenate(
            [a_ref[part, f1, pl.ds(r0, FFT_GROUP), :] for part in range(2) for f1 in range(n1)], axis=0)
        yg = _dot(m1t_ref[...], dg)
        for t1 in range(h1):
            rows = pl.ds(pl.multiple_of(FFT_N2 * t1 + r0, FFT_GROUP), FFT_GROUP)
            z = z_ref[0, rows, :].astype(F32)
            xg = xg_ref[0, rows, :].astype(F32)
            o_ref[0, rows, :] = (xg * (yg[t1 * FFT_GROUP:(t1 + 1) * FFT_GROUP] + z * skip)).astype(o_ref.dtype)
        return carry

    lax.fori_loop(0, FFT_N2 // FFT_GROUP, last, 0, unroll=4)


def _fft_conv_gate(z, xg, kspec, order, skip, tables, *, n, z_row_blk, z_col_blk0, xg_row_blk, xg_col_blk0):
    m1, m1t, cs, cst = tables
    bsz = z.shape[0]
    width = kspec.shape[3]
    n1 = cs.shape[0]
    tcn = 256
    return pl.pallas_call(
        functools.partial(_fft_conv_kernel, n1=n1),
        out_shape=jax.ShapeDtypeStruct((bsz, n, width), BF16),
        grid=(width // tcn, bsz),
        in_specs=[
            pl.BlockSpec((1, n, tcn), lambda j, b: (b, z_row_blk, z_col_blk0 + j)),
            pl.BlockSpec((1, n, tcn), lambda j, b: (b, xg_row_blk, xg_col_blk0 + j)),
            pl.BlockSpec((1, 2, n, tcn), lambda j, b: (order, 0, 0, j)),
            pl.BlockSpec((1, tcn), lambda j, b: (0, j)),
            _const_spec(m1), _const_spec(m1t), _const_spec(cs), _const_spec(cst),
        ],
        out_specs=pl.BlockSpec((1, n, tcn), lambda j, b: (b, 0, j)),
        scratch_shapes=[pltpu.VMEM((2, n1, FFT_N2, tcn), BF16)],
        compiler_params=_cparams(("arbitrary", "arbitrary")),
        name="hyena_fft_conv",
    )(z, xg, kspec, skip, m1, m1t, cs, cst)


def _hyena_seq_fft(sc, n, seq_row_blk, filt, skip, width):
    w1, b1, w2, b2, w3, b3, freq = filt
    taps, l1 = _hyena_filter_taps(n, w1, b1, w2, b2, w3, b3, freq, width)
    tables = _fft_tables(n)
    kspec = _fft_filter_spectrum(taps, l1, tables[0], tables[2], width)
    cb = width // 256
    y = None
    for i in range(HY_ORDER):
        if i == 0:
            z, zrb, zcb = sc, seq_row_blk, 0
        else:
            z, zrb, zcb = y, 0, 0
        y = _fft_conv_gate(z, sc, kspec, i, skip[i].reshape(1, width).astype(F32), tables, n=n,
                           z_row_blk=zrb, z_col_blk0=zcb, xg_row_blk=seq_row_blk, xg_col_blk0=(i + 1) * cb)
    return y


def _hyena_seq(sc, n, seq_row_blk, filt, skip, width):
    if n % (2 * FFT_N2) == 0:
        return _hyena_seq_fft(sc, n, seq_row_blk, filt, skip, width)
    w1, b1, w2, b2, w3, b3, freq = filt
    taps, l1 = _hyena_filter_taps(n, w1, b1, w2, b2, w3, b3, freq, width)
    fc, fs, tc, ts = _dft_tables(n)
    sr, si = _filter_spectrum(fc, fs, taps)
    sr = sr.reshape(n, HY_ORDER, 2, width)
    si = si.reshape(n, HY_ORDER, 2, width)
    l1 = jnp.sum(l1, axis=0).reshape(HY_ORDER, 2, width).sum(axis=1)
    norm = (1.0 / n) / l1
    cb = width // 512
    y = None
    for i in range(HY_ORDER):
        kr = (sr[:, i, 0] + sr[:, i, 1]) * norm[i]
        ki = (si[:, i, 0] - si[:, i, 1]) * norm[i]
        if i == 0:
            z, zrb, zcb = sc, seq_row_blk, 0
        else:
            z, zrb, zcb = y, 0, 0
        pr, pi = _dft_fwd_mul(fc, fs, z, kr, ki, n=n, row_blk=zrb, col_blk0=zcb)
        y = _dft_inv_gate(tc, ts, pr, pi, z, sc, skip[i].reshape(1, width).astype(F32), n=n,
                          z_row_blk=zrb, z_col_blk0=zcb, xg_row_blk=seq_row_blk, xg_col_blk0=(i + 1) * cb)
    return y


S5_CHUNK = 16
S5_BLOCK_GROUPS = 8


def _piece_transpose(arrs):
    arrs = list(arrs)
    lane = lax.broadcasted_iota(jnp.int32, arrs[0].shape, arrs[0].ndim - 1)
    piece = lane // S5_GROUP
    axis = arrs[0].ndim - 1
    for s in (4, 2, 1):
        low = (piece & s) == 0
        nxt = list(arrs)
        for r in range(len(arrs)):
            if r & s:
                continue
            lo, hi = arrs[r], arrs[r + s]
            nxt[r] = jnp.where(low, lo, pltpu.roll(hi, s * S5_GROUP, axis=axis))
            nxt[r + s] = jnp.where(low, pltpu.roll(lo, LANES - s * S5_GROUP, axis=axis), hi)
        arrs = nxt
    return arrs


def _s5_kernel(u_ref, t_ref, wb_ref, v_ref, l16_ref, y_ref, uf_ref, x_ref, yi_ref, d_ref, *, nl, nc):
    nch = nl + nc
    ng = S5_BLOCK_GROUPS
    npair = ng // 2
    pitch = nch + 4
    half = LANES

    uf_ref[...] = u_ref[0].astype(F32)
    for h in range(2):
        steps = [uf_ref[pl.ds(h * ng + j, nch, stride=S5_CHUNK), :] for j in range(ng)]
        for gl, xg in enumerate(_piece_transpose(steps)):
            x_ref[gl, :, h * half:(h + 1) * half] = xg.astype(BF16)

    for q in range(ng // 2):
        g0, g1 = 2 * q, 2 * q + 1
        x0, x1 = x_ref[g0], x_ref[g1]
        yi_ref[g0] = _dot(x0, t_ref[0, g0])
        yi_ref[g1] = _dot(x1, t_ref[0, g1])
        dp = _dot(jnp.concatenate([x0, x1], axis=1), wb_ref[0, q])
        for k in range(4):
            d_ref[k, q * pitch:q * pitch + nch, :] = dp[:, k * LANES:(k + 1) * LANES]

    lrf, lif, lrr, lir = l16_ref[0, 0], l16_ref[0, 1], l16_ref[0, 2], l16_ref[0, 3]

    def step(s, carry):
        srf, sif, srr, sir = carry
        rows_f = pl.ds(jnp.where(s < nc, nl + s, s - nc), npair, stride=pitch)
        rows_r = pl.ds(nch - 1 - s, npair, stride=pitch)
        xrf = d_ref[0, rows_f, :]
        xif = d_ref[1, rows_f, :]
        xrr = d_ref[2, rows_r, :]
        xir = d_ref[3, rows_r, :]
        d_ref[0, rows_f, :] = srf
        d_ref[1, rows_f, :] = sif
        d_ref[2, rows_r, :] = srr
        d_ref[3, rows_r, :] = sir
        return (lrf * srf - lif * sif + xrf, lrf * sif + lif * srf + xif,
                lrr * srr - lir * sir + xrr, lrr * sir + lir * srr + xir)

    zero = jnp.zeros((npair, LANES), F32)
    lax.fori_loop(0, nch, step, (zero, zero, zero, zero), unroll=2)

    for q in range(ng // 2):
        rows = slice(q * pitch, q * pitch + nch)
        sf = jnp.concatenate([d_ref[0, rows, :], d_ref[1, rows, :]], axis=1).astype(BF16)
        sr = jnp.concatenate([d_ref[2, rows, :], d_ref[3, rows, :]], axis=1).astype(BF16)
        yq = _dot(sf, v_ref[0, 0, q]) + _dot(sr, v_ref[0, 1, q])
        w = yq.shape[1] // 2
        yi_ref[2 * q] += yq[:, 0:w]
        yi_ref[2 * q + 1] += yq[:, w:]
    for h in range(2):
        groups = [yi_ref[gl, :, h * half:(h + 1) * half] for gl in range(ng)]
        for j, yj in enumerate(_piece_transpose(groups)):
            uf_ref[pl.ds(h * ng + j, nch, stride=S5_CHUNK), :] = yj
    y_ref[0] = uf_ref[...].astype(y_ref.dtype)


def _s5_ssm(p0, tsum, wbp, vp, l16, *, n_lat, n_ctx, col_blk0):
    bsz, r, _ = p0.shape
    nblk = tsum.shape[0]
    nl, nc = n_lat // S5_CHUNK, n_ctx // S5_CHUNK
    nch = nl + nc
    ng = S5_BLOCK_GROUPS
    gw = S5_CHUNK * S5_GROUP
    return pl.pallas_call(
        functools.partial(_s5_kernel, nl=nl, nc=nc),
        out_shape=jax.ShapeDtypeStruct((bsz, r, nblk * LANES), BF16),
        grid=(nblk, bsz),
        in_specs=[
            pl.BlockSpec((1, r, LANES), lambda k, b: (b, 0, col_blk0 + k)),
            pl.BlockSpec((1,) + tsum.shape[1:], lambda k, b: (k, 0, 0, 0)),
            pl.BlockSpec((1,) + wbp.shape[1:], lambda k, b: (k, 0, 0, 0)),
            pl.BlockSpec((1,) + vp.shape[1:], lambda k, b: (k, 0, 0, 0, 0)),
            pl.BlockSpec((1,) + l16.shape[1:], lambda k, b: (k, 0, 0, 0)),
        ],
        out_specs=pl.BlockSpec((1, r, LANES), lambda k, b: (b, 0, k)),
        scratch_shapes=[
            pltpu.VMEM((r, LANES), F32),
            pltpu.VMEM((ng, nch, gw), BF16),
            pltpu.VMEM((ng, nch, gw), F32),
            pltpu.VMEM((4, (ng // 2) * (nch + 4), LANES), F32),
        ],
        compiler_params=_cparams(("arbitrary", "arbitrary")),
        name="s5_ssm",
    )(p0, tsum, wbp, vp, l16)


def _s5_operators(a_re, a_im, log_dt, b_re, b_im, c_re, c_im):
    f32 = F32
    ndir, g, p = a_re.shape
    cin = b_re.shape[-1]
    L = S5_CHUNK
    gw = L * cin
    ar, ai = a_re.astype(f32), a_im.astype(f32)
    dt = jnp.exp(log_dt.astype(f32))[..., None]
    ls = jnp.arange(L + 1, dtype=f32)[:, None, None, None]
    mag = jnp.exp(ls * (ar * dt)[None])
    pr = mag * jnp.cos(ls * (ai * dt)[None])
    pi = mag * jnp.sin(ls * (ai * dt)[None])
    lr, li = pr[1], pi[1]
    den = ar * ar + ai * ai
    zr = ((lr - 1.0) * ar + li * ai) / den
    zi = (li * ar - (lr - 1.0) * ai) / den
    btr = jnp.swapaxes(b_re.astype(f32), -1, -2)
    bti = jnp.swapaxes(b_im.astype(f32), -1, -2)
    bbr = zr[:, :, None, :] * btr - zi[:, :, None, :] * bti
    bbi = zr[:, :, None, :] * bti + zi[:, :, None, :] * btr
    cr, ci = c_re.astype(f32), c_im.astype(f32)

    lbr = pr[:L, :, :, None, :] * bbr[None] - pi[:L, :, :, None, :] * bbi[None]
    lbi = pr[:L, :, :, None, :] * bbi[None] + pi[:L, :, :, None, :] * bbr[None]

    krow = (jnp.einsum('dgkp,ldgcp->dgclk', cr, lbr)
            - jnp.einsum('dgkp,ldgcp->dgclk', ci, lbi)).reshape(ndir, g, cin, gw)
    krev = jnp.flip(krow[1].reshape(g, cin, L, cin), axis=2).reshape(g, cin, gw)
    t_f = jnp.stack([jnp.pad(krow[0][:, :, :gw - cin * i], ((0, 0), (0, 0), (cin * i, 0))) for i in range(L)], axis=1)
    t_r = jnp.stack([jnp.pad(krev[:, :, cin * (L - 1 - i):], ((0, 0), (0, 0), (0, cin * (L - 1 - i))))
                     for i in range(L)], axis=1)
    tsum = (t_f + t_r).reshape(g, gw, gw)

    def chunk_rows(m):
        return jnp.swapaxes(m, 0, 1).reshape(g, gw, p)

    wf_r, wf_i = chunk_rows(lbr[::-1, 0]), chunk_rows(lbi[::-1, 0])
    wr_r, wr_i = chunk_rows(lbr[:, 1]), chunk_rows(lbi[:, 1])

    def state_out(d, er, ei):
        cxr = jnp.tile(jnp.swapaxes(cr[d], -1, -2), (1, 1, L))
        cxi = jnp.tile(jnp.swapaxes(ci[d], -1, -2), (1, 1, L))
        pxr = jnp.repeat(jnp.transpose(er, (1, 2, 0)), cin, axis=-1)
        pxi = jnp.repeat(jnp.transpose(ei, (1, 2, 0)), cin, axis=-1)
        return cxr * pxr - cxi * pxi, -(cxr * pxi + cxi * pxr)

    vf_re, vf_im = state_out(0, pr[1:L + 1, 0], pi[1:L + 1, 0])
    vr_re, vr_im = state_out(1, pr[1:L + 1, 1][::-1], pi[1:L + 1, 1][::-1])

    nblk = g // S5_BLOCK_GROUPS
    npair = S5_BLOCK_GROUPS // 2

    def pair_cols(m):
        m = m.reshape(nblk, npair, 2, m.shape[1], m.shape[2])
        z = jnp.zeros_like(m[:, :, 0])
        top = jnp.concatenate([m[:, :, 0], z], axis=-1)
        bot = jnp.concatenate([z, m[:, :, 1]], axis=-1)
        return jnp.concatenate([top, bot], axis=-2)

    wbp = jnp.concatenate([pair_cols(wf_r), pair_cols(wf_i), pair_cols(wr_r), pair_cols(wr_i)], axis=-1)

    vp = jnp.stack([jnp.concatenate([pair_cols(vf_re), pair_cols(vf_im)], axis=-2),
                    jnp.concatenate([pair_cols(vr_re), pair_cols(vr_im)], axis=-2)], axis=1)

    def rows(v):
        return v.reshape(nblk, npair, 2 * p)

    l16 = jnp.stack([rows(pr[L, 0]), rows(pi[L, 0]), rows(pr[L, 1]), rows(pi[L, 1])], axis=1)
    tsum = tsum.reshape(nblk, S5_BLOCK_GROUPS, gw, gw)
    return tsum.astype(BF16), wbp.astype(BF16), vp.astype(BF16), l16


def _s5_glu_kernel(y_ref, u_ref, sg_ref, d_ref, w_ref, b_ref, o_ref):
    y = u_ref[0].astype(F32) * d_ref[...] + y_ref[0].astype(F32)
    g = 0.5 * y * (1.0 + jnp.tanh(math.sqrt(2.0 / math.pi) * (y + 0.044715 * (y * y * y))))
    z = _dot(g.astype(BF16), w_ref[...]) + b_ref[...]
    sg = sg_ref[0].astype(F32)
    o_ref[0] = (g * jax.nn.sigmoid(z) * _silu(sg)).astype(o_ref.dtype)


def _s5_glu(yssm, p0, d, glu_w, glu_b, *, u_col_blk, sg_col_blk):
    bsz, r, width = yssm.shape
    tm = ROW_TILE
    return pl.pallas_call(
        _s5_glu_kernel,
        out_shape=jax.ShapeDtypeStruct((bsz, r, width), BF16),
        grid=(bsz, r // tm),
        in_specs=[
            pl.BlockSpec((1, tm, width), lambda b, i: (b, i, 0)),
            pl.BlockSpec((1, tm, width), lambda b, i: (b, i, u_col_blk)),
            pl.BlockSpec((1, tm, width), lambda b, i: (b, i, sg_col_blk)),
            pl.BlockSpec((1, width), lambda b, i: (0, 0)),
            pl.BlockSpec((width, width), lambda b, i: (0, 0)),
            pl.BlockSpec((1, width), lambda b, i: (0, 0)),
        ],
        out_specs=pl.BlockSpec((1, tm, width), lambda b, i: (b, i, 0)),
        compiler_params=_cparams(("arbitrary", "arbitrary")),
        name="s5_glu",
    )(yssm, p0, p0, d.reshape(1, width).astype(F32), glu_w.astype(BF16),
      glu_b.reshape(1, width).astype(F32))


def _outproj0_kernel(hyl_ref, hyc_ref, hg_ref, s5_ref, x_ref, ctx_ref, modv_ref, w_ref, o_ref,
                     *, nl_tiles, bsz, d, hw):
    b = pl.program_id(0)
    i = pl.program_id(1)
    is_lat = i < nl_tiles
    hy = jnp.where(is_lat, hyl_ref[0], hyc_ref[0]).astype(F32)
    a = (hy * _silu(hg_ref[0].astype(F32))).astype(BF16)
    acc = _dot(a, w_ref[0:hw, :]) + _dot(s5_ref[0], w_ref[hw:, :])
    row = jnp.where(is_lat, b, bsz)
    gate = modv_ref[pl.ds(row, 1), :][:, 2 * d:3 * d]
    xin = jnp.where(is_lat, x_ref[0], ctx_ref[0])
    o_ref[0] = xin + gate * acc


def _outproj0(hy_l, hy_c, p0, s5g, x, ctx, modv, w, *, hg_col_blk):
    bsz, n, d = x.shape
    nc = ctx.shape[1]
    tm = ROW_TILE
    nl_t, nc_t = n // tm, nc // tm
    hw = hy_l.shape[2]
    kern = functools.partial(_outproj0_kernel, nl_tiles=nl_t, bsz=bsz, d=d, hw=hw)
    lat = lambda b, i: (b, jnp.minimum(i, nl_t - 1), 0)
    cx = lambda b, i: (b, jnp.maximum(i - nl_t, 0), 0)
    return pl.pallas_call(
        kern,
        out_shape=jax.ShapeDtypeStruct((bsz, n + nc, d), F32),
        grid=(bsz, nl_t + nc_t),
        in_specs=[
            pl.BlockSpec((1, tm, hw), lat),
            pl.BlockSpec((1, tm, hw), cx),
            pl.BlockSpec((1, tm, hw), lambda b, i: (b, i, hg_col_blk)),
            pl.BlockSpec((1, tm, hw), lambda b, i: (b, i, 0)),
            pl.BlockSpec((1, tm, d), lat),
            pl.BlockSpec((1, tm, d), cx),
            pl.BlockSpec(modv.shape, lambda b, i: (0, 0)),
            pl.BlockSpec(w.shape, lambda b, i: (0, 0)),
        ],
        out_specs=pl.BlockSpec((1, tm, d), lambda b, i: (b, i, 0)),
        compiler_params=_cparams(("arbitrary", "arbitrary")),
        name="outproj_even",
    )(hy_l, hy_c, p0, s5g, x, ctx, modv, w)


ATTN_SAFE_LOG2 = 57.0
ATTN_KEY_CHUNK = 256


def _attn_prepare(k_ref, v_ref, vt_ref, kn_ref):
    hd = v_ref.shape[2]
    vt_ref[0:hd, :] = v_ref[0].astype(F32).T.astype(BF16)
    vt_ref[hd:, :] = jnp.ones((vt_ref.shape[0] - hd, vt_ref.shape[1]), BF16)
    kn_ref[0:1, :] = _max_subhead_sqnorm(k_ref[0])


def _max_subhead_sqnorm(x):
    hd = x.shape[1]
    sel = ((lax.broadcasted_iota(jnp.int32, (hd, hd), 0) // DA_HEAD)
           == lax.broadcasted_iota(jnp.int32, (hd, hd), 1)).astype(BF16)
    return jnp.max(_dot(x * x, sel), axis=0, keepdims=True)


def _attn_kernel(q_ref, k_ref, v_ref, g_ref, lvec_ref, subln_ref, o_ref, vt_ref, kn_ref, st_ref, *, lam_init):
    @pl.when(pl.program_id(2) == 0)
    def _():
        _attn_prepare(k_ref, v_ref, vt_ref, kn_ref)

    q = q_ref[0]
    hd = q.shape[1]
    lane = lax.broadcasted_iota(jnp.int32, q.shape, 1)
    zero = jnp.zeros_like(q)
    qs = (jnp.where(lane < DA_HEAD, q, zero), jnp.where(lane >= DA_HEAD, q, zero))
    lv = lvec_ref[...]
    lam = (jnp.exp(jnp.sum(lv[0:1] * lv[1:2], axis=-1, keepdims=True))
           - jnp.exp(jnp.sum(lv[2:3] * lv[3:4], axis=-1, keepdims=True)) + lam_init)

    bound_sq = _max_subhead_sqnorm(q) * kn_ref[0:1, :]
    safe = jnp.max(bound_sq) <= ATTN_SAFE_LOG2 * ATTN_SAFE_LOG2

    nkc = k_ref.shape[1] // ATTN_KEY_CHUNK

    def scores_t(c, m):
        kc = k_ref[0, c * ATTN_KEY_CHUNK:(c + 1) * ATTN_KEY_CHUNK, :]
        return lax.dot_general(kc, qs[m], (((1,), (1,)), ((), ())), preferred_element_type=F32)

    def attend(subtract_max):
        shifts = [None, None]
        if subtract_max:
            for m in range(2):
                for c in range(nkc):
                    cm = jnp.max(scores_t(c, m), axis=0, keepdims=True)
                    shifts[m] = cm if shifts[m] is None else jnp.maximum(shifts[m], cm)
        accs = [None, None]

        def stage(c):
            for m in range(2):
                st_ref[c % 2, m] = scores_t(c, m)

        stage(0)
        for c in range(nkc):
            if c + 1 < nkc:
                stage(c + 1)
            for m in range(2):
                st = st_ref[c % 2, m]
                if subtract_max:
                    st = st - shifts[m]
                p = jnp.exp2(st).astype(BF16)
                part = _dot(vt_ref[:, c * ATTN_KEY_CHUNK:(c + 1) * ATTN_KEY_CHUNK], p)
                accs[m] = part if accs[m] is None else accs[m] + part
        outs = [acc[0:hd] * (1.0 / acc[hd:hd + 1]) for acc in accs]
        o = (outs[0] - lam * outs[1]).T
        ms = jnp.mean(o * o, axis=-1, keepdims=True)
        o = o * lax.rsqrt(ms + EPS) * subln_ref[...] * (1.0 - lam_init)
        g = g_ref[0].astype(F32)
        o_ref[0] = (o * _silu(g)).astype(o_ref.dtype)

    @pl.when(safe)
    def _():
        attend(False)

    @pl.when(jnp.logical_not(safe))
    def _():
        attend(True)


def _attention(p1, lvec, subln, *, n_lat, heads, lam_init):
    bsz, r, _ = p1.shape
    hd = 2 * DA_HEAD
    tq = min(1024, n_lat)
    kern = functools.partial(_attn_kernel, lam_init=lam_init)
    return pl.pallas_call(
        kern,
        out_shape=jax.ShapeDtypeStruct((bsz, n_lat, heads * hd), BF16),
        grid=(bsz, heads, n_lat // tq),
        in_specs=[
            pl.BlockSpec((1, tq, hd), lambda b, h, i: (b, i, h)),
            pl.BlockSpec((1, r, hd), lambda b, h, i: (b, 0, heads + h)),
            pl.BlockSpec((1, r, hd), lambda b, h, i: (b, 0, 2 * heads + h)),
            pl.BlockSpec((1, tq, hd), lambda b, h, i: (b, i, 3 * heads + h)),
            pl.BlockSpec(lvec.shape, lambda b, h, i: (0, 0)),
            pl.BlockSpec((1, hd), lambda b, h, i: (0, 0)),
        ],
        out_specs=pl.BlockSpec((1, tq, hd), lambda b, h, i: (b, i, h)),
        scratch_shapes=[
            pltpu.VMEM((hd + 2 * SUBLANES, r), BF16),
            pltpu.VMEM((SUBLANES, LANES), F32),
            pltpu.VMEM((2, 2, ATTN_KEY_CHUNK, tq), F32),
        ],
        compiler_params=_cparams(("arbitrary", "arbitrary", "arbitrary")),
        name="diff_attention",
    )(p1, p1, p1, p1, lvec, subln)


def _outproj1_kernel(o_ref, x_ref, modv_ref, w_ref, out_ref, *, d):
    b = pl.program_id(0)
    acc = _dot(o_ref[0], w_ref[...])
    gate = modv_ref[pl.ds(b, 1), :][:, 2 * d:3 * d]
    out_ref[0] = x_ref[0] + gate * acc


def _outproj1(o, xc, modv, w):
    bsz, n, dv = o.shape
    d = xc.shape[2]
    tm = ROW_TILE
    return pl.pallas_call(
        functools.partial(_outproj1_kernel, d=d),
        out_shape=jax.ShapeDtypeStruct((bsz, n, d), F32),
        grid=(bsz, n // tm),
        in_specs=[
            pl.BlockSpec((1, tm, dv), lambda b, i: (b, i, 0)),
            pl.BlockSpec((1, tm, d), lambda b, i: (b, i, 0)),
            pl.BlockSpec(modv.shape, lambda b, i: (0, 0)),
            pl.BlockSpec(w.shape, lambda b, i: (0, 0)),
        ],
        out_specs=pl.BlockSpec((1, tm, d), lambda b, i: (b, i, 0)),
        compiler_params=_cparams(("arbitrary", "arbitrary")),
        name="outproj_odd",
    )(o, xc, modv, w)


def _rope_tables(n_lat, n_ctx):
    quarter = DA_HEAD // 4
    pos = np.arange(n_lat)
    freqs = ROPE_BASE ** (-np.arange(quarter) / quarter)
    ar = (pos // GRID_W)[:, None] * freqs[None, :]
    ac = (pos % GRID_W)[:, None] * freqs[None, :]
    cos = np.concatenate([np.cos(ar), np.cos(ar), np.cos(ac), np.cos(ac)], axis=-1)
    sin = np.concatenate([-np.sin(ar), np.sin(ar), -np.sin(ac), np.sin(ac)], axis=-1)
    cos = np.concatenate([cos, cos], axis=-1)
    sin = np.concatenate([sin, sin], axis=-1)
    cos = np.concatenate([cos, np.ones((n_ctx, 2 * DA_HEAD))], axis=0)
    sin = np.concatenate([sin, np.zeros((n_ctx, 2 * DA_HEAD))], axis=0)
    return jnp.asarray(cos, F32), jnp.asarray(sin, F32)


def kernel(x, c, ctx, c_ctx, mod_w, mod_b, norm_w, ev_in_w, ev_out_w, hy_conv_w, hy_conv_b, hy_w1, hy_b1, hy_w2, hy_b2, hy_w3, hy_b3, hy_freq, hy_skip, s5_a_re, s5_a_im, s5_log_dt, s5_b_re, s5_b_im, s5_c_re, s5_c_im, s5_d, s5_glu_w, s5_glu_b, od_in_w, od_out_w, da_q_norm, da_k_norm, da_lq1, da_lk1, da_lq2, da_lk2, da_subln):
    bsz, n, d = x.shape
    nc = ctx.shape[1]
    assert n % ROW_TILE == 0 and nc % ROW_TILE == 0
    depth = mod_w.shape[0]
    assert depth == 2

    npad = SUBLANES * ((bsz + 1 + SUBLANES - 1) // SUBLANES)
    cvec = jnp.concatenate([c, c_ctx[None, :], jnp.zeros((npad - bsz - 1, d), F32)], axis=0)
    modv = _mod_vectors(cvec, mod_w, mod_b)

    hw = hy_skip.shape[-1]
    sw = s5_d.shape[-1]
    p0 = _inproj0(x, ctx, modv[0], norm_w[0:1], ev_in_w[0].astype(BF16))
    sc = _shortconv(p0, hy_conv_w[0].astype(F32), hy_conv_b[0].astype(F32), n)
    filt = (hy_w1[0], hy_b1[0], hy_w2[0], hy_b2[0], hy_w3[0], hy_b3[0], hy_freq[0])
    hy_l = _hyena_seq(sc, n, 0, filt, hy_skip[0], hw)
    hy_c = _hyena_seq(sc, nc, n // nc, filt, hy_skip[0], hw)

    s5_ops = _s5_operators(s5_a_re[0], s5_a_im[0], s5_log_dt[0], s5_b_re[0], s5_b_im[0], s5_c_re[0], s5_c_im[0])
    su_blk = ((HY_ORDER + 2) * hw) // sw
    yssm = _s5_ssm(p0, *s5_ops, n_lat=n, n_ctx=nc, col_blk0=su_blk * (sw // LANES))
    s5g = _s5_glu(yssm, p0, s5_d[0], s5_glu_w[0], s5_glu_b[0], u_col_blk=su_blk, sg_col_blk=su_blk + 1)
    x1 = _outproj0(hy_l, hy_c, p0, s5g, x, ctx, modv[0], ev_out_w[0].astype(BF16),
                   hg_col_blk=(HY_ORDER + 1))

    heads = d // (2 * DA_HEAD)
    reps = d // DA_HEAD
    qscale = DA_HEAD ** -0.5 * math.log2(math.e)
    qkw = jnp.stack([jnp.tile(da_q_norm[0].astype(F32), reps) * qscale, jnp.tile(da_k_norm[0].astype(F32), reps)])
    qkw = jnp.concatenate([qkw, jnp.zeros((SUBLANES - 2, d), F32)], axis=0)
    gidx = np.arange(2 * LANES) // DA_HEAD
    gm = jnp.asarray(((gidx[:, None] == gidx[None, :]) * (1.0 / DA_HEAD)).astype(BF16))
    cos_t, sin_t = _rope_tables(n, nc)
    lidx = np.arange(2 * LANES)
    pm = jnp.asarray((lidx[:, None] == (lidx[None, :] ^ (DA_HEAD // 4))).astype(BF16))
    p1 = _inproj1(x1, modv[1], norm_w[1:2], od_in_w[0].astype(BF16), qkw, gm, pm, cos_t, sin_t, n)
    lam_init = 0.8 - 0.6 * math.exp(-0.3 * 1)
    lvec = jnp.stack([da_lq1[0], da_lk1[0], da_lq2[0], da_lk2[0]]).astype(F32)
    lvec = jnp.pad(lvec, ((0, SUBLANES - 4), (0, LANES - lvec.shape[1])))
    o = _attention(p1, lvec, da_subln[0].reshape(1, 2 * DA_HEAD).astype(F32), n_lat=n, heads=heads,
                   lam_init=lam_init)
    return _outproj1(o, x1, modv[1], od_out_w[0].astype(BF16))
```

```python
import functools
import math

import jax
import jax.numpy as jnp
import numpy as np
from jax import lax
from jax.experimental import pallas as pl
from jax.experimental.pallas import tpu as pltpu

F32 = jnp.float32
BF16 = jnp.bfloat16
HIGHEST = lax.Precision.HIGHEST

EPS = 1e-6
ROW_TILE = 256
LANES = 128
SUBLANES = 8
VMEM_LIMIT = 56 * 1024 * 1024

HY_ORDER = 2
HY_BANDS = 16
HY_TARGET = 1e-2
HY_FAST_PCT = 0.3
HY_SLOW_PCT = 1.5
S5_GROUP = 16
S5_STATE = 64
DA_HEAD = 64
GRID_W = 64
ROPE_BASE = 10000.0


def _cparams(sem):
    return pltpu.CompilerParams(dimension_semantics=sem, vmem_limit_bytes=VMEM_LIMIT)


def _silu(x):
    return x * jax.nn.sigmoid(x)


def _dot(a, b):
    return jnp.dot(a, b, preferred_element_type=F32)


def _mod_kernel(c_ref, w_ref, b_ref, o_ref):
    a = _silu(c_ref[...])
    o_ref[0] = jnp.dot(a, w_ref[0], precision=HIGHEST, preferred_element_type=F32) + b_ref[0]


def _mod_vectors(cvec, mod_w, mod_b):
    depth, d, d3 = mod_w.shape
    tn = 1024
    return pl.pallas_call(
        _mod_kernel,
        out_shape=jax.ShapeDtypeStruct((depth, cvec.shape[0], d3), F32),
        grid=(depth, d3 // tn),
        in_specs=[
            pl.BlockSpec(cvec.shape, lambda l, j: (0, 0)),
            pl.BlockSpec((1, d, tn), lambda l, j: (l, 0, j)),
            pl.BlockSpec((1, 1, tn), lambda l, j: (l, 0, j)),
        ],
        out_specs=pl.BlockSpec((1, cvec.shape[0], tn), lambda l, j: (l, 0, j)),
        compiler_params=_cparams(("arbitrary", "arbitrary")),
        name="mod_vectors",
    )(cvec, mod_w, mod_b.reshape(depth, 1, d3))


def _normed(x, modv_ref, nw_ref, row, d):
    m = modv_ref[pl.ds(row, 1), :]
    shift = m[:, 0:d]
    scale = m[:, d:2 * d]
    ms = jnp.mean(x * x, axis=-1, keepdims=True)
    y = x * lax.rsqrt(ms + EPS) * nw_ref[...]
    return (y * (1.0 + scale) + shift).astype(BF16)


PROJ_COLS = 2048


def _resident_spec(arr):
    zeros = (0,) * arr.ndim
    return pl.BlockSpec(arr.shape, lambda *_: zeros, pipeline_mode=pl.Buffered(1))


def _inproj0_kernel(x_ref, ctx_ref, modv_ref, nw_ref, w_ref, o_ref, *, nl_tiles, bsz, d):
    b = pl.program_id(0)
    i = pl.program_id(1)
    is_lat = i < nl_tiles
    x = jnp.where(is_lat, x_ref[0], ctx_ref[0])
    row = jnp.where(is_lat, b, bsz)
    h = _normed(x, modv_ref, nw_ref, row, d)
    for j in range(w_ref.shape[1] // PROJ_COLS):
        cols = slice(j * PROJ_COLS, (j + 1) * PROJ_COLS)
        o_ref[0, :, cols] = _dot(h, w_ref[:, cols]).astype(o_ref.dtype)


def _inproj0(x, ctx, modv, nw, w):
    bsz, n, d = x.shape
    nc = ctx.shape[1]
    tm = ROW_TILE
    nl_t, nc_t = n // tm, nc // tm
    nout = w.shape[1]
    kern = functools.partial(_inproj0_kernel, nl_tiles=nl_t, bsz=bsz, d=d)
    return pl.pallas_call(
        kern,
        out_shape=jax.ShapeDtypeStruct((bsz, n + nc, nout), BF16),
        grid=(bsz, nl_t + nc_t),
        in_specs=[
            pl.BlockSpec((1, tm, d), lambda b, i: (b, jnp.minimum(i, nl_t - 1), 0)),
            pl.BlockSpec((1, tm, d), lambda b, i: (b, jnp.maximum(i - nl_t, 0), 0)),
            pl.BlockSpec(modv.shape, lambda b, i: (0, 0)),
            pl.BlockSpec((1, d), lambda b, i: (0, 0)),
            _resident_spec(w),
        ],
        out_specs=pl.BlockSpec((1, tm, nout), lambda b, i: (b, i, 0)),
        compiler_params=_cparams(("arbitrary", "arbitrary")),
        name="inproj_even",
    )(x, ctx, modv, nw, w)


def _inproj1_kernel(x_ref, modv_ref, nw_ref, w_ref, qkw_ref, gm_ref, pm_ref, cos_ref, sin_ref, o_ref,
                    *, nl_tiles, bsz, d):
    b = pl.program_id(0)
    i = pl.program_id(1)
    is_lat = i < nl_tiles
    row = jnp.where(is_lat, b, bsz)
    h = _normed(x_ref[0], modv_ref, nw_ref, row, d)

    def plain(j):
        cols = slice(j * PROJ_COLS, (j + 1) * PROJ_COLS)
        o_ref[0, :, cols] = _dot(h, w_ref[:, cols]).astype(o_ref.dtype)

    def normed_rotary(j):
        cols = slice(j * PROJ_COLS, (j + 1) * PROJ_COLS)
        acc = _dot(h, w_ref[:, cols])
        sq = (acc * acc).astype(BF16)
        gm = gm_ref[...]
        width = gm.shape[0]
        tiles = range(acc.shape[1] // width)
        ms = jnp.concatenate([_dot(sq[:, t * width:(t + 1) * width], gm) for t in tiles], axis=1)
        yn = acc * lax.rsqrt(ms + EPS) * qkw_ref[j:j + 1, :]
        reps = acc.shape[1] // cos_ref.shape[1]
        cos = jnp.concatenate([cos_ref[...]] * reps, axis=1)
        sin = jnp.concatenate([sin_ref[...]] * reps, axis=1)
        ynb = yn.astype(BF16)
        pm = pm_ref[...]
        swapped = jnp.concatenate([_dot(ynb[:, t * width:(t + 1) * width], pm) for t in tiles], axis=1)
        o_ref[0, :, cols] = (yn * cos + swapped * sin).astype(o_ref.dtype)

    normed_rotary(1)
    plain(2)

    @pl.when(is_lat)
    def _():
        normed_rotary(0)
        plain(3)

    @pl.when(jnp.logical_not(is_lat))
    def _():
        zeros = jnp.zeros((o_ref.shape[1], PROJ_COLS), o_ref.dtype)
        o_ref[0, :, 0:PROJ_COLS] = zeros
        o_ref[0, :, 3 * PROJ_COLS:4 * PROJ_COLS] = zeros


def _inproj1(xc, modv, nw, w, qkw, gm, pm, cos_t, sin_t, n_lat):
    bsz, r, d = xc.shape
    tm = ROW_TILE
    nl_t = n_lat // tm
    nout = w.shape[1]
    assert nout == 4 * PROJ_COLS
    kern = functools.partial(_inproj1_kernel, nl_tiles=nl_t, bsz=bsz, d=d)
    return pl.pallas_call(
        kern,
        out_shape=jax.ShapeDtypeStruct((bsz, r, nout), BF16),
        grid=(bsz, r // tm),
        in_specs=[
            pl.BlockSpec((1, tm, d), lambda b, i: (b, i, 0)),
            pl.BlockSpec(modv.shape, lambda b, i: (0, 0)),
            pl.BlockSpec((1, d), lambda b, i: (0, 0)),
            _resident_spec(w),
            pl.BlockSpec(qkw.shape, lambda b, i: (0, 0)),
            pl.BlockSpec(gm.shape, lambda b, i: (0, 0)),
            pl.BlockSpec(pm.shape, lambda b, i: (0, 0)),
            pl.BlockSpec((tm, cos_t.shape[1]), lambda b, i: (i, 0)),
            pl.BlockSpec((tm, sin_t.shape[1]), lambda b, i: (i, 0)),
        ],
        out_specs=pl.BlockSpec((1, tm, nout), lambda b, i: (b, i, 0)),
        compiler_params=_cparams(("arbitrary", "arbitrary")),
        name="inproj_odd",
    )(xc, modv, nw, w, qkw, gm, pm, cos_t, sin_t)


def _shortconv_kernel(p_ref, w_ref, b_ref, o_ref, *, n_lat):
    p = p_ref[0].astype(F32)
    r = p.shape[0]
    row = lax.broadcasted_iota(jnp.int32, p.shape, 0)
    prev = pltpu.roll(p, 1, axis=0)
    nxt = pltpu.roll(p, r - 1, axis=0)
    prev = jnp.where((row == 0) | (row == n_lat), 0.0, prev)
    nxt = jnp.where((row == n_lat - 1) | (row == r - 1), 0.0, nxt)
    w = w_ref[...]
    o_ref[0] = (prev * w[0:1] + p * w[1:2] + nxt * w[2:3] + b_ref[...]).astype(o_ref.dtype)


def _shortconv(p0, conv_w, conv_b, n_lat):
    bsz, r, _ = p0.shape
    width = conv_w.shape[1]
    tc = 512
    return pl.pallas_call(
        functools.partial(_shortconv_kernel, n_lat=n_lat),
        out_shape=jax.ShapeDtypeStruct((bsz, r, width), BF16),
        grid=(bsz, width // tc),
        in_specs=[
            pl.BlockSpec((1, r, tc), lambda b, j: (b, 0, j)),
            pl.BlockSpec((3, tc), lambda b, j: (0, j)),
            pl.BlockSpec((1, tc), lambda b, j: (0, j)),
        ],
        out_specs=pl.BlockSpec((1, r, tc), lambda b, j: (b, 0, j)),
        compiler_params=_cparams(("arbitrary", "arbitrary")),
        name="hyena_shortconv",
    )(p0, conv_w, conv_b.reshape(1, width))


def _filter_kernel(feat_ref, w1_ref, b1_ref, w2_ref, b2_ref, w3_ref, b3_ref, freq_ref, delta_ref,
                   h_ref, l1_ref):
    ti = pl.program_id(0)
    feat = feat_ref[...]
    tt = feat.shape[0]
    width = delta_ref.shape[1]
    z1 = jnp.dot(feat, w1_ref[...], precision=HIGHEST, preferred_element_type=F32) + b1_ref[...]
    h1 = jnp.sin(freq_ref[0:1, :] * z1)
    z2 = jnp.dot(h1, w2_ref[...], precision=HIGHEST, preferred_element_type=F32) + b2_ref[...]
    h2 = jnp.sin(freq_ref[1:2, :] * z2)
    decay = jnp.exp(-feat[:, 0:1] * delta_ref[...])
    row = lax.broadcasted_iota(jnp.int32, decay.shape, 0) + ti * tt

    @pl.when(ti == 0)
    def _():
        l1_ref[...] = jnp.zeros(l1_ref.shape, l1_ref.dtype)

    for q in range(w3_ref.shape[1] // width):
        cols = slice(q * width, (q + 1) * width)
        z3 = jnp.dot(h2, w3_ref[:, cols], precision=HIGHEST, preferred_element_type=F32) + b3_ref[:, cols]
        hf = z3 * decay
        if q % 2 == 1:
            hf = jnp.where(row == 0, 0.0, hf)
        h_ref[:, cols] = hf.astype(h_ref.dtype)
        l1_ref[:, cols] += jnp.sum(jnp.abs(hf).reshape(tt // SUBLANES, SUBLANES, width), axis=0)


def _hyena_filter_taps(n, w1, b1, w2, b2, w3, b3, freq, width):
    t = np.arange(n, dtype=np.float64)
    bands = np.linspace(1e-4, HY_BANDS - 1, HY_BANDS)
    ang = (2.0 * math.pi / n) * t[:, None] * bands[None, :]
    feat = np.concatenate([(t / n)[:, None], np.cos(ang), -np.sin(ang)], axis=-1)
    emb = feat.shape[1]
    feat = jnp.asarray(np.pad(feat, ((0, 0), (0, LANES - emb))), F32)
    w1p = jnp.pad(w1.astype(F32), ((0, LANES - emb), (0, 0)))
    ffn = w1.shape[1]
    deltas = jnp.asarray(np.abs(np.linspace(math.log(HY_TARGET) / HY_SLOW_PCT, math.log(HY_TARGET) / HY_FAST_PCT,
                                            width)).reshape(1, width), F32)
    ncol = w3.shape[1]
    tt = min(512, n)
    full = lambda i: (0, 0)
    return pl.pallas_call(
        _filter_kernel,
        out_shape=(jax.ShapeDtypeStruct((n, ncol), BF16), jax.ShapeDtypeStruct((SUBLANES, ncol), F32)),
        grid=(n // tt,),
        in_specs=[
            pl.BlockSpec((tt, LANES), lambda i: (i, 0)),
            pl.BlockSpec((LANES, ffn), full),
            pl.BlockSpec((1, ffn), full),
            pl.BlockSpec((ffn, ffn), full),
            pl.BlockSpec((1, ffn), full),
            pl.BlockSpec((ffn, ncol), full),
            pl.BlockSpec((1, ncol), full),
            pl.BlockSpec((2, ffn), full),
            pl.BlockSpec((1, width), full),
        ],
        out_specs=(pl.BlockSpec((tt, ncol), lambda i: (i, 0)),
                   pl.BlockSpec((SUBLANES, ncol), full)),
        compiler_params=_cparams(("arbitrary",)),
        name="hyena_filter_taps",
    )(feat, w1p, b1.reshape(1, ffn).astype(F32), w2.astype(F32), b2.reshape(1, ffn).astype(F32),
      w3.astype(F32), b3.reshape(1, ncol).astype(F32), freq.astype(F32), deltas)


def _dft_tables(n):
    idx = np.arange(n)
    ang = ((2 * idx[:, None] + 1) * idx[None, :] % (4 * n)) * (math.pi / (2 * n))
    fc, fs = np.cos(ang), -np.sin(ang)
    return tuple(jnp.asarray(t.astype(BF16)) for t in (fc, fs, fc.T, fs.T))


def _dft_fwd_kernel(fc_ref, fs_ref, z_ref, o_re_ref, o_im_ref):
    z = z_ref[...].astype(BF16)
    o_re_ref[...] = _dot(fc_ref[...], z)
    o_im_ref[...] = _dot(fs_ref[...], z)


def _filter_spectrum(fc, fs, taps):
    n, ncol = taps.shape
    tf = min(512, n)
    tcn = 512
    return pl.pallas_call(
        _dft_fwd_kernel,
        out_shape=(jax.ShapeDtypeStruct((n, ncol), F32), jax.ShapeDtypeStruct((n, ncol), F32)),
        grid=(ncol // tcn, n // tf),
        in_specs=[
            pl.BlockSpec((tf, n), lambda j, f: (f, 0)),
            pl.BlockSpec((tf, n), lambda j, f: (f, 0)),
            pl.BlockSpec((n, tcn), lambda j, f: (0, j)),
        ],
        out_specs=(pl.BlockSpec((tf, tcn), lambda j, f: (f, j)),
                   pl.BlockSpec((tf, tcn), lambda j, f: (f, j))),
        compiler_params=_cparams(("arbitrary", "arbitrary")),
        name="hyena_filter_spectrum",
    )(fc, fs, taps)


def _dft_fwd_mul_kernel(fc_ref, fs_ref, z_ref, kr_ref, ki_ref, pr_ref, pi_ref):
    z = z_ref[0]
    ur = _dot(fc_ref[...], z)
    ui = _dot(fs_ref[...], z)
    kr = kr_ref[...]
    ki = ki_ref[...]
    pr_ref[0] = (ur * kr - ui * ki).astype(pr_ref.dtype)
    pi_ref[0] = (ur * ki + ui * kr).astype(pi_ref.dtype)


def _dft_fwd_mul(fc, fs, z, kr, ki, *, n, row_blk, col_blk0):
    bsz = z.shape[0]
    width = kr.shape[1]
    tf = min(512, n)
    tcn = 512
    return pl.pallas_call(
        _dft_fwd_mul_kernel,
        out_shape=(jax.ShapeDtypeStruct((bsz, n, width), BF16), jax.ShapeDtypeStruct((bsz, n, width), BF16)),
        grid=(width // tcn, n // tf, bsz),
        in_specs=[
            pl.BlockSpec((tf, n), lambda j, f, b: (f, 0)),
            pl.BlockSpec((tf, n), lambda j, f, b: (f, 0)),
            pl.BlockSpec((1, n, tcn), lambda j, f, b: (b, row_blk, col_blk0 + j)),
            pl.BlockSpec((tf, tcn), lambda j, f, b: (f, j)),
            pl.BlockSpec((tf, tcn), lambda j, f, b: (f, j)),
        ],
        out_specs=(pl.BlockSpec((1, tf, tcn), lambda j, f, b: (b, f, j)),
                   pl.BlockSpec((1, tf, tcn), lambda j, f, b: (b, f, j))),
        compiler_params=_cparams(("arbitrary", "arbitrary", "arbitrary")),
        name="hyena_dft_fwd",
    )(fc, fs, z, kr, ki)


def _dft_inv_gate_kernel(tc_ref, ts_ref, pr_ref, pi_ref, z_ref, xg_ref, skip_ref, o_ref):
    y = _dot(tc_ref[...], pr_ref[0]) + _dot(ts_ref[...], pi_ref[0])
    z = z_ref[0].astype(F32)
    xg = xg_ref[0].astype(F32)
    o_ref[0] = (xg * (y + z * skip_ref[...])).astype(o_ref.dtype)


def _dft_inv_gate(tc, ts, pr, pi, z, xg, skip, *, n, z_row_blk, z_col_blk0, xg_row_blk, xg_col_blk0):
    bsz, _, width = pr.shape
    tt = min(512, n)
    tcn = 512
    zrb = z_row_blk * (n // tt)
    xrb = xg_row_blk * (n // tt)
    return pl.pallas_call(
        _dft_inv_gate_kernel,
        out_shape=jax.ShapeDtypeStruct((bsz, n, width), BF16),
        grid=(width // tcn, bsz, n // tt),
        in_specs=[
            pl.BlockSpec((tt, n), lambda j, b, i: (i, 0)),
            pl.BlockSpec((tt, n), lambda j, b, i: (i, 0)),
            pl.BlockSpec((1, n, tcn), lambda j, b, i: (b, 0, j)),
            pl.BlockSpec((1, n, tcn), lambda j, b, i: (b, 0, j)),
            pl.BlockSpec((1, tt, tcn), lambda j, b, i: (b, zrb + i, z_col_blk0 + j)),
            pl.BlockSpec((1, tt, tcn), lambda j, b, i: (b, xrb + i, xg_col_blk0 + j)),
            pl.BlockSpec((1, tcn), lambda j, b, i: (0, j)),
        ],
        out_specs=pl.BlockSpec((1, tt, tcn), lambda j, b, i: (b, i, j)),
        compiler_params=_cparams(("arbitrary", "arbitrary", "arbitrary")),
        name="hyena_dft_inv",
    )(tc, ts, pr, pi, z, xg, skip)


FFT_N2 = 256
FFT_GROUP = 16


def _fft_tables(n):
    n1 = (2 * n) // FFT_N2
    h1 = n1 // 2
    nf2 = n // n1
    f1 = np.arange(n1)
    t1 = np.arange(h1)
    th = ((2 * f1[:, None] + 1) * t1[None, :] % (2 * n1)) * (math.pi / n1)
    base = np.stack([np.cos(th), -np.sin(th)])
    m1 = np.einsum('pft,jk->pfjtk', base, np.eye(FFT_GROUP)).reshape(2 * n1 * FFT_GROUP, h1 * FFT_GROUP)
    f = f1[:, None, None] + n1 * np.arange(nf2)[None, :, None]
    t2 = np.arange(FFT_N2)[None, None, :]
    psi = ((2 * f + 1) * t2 % (4 * n)) * (math.pi / (2 * n))
    cs = np.concatenate([np.cos(psi), np.sin(psi)], axis=1)
    tables = (m1, m1.T, cs, np.swapaxes(cs, 1, 2))
    return tuple(jnp.asarray(np.ascontiguousarray(t).astype(BF16)) for t in tables)


def _fft_stage1(z_ref, m1_ref, a_ref, n1):
    h1 = n1 // 2

    def body(g, carry):
        r0 = pl.multiple_of(g * FFT_GROUP, FFT_GROUP)
        xg = jnp.concatenate(
            [z_ref[pl.ds(pl.multiple_of(FFT_N2 * t1 + r0, FFT_GROUP), FFT_GROUP), :] for t1 in range(h1)], axis=0)
        out = _dot(m1_ref[...], xg).astype(BF16)
        for part in range(2):
            for f1 in range(n1):
                row = (part * n1 + f1) * FFT_GROUP
                a_ref[part, f1, pl.ds(r0, FFT_GROUP), :] = out[row:row + FFT_GROUP]
        return carry

    lax.fori_loop(0, FFT_N2 // FFT_GROUP, body, 0, unroll=4)


def _fft_stage2(a_ref, cs_ref, f1):
    rhs = jnp.concatenate([a_ref[0, f1], a_ref[1, f1]], axis=1)
    r = _dot(cs_ref[f1], rhs)
    nf2 = r.shape[0] // 2
    w = r.shape[1] // 2
    xr = r[0:nf2, 0:w] + r[nf2:, w:]
    xi = r[0:nf2, w:] - r[nf2:, 0:w]
    return xr, xi


def _fft_filter_kernel(hf_ref, hb_ref, l1f_ref, l1b_ref, m1_ref, cs_ref, o_ref, sd_ref, a_ref, *, n1, n):
    hf = hf_ref[...].astype(F32)
    hb = hb_ref[...].astype(F32)
    sd_ref[0] = (hf + hb).astype(BF16)
    sd_ref[1] = (hf - hb).astype(BF16)
    _fft_stage1(sd_ref.at[0], m1_ref, a_ref.at[0], n1)
    _fft_stage1(sd_ref.at[1], m1_ref, a_ref.at[1], n1)
    l1 = jnp.sum(l1f_ref[...], axis=0, keepdims=True) + jnp.sum(l1b_ref[...], axis=0, keepdims=True)
    norm = (1.0 / n) / l1

    def body(f1, carry):
        cs = cs_ref[f1]
        nf2 = cs.shape[0] // 2
        c, s = cs[0:nf2], cs[nf2:]
        kr = _dot(c, a_ref[0, 0, f1]) + _dot(s, a_ref[0, 1, f1])
        ki = _dot(c, a_ref[1, 1, f1]) - _dot(s, a_ref[1, 0, f1])
        rows = pl.ds(pl.multiple_of(f1 * nf2, nf2), nf2)
        o_ref[0, 0, rows, :] = (kr * norm).astype(o_ref.dtype)
        o_ref[0, 1, rows, :] = (ki * norm).astype(o_ref.dtype)
        return carry

    lax.fori_loop(0, n1, body, 0, unroll=min(4, n1))


def _const_spec(arr):
    nd = arr.ndim
    return pl.BlockSpec(arr.shape, lambda *_: (0,) * nd)


def _fft_filter_spectrum(taps, l1, m1, cs, width):
    n, _ = taps.shape
    n1 = cs.shape[0]
    tcn = 256
    cb = width // tcn
    return pl.pallas_call(
        functools.partial(_fft_filter_kernel, n1=n1, n=n),
        out_shape=jax.ShapeDtypeStruct((HY_ORDER, 2, n, width), BF16),
        grid=(HY_ORDER, cb),
        in_specs=[
            pl.BlockSpec((n, tcn), lambda i, j: (0, (2 * i) * cb + j)),
            pl.BlockSpec((n, tcn), lambda i, j: (0, (2 * i + 1) * cb + j)),
            pl.BlockSpec((SUBLANES, tcn), lambda i, j: (0, (2 * i) * cb + j)),
            pl.BlockSpec((SUBLANES, tcn), lambda i, j: (0, (2 * i + 1) * cb + j)),
            _const_spec(m1), _const_spec(cs),
        ],
        out_specs=pl.BlockSpec((1, 2, n, tcn), lambda i, j: (i, 0, 0, j)),
        scratch_shapes=[pltpu.VMEM((2, n, tcn), BF16), pltpu.VMEM((2, 2, n1, FFT_N2, tcn), BF16)],
        compiler_params=_cparams(("arbitrary", "arbitrary")),
        name="hyena_fft_filter_spectrum",
    )(taps, taps, l1, l1, m1, cs)


def _fft_conv_kernel(z_ref, xg_ref, k_ref, skip_ref, m1_ref, m1t_ref, cs_ref, cst_ref, o_ref, a_ref, *, n1):
    h1 = n1 // 2
    _fft_stage1(z_ref.at[0], m1_ref, a_ref, n1)

    def mid(f1, carry):
        xr, xi = _fft_stage2(a_ref, cs_ref, f1)
        nf2 = xr.shape[0]
        rows = pl.ds(pl.multiple_of(f1 * nf2, nf2), nf2)
        kr = k_ref[0, 0, rows, :].astype(F32)
        ki = k_ref[0, 1, rows, :].astype(F32)
        pr = xr * kr - xi * ki
        pi = xr * ki + xi * kr
        rhs = jnp.concatenate([jnp.concatenate([pr, pi], axis=1), jnp.concatenate([-pi, pr], axis=1)],
                              axis=0).astype(BF16)
        d = _dot(cst_ref[f1], rhs).astype(BF16)
        w = d.shape[1] // 2
        a_ref[0, f1] = d[:, 0:w]
        a_ref[1, f1] = d[:, w:]
        return carry

    lax.fori_loop(0, n1, mid, 0, unroll=min(8, n1))

    skip = skip_ref[...]

    def last(g, carry):
        r0 = pl.multiple_of(g * FFT_GROUP, FFT_GROUP)
        dg = jnp.concatenate(
            [a_ref[part, f1, pl.ds(r0, FFT_GROUP), :] for part in range(2) for f1 in range(n1)], axis=0)
        yg = _dot(m1t_ref[...], dg)
        for t1 in range(h1):
            rows = pl.ds(pl.multiple_of(FFT_N2 * t1 + r0, FFT_GROUP), FFT_GROUP)
            z = z_ref[0, rows, :].astype(F32)
            xg = xg_ref[0, rows, :].astype(F32)
            o_ref[0, rows, :] = (xg * (yg[t1 * FFT_GROUP:(t1 + 1) * FFT_GROUP] + z * skip)).astype(o_ref.dtype)
        return carry

    lax.fori_loop(0, FFT_N2 // FFT_GROUP, last, 0, unroll=4)


def _fft_conv_gate(z, xg, kspec, order, skip, tables, *, n, z_row_blk, z_col_blk0, xg_row_blk, xg_col_blk0):
    m1, m1t, cs, cst = tables
    bsz = z.shape[0]
    width = kspec.shape[3]
    n1 = cs.shape[0]
    tcn = 256
    return pl.pallas_call(
        functools.partial(_fft_conv_kernel, n1=n1),
        out_shape=jax.ShapeDtypeStruct((bsz, n, width), BF16),
        grid=(width // tcn, bsz),
        in_specs=[
            pl.BlockSpec((1, n, tcn), lambda j, b: (b, z_row_blk, z_col_blk0 + j)),
            pl.BlockSpec((1, n, tcn), lambda j, b: (b, xg_row_blk, xg_col_blk0 + j)),
            pl.BlockSpec((1, 2, n, tcn), lambda j, b: (order, 0, 0, j)),
            pl.BlockSpec((1, tcn), lambda j, b: (0, j)),
            _const_spec(m1), _const_spec(m1t), _const_spec(cs), _const_spec(cst),
        ],
        out_specs=pl.BlockSpec((1, n, tcn), lambda j, b: (b, 0, j)),
        scratch_shapes=[pltpu.VMEM((2, n1, FFT_N2, tcn), BF16)],
        compiler_params=_cparams(("arbitrary", "arbitrary")),
        name="hyena_fft_conv",
    )(z, xg, kspec, skip, m1, m1t, cs, cst)


def _hyena_seq_fft(sc, n, seq_row_blk, filt, skip, width):
    w1, b1, w2, b2, w3, b3, freq = filt
    taps, l1 = _hyena_filter_taps(n, w1, b1, w2, b2, w3, b3, freq, width)
    tables = _fft_tables(n)
    kspec = _fft_filter_spectrum(taps, l1, tables[0], tables[2], width)
    cb = width // 256
    y = None
    for i in range(HY_ORDER):
        if i == 0:
            z, zrb, zcb = sc, seq_row_blk, 0
        else:
            z, zrb, zcb = y, 0, 0
        y = _fft_conv_gate(z, sc, kspec, i, skip[i].reshape(1, width).astype(F32), tables, n=n,
                           z_row_blk=zrb, z_col_blk0=zcb, xg_row_blk=seq_row_blk, xg_col_blk0=(i + 1) * cb)
    return y


def _hyena_seq(sc, n, seq_row_blk, filt, skip, width):
    if n % (2 * FFT_N2) == 0:
        return _hyena_seq_fft(sc, n, seq_row_blk, filt, skip, width)
    w1, b1, w2, b2, w3, b3, freq = filt
    taps, l1 = _hyena_filter_taps(n, w1, b1, w2, b2, w3, b3, freq, width)
    fc, fs, tc, ts = _dft_tables(n)
    sr, si = _filter_spectrum(fc, fs, taps)
    sr = sr.reshape(n, HY_ORDER, 2, width)
    si = si.reshape(n, HY_ORDER, 2, width)
    l1 = jnp.sum(l1, axis=0).reshape(HY_ORDER, 2, width).sum(axis=1)
    norm = (1.0 / n) / l1
    cb = width // 512
    y = None
    for i in range(HY_ORDER):
        kr = (sr[:, i, 0] + sr[:, i, 1]) * norm[i]
        ki = (si[:, i, 0] - si[:, i, 1]) * norm[i]
        if i == 0:
            z, zrb, zcb = sc, seq_row_blk, 0
        else:
            z, zrb, zcb = y, 0, 0
        pr, pi = _dft_fwd_mul(fc, fs, z, kr, ki, n=n, row_blk=zrb, col_blk0=zcb)
        y = _dft_inv_gate(tc, ts, pr, pi, z, sc, skip[i].reshape(1, width).astype(F32), n=n,
                          z_row_blk=zrb, z_col_blk0=zcb, xg_row_blk=seq_row_blk, xg_col_blk0=(i + 1) * cb)
    return y


S5_CHUNK = 16
S5_BLOCK_GROUPS = 8


def _piece_transpose(arrs):
    arrs = list(arrs)
    lane = lax.broadcasted_iota(jnp.int32, arrs[0].shape, arrs[0].ndim - 1)
    piece = lane // S5_GROUP
    axis = arrs[0].ndim - 1
    for s in (4, 2, 1):
        low = (piece & s) == 0
        nxt = list(arrs)
        for r in range(len(arrs)):
            if r & s:
                continue
            lo, hi = arrs[r], arrs[r + s]
            nxt[r] = jnp.where(low, lo, pltpu.roll(hi, s * S5_GROUP, axis=axis))
            nxt[r + s] = jnp.where(low, pltpu.roll(lo, LANES - s * S5_GROUP, axis=axis), hi)
        arrs = nxt
    return arrs


def _s5_toeplitz(kf_ref, kr_ref, t_ref):
    cin = kf_ref.shape[2]
    gw = kf_ref.shape[3]
    steps = gw // cin
    lane = lax.broadcasted_iota(jnp.int32, (cin, gw), 1)
    for g in range(kf_ref.shape[1]):
        kf = kf_ref[0, g]
        kr = kr_ref[0, g]
        for i in range(steps):
            right = cin * i
            left = cin * (steps - 1 - i)
            tf = kf if right == 0 else jnp.where(lane >= right, pltpu.roll(kf, right, axis=1), 0.0)
            tr = kr if left == 0 else jnp.where(lane < gw - left, pltpu.roll(kr, gw - left, axis=1), 0.0)
            t_ref[g, cin * i:cin * (i + 1), :] = (tf + tr).astype(t_ref.dtype)


def _s5_kernel(u_ref, kf_ref, kr_ref, wb_ref, v_ref, l16_ref, y_ref, uf_ref, x_ref, yi_ref, d_ref, t_ref,
               *, nl, nc):
    @pl.when(pl.program_id(1) == 0)
    def _():
        _s5_toeplitz(kf_ref, kr_ref, t_ref)

    nch = nl + nc
    ng = S5_BLOCK_GROUPS
    npair = ng // 2
    pitch = nch + 4
    half = LANES

    uf_ref[...] = u_ref[0].astype(F32)
    for h in range(2):
        steps = [uf_ref[pl.ds(h * ng + j, nch, stride=S5_CHUNK), :] for j in range(ng)]
        for gl, xg in enumerate(_piece_transpose(steps)):
            x_ref[gl, :, h * half:(h + 1) * half] = xg.astype(BF16)

    for q in range(ng // 2):
        g0, g1 = 2 * q, 2 * q + 1
        x0, x1 = x_ref[g0], x_ref[g1]
        yi_ref[g0] = _dot(x0, t_ref[g0])
        yi_ref[g1] = _dot(x1, t_ref[g1])
        dp = _dot(jnp.concatenate([x0, x1], axis=1), wb_ref[0, q])
        for k in range(4):
            d_ref[k, q * pitch:q * pitch + nch, :] = dp[:, k * LANES:(k + 1) * LANES]

    lrf, lif, lrr, lir = l16_ref[0, 0], l16_ref[0, 1], l16_ref[0, 2], l16_ref[0, 3]

    def step(s, carry):
        srf, sif, srr, sir = carry
        rows_f = pl.ds(jnp.where(s < nc, nl + s, s - nc), npair, stride=pitch)
        rows_r = pl.ds(nch - 1 - s, npair, stride=pitch)
        xrf = d_ref[0, rows_f, :]
        xif = d_ref[1, rows_f, :]
        xrr = d_ref[2, rows_r, :]
        xir = d_ref[3, rows_r, :]
        d_ref[0, rows_f, :] = srf
        d_ref[1, rows_f, :] = sif
        d_ref[2, rows_r, :] = srr
        d_ref[3, rows_r, :] = sir
        return (lrf * srf - lif * sif + xrf, lrf * sif + lif * srf + xif,
                lrr * srr - lir * sir + xrr, lrr * sir + lir * srr + xir)

    zero = jnp.zeros((npair, LANES), F32)
    lax.fori_loop(0, nch, step, (zero, zero, zero, zero), unroll=2)

    for q in range(ng // 2):
        rows = slice(q * pitch, q * pitch + nch)
        sf = jnp.concatenate([d_ref[0, rows, :], d_ref[1, rows, :]], axis=1).astype(BF16)
        sr = jnp.concatenate([d_ref[2, rows, :], d_ref[3, rows, :]], axis=1).astype(BF16)
        yq = _dot(sf, v_ref[0, 0, q]) + _dot(sr, v_ref[0, 1, q])
        w = yq.shape[1] // 2
        yi_ref[2 * q] += yq[:, 0:w]
        yi_ref[2 * q + 1] += yq[:, w:]
    for h in range(2):
        groups = [yi_ref[gl, :, h * half:(h + 1) * half] for gl in range(ng)]
        for j, yj in enumerate(_piece_transpose(groups)):
            uf_ref[pl.ds(h * ng + j, nch, stride=S5_CHUNK), :] = yj
    y_ref[0] = uf_ref[...].astype(y_ref.dtype)


def _s5_ssm(p0, kf, kr, wbp, vp, l16, *, n_lat, n_ctx, col_blk0):
    bsz, r, _ = p0.shape
    nblk = kf.shape[0]
    nl, nc = n_lat // S5_CHUNK, n_ctx // S5_CHUNK
    nch = nl + nc
    ng = S5_BLOCK_GROUPS
    gw = S5_CHUNK * S5_GROUP
    return pl.pallas_call(
        functools.partial(_s5_kernel, nl=nl, nc=nc),
        out_shape=jax.ShapeDtypeStruct((bsz, r, nblk * LANES), BF16),
        grid=(nblk, bsz),
        in_specs=[
            pl.BlockSpec((1, r, LANES), lambda k, b: (b, 0, col_blk0 + k)),
            pl.BlockSpec((1,) + kf.shape[1:], lambda k, b: (k, 0, 0, 0)),
            pl.BlockSpec((1,) + kr.shape[1:], lambda k, b: (k, 0, 0, 0)),
            pl.BlockSpec((1,) + wbp.shape[1:], lambda k, b: (k, 0, 0, 0)),
            pl.BlockSpec((1,) + vp.shape[1:], lambda k, b: (k, 0, 0, 0, 0)),
            pl.BlockSpec((1,) + l16.shape[1:], lambda k, b: (k, 0, 0, 0)),
        ],
        out_specs=pl.BlockSpec((1, r, LANES), lambda k, b: (b, 0, k)),
        scratch_shapes=[
            pltpu.VMEM((r, LANES), F32),
            pltpu.VMEM((ng, nch, gw), BF16),
            pltpu.VMEM((ng, nch, gw), F32),
            pltpu.VMEM((4, (ng // 2) * (nch + 4), LANES), F32),
            pltpu.VMEM((ng, gw, gw), BF16),
        ],
        compiler_params=_cparams(("arbitrary", "arbitrary")),
        name="s5_ssm",
    )(p0, kf, kr, wbp, vp, l16)


def _s5_operators(a_re, a_im, log_dt, b_re, b_im, c_re, c_im):
    f32 = F32
    ndir, g, p = a_re.shape
    cin = b_re.shape[-1]
    L = S5_CHUNK
    gw = L * cin
    ar, ai = a_re.astype(f32), a_im.astype(f32)
    dt = jnp.exp(log_dt.astype(f32))[..., None]
    ls = jnp.arange(L + 1, dtype=f32)[:, None, None, None]
    mag = jnp.exp(ls * (ar * dt)[None])
    pr = mag * jnp.cos(ls * (ai * dt)[None])
    pi = mag * jnp.sin(ls * (ai * dt)[None])
    lr, li = pr[1], pi[1]
    den = ar * ar + ai * ai
    zr = ((lr - 1.0) * ar + li * ai) / den
    zi = (li * ar - (lr - 1.0) * ai) / den
    btr = jnp.swapaxes(b_re.astype(f32), -1, -2)
    bti = jnp.swapaxes(b_im.astype(f32), -1, -2)
    bbr = zr[:, :, None, :] * btr - zi[:, :, None, :] * bti
    bbi = zr[:, :, None, :] * bti + zi[:, :, None, :] * btr
    cr, ci = c_re.astype(f32), c_im.astype(f32)

    lbr = pr[:L, :, :, None, :] * bbr[None] - pi[:L, :, :, None, :] * bbi[None]
    lbi = pr[:L, :, :, None, :] * bbi[None] + pi[:L, :, :, None, :] * bbr[None]

    krow = (jnp.einsum('dgkp,ldgcp->dgclk', cr, lbr)
            - jnp.einsum('dgkp,ldgcp->dgclk', ci, lbi)).reshape(ndir, g, cin, gw)
    krev = jnp.flip(krow[1].reshape(g, cin, L, cin), axis=2).reshape(g, cin, gw)

    def chunk_rows(m):
        return jnp.swapaxes(m, 0, 1).reshape(g, gw, p)

    wf_r, wf_i = chunk_rows(lbr[::-1, 0]), chunk_rows(lbi[::-1, 0])
    wr_r, wr_i = chunk_rows(lbr[:, 1]), chunk_rows(lbi[:, 1])

    def state_out(d, er, ei):
        cxr = jnp.tile(jnp.swapaxes(cr[d], -1, -2), (1, 1, L))
        cxi = jnp.tile(jnp.swapaxes(ci[d], -1, -2), (1, 1, L))
        pxr = jnp.repeat(jnp.transpose(er, (1, 2, 0)), cin, axis=-1)
        pxi = jnp.repeat(jnp.transpose(ei, (1, 2, 0)), cin, axis=-1)
        return cxr * pxr - cxi * pxi, -(cxr * pxi + cxi * pxr)

    vf_re, vf_im = state_out(0, pr[1:L + 1, 0], pi[1:L + 1, 0])
    vr_re, vr_im = state_out(1, pr[1:L + 1, 1][::-1], pi[1:L + 1, 1][::-1])

    nblk = g // S5_BLOCK_GROUPS
    npair = S5_BLOCK_GROUPS // 2

    def pair_cols(m):
        m = m.reshape(nblk, npair, 2, m.shape[1], m.shape[2])
        z = jnp.zeros_like(m[:, :, 0])
        top = jnp.concatenate([m[:, :, 0], z], axis=-1)
        bot = jnp.concatenate([z, m[:, :, 1]], axis=-1)
        return jnp.concatenate([top, bot], axis=-2)

    wbp = jnp.concatenate([pair_cols(wf_r), pair_cols(wf_i), pair_cols(wr_r), pair_cols(wr_i)], axis=-1)

    vp = jnp.stack([jnp.concatenate([pair_cols(vf_re), pair_cols(vf_im)], axis=-2),
                    jnp.concatenate([pair_cols(vr_re), pair_cols(vr_im)], axis=-2)], axis=1)

    def rows(v):
        return v.reshape(nblk, npair, 2 * p)

    l16 = jnp.stack([rows(pr[L, 0]), rows(pi[L, 0]), rows(pr[L, 1]), rows(pi[L, 1])], axis=1)
    kf = krow[0].reshape(nblk, S5_BLOCK_GROUPS, cin, gw)
    kr = krev.reshape(nblk, S5_BLOCK_GROUPS, cin, gw)
    return kf, kr, wbp.astype(BF16), vp.astype(BF16), l16


def _outproj0_kernel(hyl_ref, hyc_ref, hg_ref, ys_ref, u_ref, sg_ref, sd_ref, gw_ref, gb_ref, x_ref, ctx_ref,
                     modv_ref, w_ref, o_ref, *, nl_tiles, bsz, d, hw):
    b = pl.program_id(0)
    i = pl.program_id(1)
    is_lat = i < nl_tiles
    hy = jnp.where(is_lat, hyl_ref[0], hyc_ref[0]).astype(F32)
    a = (hy * _silu(hg_ref[0].astype(F32))).astype(BF16)
    y = u_ref[0].astype(F32) * sd_ref[...] + ys_ref[0].astype(F32)
    g = 0.5 * y * (1.0 + jnp.tanh(math.sqrt(2.0 / math.pi) * (y + 0.044715 * (y * y * y))))
    z = _dot(g.astype(BF16), gw_ref[...]) + gb_ref[...]
    s5 = (g * jax.nn.sigmoid(z) * _silu(sg_ref[0].astype(F32))).astype(BF16)
    acc = _dot(a, w_ref[0:hw, :]) + _dot(s5, w_ref[hw:, :])
    row = jnp.where(is_lat, b, bsz)
    gate = modv_ref[pl.ds(row, 1), :][:, 2 * d:3 * d]
    xin = jnp.where(is_lat, x_ref[0], ctx_ref[0])
    o_ref[0] = xin + gate * acc


def _outproj0(hy_l, hy_c, p0, yssm, s5_d, glu_w, glu_b, x, ctx, modv, w, *, hg_col_blk, u_col_blk, sg_col_blk):
    bsz, n, d = x.shape
    nc = ctx.shape[1]
    tm = ROW_TILE
    nl_t, nc_t = n // tm, nc // tm
    hw = hy_l.shape[2]
    sw = yssm.shape[2]
    kern = functools.partial(_outproj0_kernel, nl_tiles=nl_t, bsz=bsz, d=d, hw=hw)
    lat = lambda b, i: (b, jnp.minimum(i, nl_t - 1), 0)
    cx = lambda b, i: (b, jnp.maximum(i - nl_t, 0), 0)
    return pl.pallas_call(
        kern,
        out_shape=jax.ShapeDtypeStruct((bsz, n + nc, d), F32),
        grid=(bsz, nl_t + nc_t),
        in_specs=[
            pl.BlockSpec((1, tm, hw), lat),
            pl.BlockSpec((1, tm, hw), cx),
            pl.BlockSpec((1, tm, hw), lambda b, i: (b, i, hg_col_blk)),
            pl.BlockSpec((1, tm, sw), lambda b, i: (b, i, 0)),
            pl.BlockSpec((1, tm, sw), lambda b, i: (b, i, u_col_blk)),
            pl.BlockSpec((1, tm, sw), lambda b, i: (b, i, sg_col_blk)),
            pl.BlockSpec((1, sw), lambda b, i: (0, 0)),
            _resident_spec(glu_w),
            pl.BlockSpec((1, sw), lambda b, i: (0, 0)),
            pl.BlockSpec((1, tm, d), lat),
            pl.BlockSpec((1, tm, d), cx),
            pl.BlockSpec(modv.shape, lambda b, i: (0, 0)),
            _resident_spec(w),
        ],
        out_specs=pl.BlockSpec((1, tm, d), lambda b, i: (b, i, 0)),
        compiler_params=_cparams(("arbitrary", "arbitrary")),
        name="outproj_even",
    )(hy_l, hy_c, p0, yssm, p0, p0, s5_d.reshape(1, sw).astype(F32), glu_w.astype(BF16),
      glu_b.reshape(1, sw).astype(F32), x, ctx, modv, w)


ATTN_SAFE_LOG2 = 57.0
ATTN_KEY_CHUNK = 256


def _attn_prepare(k_ref, v_ref, vt_ref, kn_ref):
    hd = v_ref.shape[2]
    vt_ref[0:hd, :] = v_ref[0].astype(F32).T.astype(BF16)
    vt_ref[hd:, :] = jnp.ones((vt_ref.shape[0] - hd, vt_ref.shape[1]), BF16)
    kn_ref[0:1, :] = _max_subhead_sqnorm(k_ref[0])


def _max_subhead_sqnorm(x):
    hd = x.shape[1]
    sel = ((lax.broadcasted_iota(jnp.int32, (hd, hd), 0) // DA_HEAD)
           == lax.broadcasted_iota(jnp.int32, (hd, hd), 1)).astype(BF16)
    return jnp.max(_dot(x * x, sel), axis=0, keepdims=True)


def _attn_kernel(q_ref, k_ref, v_ref, g_ref, lvec_ref, subln_ref, o_ref, vt_ref, kn_ref, st_ref, *, lam_init):
    @pl.when(pl.program_id(2) == 0)
    def _():
        _attn_prepare(k_ref, v_ref, vt_ref, kn_ref)

    q = q_ref[0]
    hd = q.shape[1]
    lane = lax.broadcasted_iota(jnp.int32, q.shape, 1)
    zero = jnp.zeros_like(q)
    qs = (jnp.where(lane < DA_HEAD, q, zero), jnp.where(lane >= DA_HEAD, q, zero))
    lv = lvec_ref[...]
    lam = (jnp.exp(jnp.sum(lv[0:1] * lv[1:2], axis=-1, keepdims=True))
           - jnp.exp(jnp.sum(lv[2:3] * lv[3:4], axis=-1, keepdims=True)) + lam_init)

    bound_sq = _max_subhead_sqnorm(q) * kn_ref[0:1, :]
    safe = jnp.max(bound_sq) <= ATTN_SAFE_LOG2 * ATTN_SAFE_LOG2

    nkc = k_ref.shape[1] // ATTN_KEY_CHUNK

    def scores_t(c, m):
        kc = k_ref[0, c * ATTN_KEY_CHUNK:(c + 1) * ATTN_KEY_CHUNK, :]
        return lax.dot_general(kc, qs[m], (((1,), (1,)), ((), ())), preferred_element_type=F32)

    def attend(subtract_max):
        shifts = [None, None]
        if subtract_max:
            for m in range(2):
                for c in range(nkc):
                    cm = jnp.max(scores_t(c, m), axis=0, keepdims=True)
                    shifts[m] = cm if shifts[m] is None else jnp.maximum(shifts[m], cm)
        accs = [None, None]

        def stage(c):
            for m in range(2):
                st_ref[c % 2, m] = scores_t(c, m)

        stage(0)
        for c in range(nkc):
            if c + 1 < nkc:
                stage(c + 1)
            for m in range(2):
                st = st_ref[c % 2, m]
                if subtract_max:
                    st = st - shifts[m]
                p = jnp.exp2(st).astype(BF16)
                part = _dot(vt_ref[:, c * ATTN_KEY_CHUNK:(c + 1) * ATTN_KEY_CHUNK], p)
                accs[m] = part if accs[m] is None else accs[m] + part
        outs = [acc[0:hd] * (1.0 / acc[hd:hd + 1]) for acc in accs]
        o = (outs[0] - lam * outs[1]).T
        ms = jnp.mean(o * o, axis=-1, keepdims=True)
        o = o * lax.rsqrt(ms + EPS) * subln_ref[...] * (1.0 - lam_init)
        g = g_ref[0].astype(F32)
        o_ref[0] = (o * _silu(g)).astype(o_ref.dtype)

    @pl.when(safe)
    def _():
        attend(False)

    @pl.when(jnp.logical_not(safe))
    def _():
        attend(True)


def _attention(p1, lvec, subln, *, n_lat, heads, lam_init):
    bsz, r, _ = p1.shape
    hd = 2 * DA_HEAD
    tq = min(1024, n_lat)
    kern = functools.partial(_attn_kernel, lam_init=lam_init)
    return pl.pallas_call(
        kern,
        out_shape=jax.ShapeDtypeStruct((bsz, n_lat, heads * hd), BF16),
        grid=(bsz, heads, n_lat // tq),
        in_specs=[
            pl.BlockSpec((1, tq, hd), lambda b, h, i: (b, i, h)),
            pl.BlockSpec((1, r, hd), lambda b, h, i: (b, 0, heads + h)),
            pl.BlockSpec((1, r, hd), lambda b, h, i: (b, 0, 2 * heads + h)),
            pl.BlockSpec((1, tq, hd), lambda b, h, i: (b, i, 3 * heads + h)),
            pl.BlockSpec(lvec.shape, lambda b, h, i: (0, 0)),
            pl.BlockSpec((1, hd), lambda b, h, i: (0, 0)),
        ],
        out_specs=pl.BlockSpec((1, tq, hd), lambda b, h, i: (b, i, h)),
        scratch_shapes=[
            pltpu.VMEM((hd + 2 * SUBLANES, r), BF16),
            pltpu.VMEM((SUBLANES, LANES), F32),
            pltpu.VMEM((2, 2, ATTN_KEY_CHUNK, tq), F32),
        ],
        compiler_params=_cparams(("arbitrary", "arbitrary", "arbitrary")),
        name="diff_attention",
    )(p1, p1, p1, p1, lvec, subln)


def _outproj1_kernel(o_ref, x_ref, modv_ref, w_ref, out_ref, *, d):
    b = pl.program_id(0)
    acc = _dot(o_ref[0], w_ref[...])
    gate = modv_ref[pl.ds(b, 1), :][:, 2 * d:3 * d]
    out_ref[0] = x_ref[0] + gate * acc


def _outproj1(o, xc, modv, w):
    bsz, n, dv = o.shape
    d = xc.shape[2]
    tm = ROW_TILE
    return pl.pallas_call(
        functools.partial(_outproj1_kernel, d=d),
        out_shape=jax.ShapeDtypeStruct((bsz, n, d), F32),
        grid=(bsz, n // tm),
        in_specs=[
            pl.BlockSpec((1, tm, dv), lambda b, i: (b, i, 0)),
            pl.BlockSpec((1, tm, d), lambda b, i: (b, i, 0)),
            pl.BlockSpec(modv.shape, lambda b, i: (0, 0)),
            pl.BlockSpec(w.shape, lambda b, i: (0, 0)),
        ],
        out_specs=pl.BlockSpec((1, tm, d), lambda b, i: (b, i, 0)),
        compiler_params=_cparams(("arbitrary", "arbitrary")),
        name="outproj_odd",
    )(o, xc, modv, w)


def _rope_tables(n_lat, n_ctx):
    quarter = DA_HEAD // 4
    pos = np.arange(n_lat)
    freqs = ROPE_BASE ** (-np.arange(quarter) / quarter)
    ar = (pos // GRID_W)[:, None] * freqs[None, :]
    ac = (pos % GRID_W)[:, None] * freqs[None, :]
    cos = np.concatenate([np.cos(ar), np.cos(ar), np.cos(ac), np.cos(ac)], axis=-1)
    sin = np.concatenate([-np.sin(ar), np.sin(ar), -np.sin(ac), np.sin(ac)], axis=-1)
    cos = np.concatenate([cos, cos], axis=-1)
    sin = np.concatenate([sin, sin], axis=-1)
    cos = np.concatenate([cos, np.ones((n_ctx, 2 * DA_HEAD))], axis=0)
    sin = np.concatenate([sin, np.zeros((n_ctx, 2 * DA_HEAD))], axis=0)
    return jnp.asarray(cos, F32), jnp.asarray(sin, F32)


def kernel(x, c, ctx, c_ctx, mod_w, mod_b, norm_w, ev_in_w, ev_out_w, hy_conv_w, hy_conv_b, hy_w1, hy_b1, hy_w2, hy_b2, hy_w3, hy_b3, hy_freq, hy_skip, s5_a_re, s5_a_im, s5_log_dt, s5_b_re, s5_b_im, s5_c_re, s5_c_im, s5_d, s5_glu_w, s5_glu_b, od_in_w, od_out_w, da_q_norm, da_k_norm, da_lq1, da_lk1, da_lq2, da_lk2, da_subln):
    bsz, n, d = x.shape
    nc = ctx.shape[1]
    assert n % ROW_TILE == 0 and nc % ROW_TILE == 0
    depth = mod_w.shape[0]
    assert depth == 2

    npad = SUBLANES * ((bsz + 1 + SUBLANES - 1) // SUBLANES)
    cvec = jnp.concatenate([c, c_ctx[None, :], jnp.zeros((npad - bsz - 1, d), F32)], axis=0)
    modv = _mod_vectors(cvec, mod_w, mod_b)

    hw = hy_skip.shape[-1]
    sw = s5_d.shape[-1]
    p0 = _inproj0(x, ctx, modv[0], norm_w[0:1], ev_in_w[0].astype(BF16))
    sc = _shortconv(p0, hy_conv_w[0].astype(F32), hy_conv_b[0].astype(F32), n)
    filt = (hy_w1[0], hy_b1[0], hy_w2[0], hy_b2[0], hy_w3[0], hy_b3[0], hy_freq[0])
    hy_l = _hyena_seq(sc, n, 0, filt, hy_skip[0], hw)
    hy_c = _hyena_seq(sc, nc, n // nc, filt, hy_skip[0], hw)

    s5_ops = _s5_operators(s5_a_re[0], s5_a_im[0], s5_log_dt[0], s5_b_re[0], s5_b_im[0], s5_c_re[0], s5_c_im[0])
    su_blk = ((HY_ORDER + 2) * hw) // sw
    yssm = _s5_ssm(p0, *s5_ops, n_lat=n, n_ctx=nc, col_blk0=su_blk * (sw // LANES))
    x1 = _outproj0(hy_l, hy_c, p0, yssm, s5_d[0], s5_glu_w[0], s5_glu_b[0], x, ctx, modv[0],
                   ev_out_w[0].astype(BF16), hg_col_blk=(HY_ORDER + 1), u_col_blk=su_blk,
                   sg_col_blk=su_blk + 1)

    heads = d // (2 * DA_HEAD)
    reps = d // DA_HEAD
    qscale = DA_HEAD ** -0.5 * math.log2(math.e)
    qkw = jnp.stack([jnp.tile(da_q_norm[0].astype(F32), reps) * qscale, jnp.tile(da_k_norm[0].astype(F32), reps)])
    qkw = jnp.concatenate([qkw, jnp.zeros((SUBLANES - 2, d), F32)], axis=0)
    gidx = np.arange(2 * LANES) // DA_HEAD
    gm = jnp.asarray(((gidx[:, None] == gidx[None, :]) * (1.0 / DA_HEAD)).astype(BF16))
    cos_t, sin_t = _rope_tables(n, nc)
    lidx = np.arange(2 * LANES)
    pm = jnp.asarray((lidx[:, None] == (lidx[None, :] ^ (DA_HEAD // 4))).astype(BF16))
    p1 = _inproj1(x1, modv[1], norm_w[1:2], od_in_w[0].astype(BF16), qkw, gm, pm, cos_t, sin_t, n)
    lam_init = 0.8 - 0.6 * math.exp(-0.3 * 1)
    lvec = jnp.stack([da_lq1[0], da_lk1[0], da_lq2[0], da_lk2[0]]).astype(F32)
    lvec = jnp.pad(lvec, ((0, SUBLANES - 4), (0, LANES - lvec.shape[1])))
    o = _attention(p1, lvec, da_subln[0].reshape(1, 2 * DA_HEAD).astype(F32), n_lat=n, heads=heads,
                   lam_init=lam_init)
    return _outproj1(o, x1, modv[1], od_out_w[0].astype(BF16))
```
